```python
import jax, jax.numpy as jnp
from jax import lax
import numpy as np

D_MODEL = 1024
BATCH = 16
SEQ = 2048
DEPTH = 2

N_A_LAYERS = DEPTH // 2
N_B_LAYERS = DEPTH - N_A_LAYERS
N_DENSE = (DEPTH + 1) // 2
N_MOE = DEPTH // 2

ROPE_THETA = 10000.0
LN_EPS = 1e-5
DN_ALPHA = (2.0 * DEPTH) ** 0.25
DN_BETA = (8.0 * DEPTH) ** -0.25

RET_HEADS = 4
RET_DK = D_MODEL // RET_HEADS
RET_DV = 2 * RET_DK
RET_CHUNK = 128

NSA_HEADS = 16
NSA_GROUPS = 4
NSA_HD = D_MODEL // NSA_HEADS
CMP_LEN = 32
CMP_STRIDE = 16
CMP_HID = 4 * NSA_HD
SLC_LEN = 64
SLC_TOPK = 8
WIN = 512
NSA_QBLK = 32
FORCE_SCORE = 1e9

D_FF = 2816
N_EXPERTS = 8
TOP_K = 2
D_FF_E = 3584

kernel_name = 'yoco_retnet_nsa_moe_block'


def rotary(x, pos):
    d = x.shape[-1]
    inv = 1.0 / (ROPE_THETA ** (jnp.arange(0, d, 2, dtype=jnp.float32) / d))
    ang = pos.astype(jnp.float32)[:, None] * inv[None, :]
    cos, sin = jnp.cos(ang), jnp.sin(ang)
    xf = x.astype(jnp.float32)
    x1, x2 = xf[..., : d // 2], xf[..., d // 2:]
    return jnp.concatenate([x1 * cos - x2 * sin, x1 * sin + x2 * cos], axis=-1).astype(x.dtype)


def layer_norm(x, g, b):
    xf = x.astype(jnp.float32)
    mu = jnp.mean(xf, -1, keepdims=True)
    var = jnp.mean(jnp.square(xf - mu), -1, keepdims=True)
    return ((xf - mu) * lax.rsqrt(var + LN_EPS) * g + b).astype(x.dtype)


def masked_softmax(s, valid):
    s = jnp.where(valid, s.astype(jnp.float32), -jnp.inf)
    m = jnp.max(s, -1, keepdims=True)
    m = jnp.where(jnp.isfinite(m), m, 0.0)
    e = jnp.where(valid, jnp.exp(s - m), 0.0)
    return e / jnp.maximum(jnp.sum(e, -1, keepdims=True), 1e-30)


def swiglu(x, w_in, w_out):
    a, b = jnp.split(x @ w_in, 2, axis=-1)
    return (jax.nn.silu(a) * b) @ w_out


def retention(x, w_in, gn_g, w_out):
    B, S, _ = x.shape
    H, dk, dv, C = RET_HEADS, RET_DK, RET_DV, RET_CHUNK
    N = S // C
    proj = x @ w_in
    q, k, v, g = jnp.split(proj, [H * dk, 2 * H * dk, 2 * H * dk + H * dv], axis=-1)
    pos = jnp.arange(S)
    q = rotary(q.reshape(B, S, H, dk).transpose(0, 2, 1, 3), pos)
    k = rotary(k.reshape(B, S, H, dk).transpose(0, 2, 1, 3), pos) * dk ** -0.5
    v = v.reshape(B, S, H, dv).transpose(0, 2, 1, 3)
    log_gamma = jnp.log1p(-(2.0 ** (-5.0 - jnp.arange(H, dtype=jnp.float32))))
    i = jnp.arange(C, dtype=jnp.float32)
    rel = i[:, None] - i[None, :]
    intra_decay = jnp.where(rel >= 0, jnp.exp(log_gamma[:, None, None] * jnp.maximum(rel, 0.0)), 0.0)
    q_decay = jnp.exp(log_gamma[:, None] * (i + 1.0))[None, :, :, None]
    k_decay = jnp.exp(log_gamma[:, None] * (C - 1.0 - i))[None, :, :, None]
    chunk_decay = jnp.exp(log_gamma * C)[None, :, None, None]

    def to_chunks(t):
        return t.reshape(B, H, N, C, t.shape[-1]).transpose(2, 0, 1, 3, 4)

    def step(state, inp):
        qi, ki, vi = inp
        scores = jnp.einsum('bhqd,bhkd->bhqk', qi, ki) * intra_decay
        inner = jnp.einsum('bhqk,bhkv->bhqv', scores, vi)
        cross = jnp.einsum('bhqd,bhdv->bhqv', qi, state) * q_decay
        new_state = state * chunk_decay + jnp.einsum('bhkd,bhkv->bhdv', ki * k_decay, vi)
        return new_state, inner + cross

    state0 = jnp.zeros((B, H, dk, dv), jnp.float32)
    _, out = lax.scan(step, state0, (to_chunks(q), to_chunks(k), to_chunks(v)))
    out = out.transpose(1, 2, 0, 3, 4).reshape(B, H, S, dv).astype(jnp.float32)
    mu = jnp.mean(out, -1, keepdims=True)
    var = jnp.mean(jnp.square(out - mu), -1, keepdims=True)
    normed = (out - mu) * lax.rsqrt(var + LN_EPS) * gn_g.reshape(H, 1, dv)
    normed = normed.transpose(0, 2, 1, 3).reshape(B, S, H * dv)
    y = jax.nn.silu(g.astype(jnp.float32)) * normed
    return (y @ w_out).astype(x.dtype)


def nsa_shared_kv(h, w_kv, ck_pe, ck_w1, ck_w2, cv_pe, cv_w1, cv_w2):
    B, S, _ = h.shape
    G, dh = NSA_GROUPS, NSA_HD
    kv = (h @ w_kv).reshape(B, S, 6, G, dh).transpose(2, 0, 3, 1, 4)
    pos = jnp.arange(S)
    k_cmp, v_cmp = rotary(kv[0], pos), kv[1]
    k_slc, v_slc = rotary(kv[2], pos), kv[3]
    k_win, v_win = rotary(kv[4], pos), kv[5]
    n_cmp = (S - CMP_LEN) // CMP_STRIDE + 1
    idx = jnp.arange(n_cmp)[:, None] * CMP_STRIDE + jnp.arange(CMP_LEN)[None, :]

    def compress(t, pe, w1, w2):
        blocks = t[:, :, idx, :] + pe
        flat = blocks.reshape(B, G, n_cmp, CMP_LEN * dh)
        return jax.nn.gelu(flat @ w1) @ w2

    k_c = compress(k_cmp, ck_pe, ck_w1, ck_w2)
    v_c = compress(v_cmp, cv_pe, cv_w1, cv_w2)
    n_slc = S // SLC_LEN
    k_s = k_slc.reshape(B, G, n_slc, SLC_LEN, dh)
    v_s = v_slc.reshape(B, G, n_slc, SLC_LEN, dh)
    pad = ((0, 0), (0, 0), (WIN, 0), (0, 0))
    k_w = jnp.pad(k_win, pad)
    v_w = jnp.pad(v_win, pad)
    return (k_c, v_c, k_s, v_s, k_w, v_w)


def nsa_attention(x, w_q, w_out, k_c, v_c, k_s, v_s, k_w, v_w):
    B, S, _ = x.shape
    H, G, dh = NSA_HEADS, NSA_GROUPS, NSA_HD
    R = H // G
    Q = NSA_QBLK
    proj = x @ w_q
    q = proj[..., : H * dh].reshape(B, S, G, R, dh).transpose(0, 2, 3, 1, 4)
    q = rotary(q, jnp.arange(S)) * dh ** -0.5
    gates = jax.nn.sigmoid(proj[..., H * dh:].astype(jnp.float32))
    gates = gates.reshape(B, S, G, R, 3).transpose(0, 2, 3, 1, 4)
    n_cmp = k_c.shape[2]
    n_slc = k_s.shape[2]
    n_top = min(SLC_TOPK, n_slc)
    cmp_start = jnp.arange(n_cmp) * CMP_STRIDE
    cmp_end = cmp_start + CMP_LEN - 1
    slc_start = jnp.arange(n_slc) * SLC_LEN
    overlap = jnp.clip(jnp.minimum(cmp_start[:, None] + CMP_LEN, slc_start[None, :] + SLC_LEN)
                       - jnp.maximum(cmp_start[:, None], slc_start[None, :]), 0).astype(jnp.float32) / CMP_STRIDE
    j_idx = jnp.arange(n_slc)
    bi = jnp.arange(B)[:, None, None, None]
    gi = jnp.arange(G)[None, :, None, None]

    def block(i):
        s0 = i * Q
        t = s0 + jnp.arange(Q)
        qb = lax.dynamic_slice_in_dim(q, s0, Q, axis=3)
        gb = lax.dynamic_slice_in_dim(gates, s0, Q, axis=3)
        sc = jnp.einsum('bgrqd,bgnd->bgrqn', qb, k_c)
        p_cmp = masked_softmax(sc, cmp_end[None, :] <= t[:, None])
        o_cmp = jnp.einsum('bgrqn,bgnd->bgrqd', p_cmp, v_c)
        imp = jnp.einsum('bgrqn,nj->bgqj', p_cmp, overlap)
        cur = t // SLC_LEN
        forced = (j_idx[None, :] == 0) | (j_idx[None, :] == cur[:, None]) | (j_idx[None, :] == cur[:, None] - 1)
        imp = jnp.where(forced, FORCE_SCORE, imp)
        imp = jnp.where(j_idx[None, :] <= cur[:, None], imp, -jnp.inf)
        top_val, top_idx = lax.top_k(imp, n_top)
        sel_ok = jnp.isfinite(top_val)
        ksel = k_s[bi, gi, top_idx]
        vsel = v_s[bi, gi, top_idx]
        tok = top_idx[..., None] * SLC_LEN + jnp.arange(SLC_LEN)
        valid_s = sel_ok[..., None] & (tok <= t[None, None, :, None, None])
        ss = jnp.einsum('bgrqd,bgqkld->bgrqkl', qb, ksel).reshape(B, G, R, Q, n_top * SLC_LEN)
        p_s = masked_softmax(ss, valid_s.reshape(B, G, 1, Q, n_top * SLC_LEN))
        o_slc = jnp.einsum('bgrqkl,bgqkld->bgrqd', p_s.reshape(B, G, R, Q, n_top, SLC_LEN), vsel)
        kw = lax.dynamic_slice_in_dim(k_w, s0, Q + WIN, axis=2)
        vw = lax.dynamic_slice_in_dim(v_w, s0, Q + WIN, axis=2)
        kpos = s0 - WIN + jnp.arange(Q + WIN)
        valid_w = (kpos[None, :] <= t[:, None]) & (kpos[None, :] > t[:, None] - WIN) & (kpos[None, :] >= 0)
        sw = jnp.einsum('bgrqd,bgkd->bgrqk', qb, kw)
        o_win = jnp.einsum('bgrqk,bgkd->bgrqd', masked_softmax(sw, valid_w), vw)
        return gb[..., 0:1] * o_cmp + gb[..., 1:2] * o_slc + gb[..., 2:3] * o_win

    out = lax.map(block, jnp.arange(S // Q))
    out = out.transpose(1, 0, 4, 2, 3, 5).reshape(B, S, H * dh)
    return (out @ w_out).astype(x.dtype)


def moe_swiglu(x, w_router, w_in, w_out):
    B, S, D = x.shape
    xf = x.reshape(B * S, D)
    logits = (xf @ w_router).astype(jnp.float32)
    top_v, top_i = lax.top_k(logits, TOP_K)
    w = jax.nn.softmax(top_v, axis=-1)
    gate = jnp.sum(jax.nn.one_hot(top_i, N_EXPERTS, dtype=jnp.float32) * w[..., None], axis=1)
    out = jnp.zeros((B * S, D), jnp.float32)
    for e in range(N_EXPERTS):
        out = out + gate[:, e:e + 1] * swiglu(xf, w_in[e], w_out[e])
    return out.reshape(B, S, D).astype(x.dtype)


def setup_inputs(seed: int = 0) -> dict:
    key = jax.random.key(seed)
    ks = jax.random.split(key, 20)

    def nrm(k, shape, scale):
        return jax.random.normal(k, shape, jnp.float32) * scale

    ret_qk = RET_HEADS * RET_DK
    ret_v = RET_HEADS * RET_DV
    nsa_q = NSA_HEADS * NSA_HD
    return {
        'x': nrm(ks[0], (BATCH, SEQ, D_MODEL), 1.0),
        'ret_w_in': nrm(ks[1], (N_A_LAYERS, D_MODEL, 2 * ret_qk + 2 * ret_v), D_MODEL ** -0.5),
        'ret_gn_g': 1.0 + nrm(ks[2], (N_A_LAYERS, ret_v), 0.02),
        'ret_w_out': nrm(ks[3], (N_A_LAYERS, ret_v, D_MODEL), DN_BETA * ret_v ** -0.5),
        'nsa_w_kv': nrm(ks[4], (D_MODEL, 6 * NSA_GROUPS * NSA_HD), D_MODEL ** -0.5),
        'cmp_k_pe': nrm(ks[5], (CMP_LEN, NSA_HD), 0.1),
        'cmp_k_w1': nrm(ks[6], (CMP_LEN * NSA_HD, CMP_HID), (CMP_LEN * NSA_HD) ** -0.5),
        'cmp_k_w2': nrm(ks[7], (CMP_HID, NSA_HD), CMP_HID ** -0.5),
        'cmp_v_pe': nrm(ks[8], (CMP_LEN, NSA_HD), 0.1),
        'cmp_v_w1': nrm(ks[9], (CMP_LEN * NSA_HD, CMP_HID), (CMP_LEN * NSA_HD) ** -0.5),
        'cmp_v_w2': nrm(ks[10], (CMP_HID, NSA_HD), CMP_HID ** -0.5),
        'nsa_w_q': nrm(ks[11], (N_B_LAYERS, D_MODEL, nsa_q + 3 * NSA_HEADS), D_MODEL ** -0.5),
        'nsa_w_out': nrm(ks[12], (N_B_LAYERS, nsa_q, D_MODEL), DN_BETA * nsa_q ** -0.5),
        'ffn_w_in': nrm(ks[13], (N_DENSE, D_MODEL, 2 * D_FF), D_MODEL ** -0.5),
        'ffn_w_out': nrm(ks[14], (N_DENSE, D_FF, D_MODEL), DN_BETA * D_FF ** -0.5),
        'moe_router': nrm(ks[15], (N_MOE, D_MODEL, N_EXPERTS), D_MODEL ** -0.5),
        'moe_w_in': nrm(ks[16], (N_MOE, N_EXPERTS, D_MODEL, 2 * D_FF_E), D_MODEL ** -0.5),
        'moe_w_out': nrm(ks[17], (N_MOE, N_EXPERTS, D_FF_E, D_MODEL), DN_BETA * D_FF_E ** -0.5),
        'ln_g': 1.0 + nrm(ks[18], (DEPTH, 2, D_MODEL), 0.02),
        'ln_b': nrm(ks[19], (DEPTH, 2, D_MODEL), 0.02),
    }


def reference(x, ret_w_in, ret_gn_g, ret_w_out, nsa_w_kv, cmp_k_pe, cmp_k_w1, cmp_k_w2,
              cmp_v_pe, cmp_v_w1, cmp_v_w2, nsa_w_q, nsa_w_out, ffn_w_in, ffn_w_out,
              moe_router, moe_w_in, moe_w_out, ln_g, ln_b):
    shared = None
    for l in range(DEPTH):
        if l < N_A_LAYERS:
            mix = retention(x, ret_w_in[l], ret_gn_g[l], ret_w_out[l])
        else:
            if l == N_A_LAYERS:
                shared = nsa_shared_kv(x, nsa_w_kv, cmp_k_pe, cmp_k_w1, cmp_k_w2,
                                       cmp_v_pe, cmp_v_w1, cmp_v_w2)
            jb = l - N_A_LAYERS
            mix = nsa_attention(x, nsa_w_q[jb], nsa_w_out[jb], *shared)
        x = layer_norm(DN_ALPHA * x + mix, ln_g[l, 0], ln_b[l, 0])
        if l % 2 == 0:
            f = swiglu(x, ffn_w_in[l // 2], ffn_w_out[l // 2])
        else:
            f = moe_swiglu(x, moe_router[l // 2], moe_w_in[l // 2], moe_w_out[l // 2])
        x = layer_norm(DN_ALPHA * x + f, ln_g[l, 1], ln_b[l, 1])
    return x
```

```python
import functools

import jax
import jax.numpy as jnp
from jax import lax
from jax.experimental import pallas as pl
from jax.experimental.pallas import tpu as pltpu

F32 = jnp.float32
BF16 = jnp.bfloat16

DEPTH = 2
ROPE_THETA = 10000.0
LN_EPS = 1e-5
DN_ALPHA = (2.0 * DEPTH) ** 0.25

RET_HEADS = 4
RET_CHUNK = 128

NSA_HEADS = 16
NSA_GROUPS = 4
NSA_REP = NSA_HEADS // NSA_GROUPS
NSA_HD = 64
CMP_LEN = 32
CMP_STRIDE = 16
SLC_LEN = 64
SLC_TOPK = 8
WIN = 512
FORCE_SCORE = 1e9

N_EXPERTS = 8

LANES = 128
MASK_NEG = -1e30
VMEM_LIMIT = 56 * 1024 * 1024


def _params(*sem):
    return pltpu.CompilerParams(dimension_semantics=sem, vmem_limit_bytes=VMEM_LIMIT)


def _dot(a, b):
    return jnp.dot(a, b, preferred_element_type=F32)


def _dot_nt(a, b):
    return lax.dot_general(a, b, (((1,), (1,)), ((), ())), preferred_element_type=F32)


def _dot_tn(a, b):
    return lax.dot_general(a, b, (((0,), (0,)), ((), ())), preferred_element_type=F32)


def _layer_norm(z, g, b):
    mu = jnp.mean(z, -1, keepdims=True)
    zc = z - mu
    var = jnp.mean(zc * zc, -1, keepdims=True)
    return zc * lax.rsqrt(var + LN_EPS) * g + b


def _silu(a):
    return a * jax.nn.sigmoid(a)


def _rope_tables(seq, dim, width):
    inv = 1.0 / (ROPE_THETA ** (jnp.arange(0, dim, 2, dtype=F32) / dim))
    ang = jnp.arange(seq, dtype=F32)[:, None] * inv[None, :]
    cos, sin = jnp.cos(ang), jnp.sin(ang)
    cos_h = jnp.concatenate([cos, cos], -1)
    sin_h = jnp.concatenate([-sin, sin], -1)
    rep = width // dim
    return jnp.tile(cos_h, (1, rep)), jnp.tile(sin_h, (1, rep))


def _ret_proj_kernel(x_ref, w_ref, cos_ref, sin_ref, o_ref, *, dk, n_rot):
    j = pl.program_id(1)
    acc = _dot(x_ref[...], w_ref[...])
    tn = acc.shape[1]
    half = dk // 2

    @pl.when(j < n_rot)
    def _():
        scale = jnp.where(j == n_rot - 1, dk ** -0.5, 1.0).astype(F32)
        cos = cos_ref[...] * scale
        sin = sin_ref[...] * scale
        for h in range(tn // dk):
            a = acc[:, h * dk:h * dk + half]
            b = acc[:, h * dk + half:(h + 1) * dk]
            o_ref[:, h * dk:h * dk + half] = (a * cos - b * sin).astype(o_ref.dtype)
            o_ref[:, h * dk + half:(h + 1) * dk] = (a * sin + b * cos).astype(o_ref.dtype)

    @pl.when(j >= n_rot)
    def _():
        o_ref[...] = acc.astype(o_ref.dtype)


def _ret_proj(xb, w, cos, sin, seq, dk, tm):
    T, D = xb.shape
    N = w.shape[1]
    tn = RET_HEADS * dk
    half = dk // 2
    n_pos = seq // tm
    return pl.pallas_call(
        functools.partial(_ret_proj_kernel, dk=dk, n_rot=2),
        grid=(T // tm, N // tn),
        in_specs=[
            pl.BlockSpec((tm, D), lambda i, j: (i, 0)),
            pl.BlockSpec((D, tn), lambda i, j: (0, j)),
            pl.BlockSpec((tm, half), lambda i, j: (i % n_pos, 0)),
            pl.BlockSpec((tm, half), lambda i, j: (i % n_pos, 0)),
        ],
        out_specs=pl.BlockSpec((tm, tn), lambda i, j: (i, j)),
        out_shape=jax.ShapeDtypeStruct((T, N), BF16),
        compiler_params=_params("parallel", "arbitrary"),
    )(xb, w, cos, sin)


def _retention_kernel(q_ref, k_ref, v_ref, g_ref, dec_ref, qd_ref, kd_ref, gn_ref, o_ref, state_ref,
                      *, chunk, n_chunks):
    @pl.when(pl.program_id(2) == 0)
    def _():
        state_ref[...] = jnp.zeros_like(state_ref)

    decay = dec_ref[0]
    qd = qd_ref[0]
    kd = kd_ref[0]
    cd = qd[chunk - 1:chunk, :]
    gn = gn_ref[0]
    for c in range(n_chunks):
        rows = slice(c * chunk, (c + 1) * chunk)
        q = q_ref[rows, :]
        k = k_ref[rows, :]
        v = v_ref[rows, :]
        scores = _dot_nt(q, k) * decay
        inner = _dot(scores.astype(BF16), v)
        state = state_ref[...]
        cross = _dot(q, state.astype(BF16)) * qd
        out = inner + cross
        k_dec = (k.astype(F32) * kd).astype(BF16)
        state_ref[...] = state * cd + _dot_tn(k_dec, v)
        mu = jnp.mean(out, -1, keepdims=True)
        oc = out - mu
        var = jnp.mean(oc * oc, -1, keepdims=True)
        normed = oc * lax.rsqrt(var + LN_EPS) * gn
        o_ref[rows, :] = (_silu(g_ref[rows, :].astype(F32)) * normed).astype(o_ref.dtype)


def _retention(qkvg, gn_g, batch, seq, dk, dv, rows):
    H, C = RET_HEADS, RET_CHUNK
    T = batch * seq
    n_steps = seq // rows
    log_gamma = jnp.log1p(-(2.0 ** (-5.0 - jnp.arange(H, dtype=F32))))
    i = jnp.arange(C, dtype=F32)
    rel = i[:, None] - i[None, :]
    intra = jnp.where(rel >= 0, jnp.exp(log_gamma[:, None, None] * jnp.maximum(rel, 0.0)), 0.0)
    q_decay = jnp.exp(log_gamma[:, None] * (i + 1.0))[:, :, None]
    k_decay = jnp.exp(log_gamma[:, None] * (C - 1.0 - i))[:, :, None]
    v_off = (2 * H * dk) // dv
    g_off = (2 * H * dk + H * dv) // dv
    row = lambda b, h, n: b * n_steps + n
    return pl.pallas_call(
        functools.partial(_retention_kernel, chunk=C, n_chunks=rows // C),
        grid=(batch, H, n_steps),
        in_specs=[
            pl.BlockSpec((rows, dk), lambda b, h, n: (row(b, h, n), h)),
            pl.BlockSpec((rows, dk), lambda b, h, n: (row(b, h, n), H + h)),
            pl.BlockSpec((rows, dv), lambda b, h, n: (row(b, h, n), v_off + h)),
            pl.BlockSpec((rows, dv), lambda b, h, n: (row(b, h, n), g_off + h)),
            pl.BlockSpec((1, C, C), lambda b, h, n: (h, 0, 0)),
            pl.BlockSpec((1, C, 1), lambda b, h, n: (h, 0, 0)),
            pl.BlockSpec((1, C, 1), lambda b, h, n: (h, 0, 0)),
            pl.BlockSpec((1, 1, dv), lambda b, h, n: (h, 0, 0)),
        ],
        out_specs=pl.BlockSpec((rows, dv), lambda b, h, n: (row(b, h, n), h)),
        out_shape=jax.ShapeDtypeStruct((T, H * dv), BF16),
        scratch_shapes=[pltpu.VMEM((dk, dv), F32)],
        compiler_params=_params("parallel", "parallel", "arbitrary"),
    )(qkvg, qkvg, qkvg, qkvg, intra, q_decay, k_decay, gn_g.reshape(H, 1, dv))


def _proj_ln_kernel(y_ref, w_ref, x_ref, g_ref, b_ref, o_ref, ob_ref):
    z = DN_ALPHA * x_ref[...] + _dot(y_ref[...], w_ref[...])
    o = _layer_norm(z, g_ref[...], b_ref[...])
    o_ref[...] = o
    ob_ref[...] = o.astype(BF16)


def _proj_ln(y, w, x, g, b, tm):
    T, K = y.shape
    D = w.shape[1]
    return pl.pallas_call(
        _proj_ln_kernel,
        grid=(T // tm,),
        in_specs=[
            pl.BlockSpec((tm, K), lambda i: (i, 0)),
            pl.BlockSpec((K, D), lambda i: (0, 0)),
            pl.BlockSpec((tm, D), lambda i: (i, 0)),
            pl.BlockSpec((1, D), lambda i: (0, 0)),
            pl.BlockSpec((1, D), lambda i: (0, 0)),
        ],
        out_specs=[pl.BlockSpec((tm, D), lambda i: (i, 0)), pl.BlockSpec((tm, D), lambda i: (i, 0))],
        out_shape=[jax.ShapeDtypeStruct((T, D), F32), jax.ShapeDtypeStruct((T, D), BF16)],
        compiler_params=_params("parallel"),
    )(y, w, x, g.reshape(1, D), b.reshape(1, D))


def _ffn_kernel(xb_ref, wa_ref, wb_ref, wo_ref, x_ref, g_ref, b_ref, o_ref, ob_ref, acc_ref):
    f = pl.program_id(1)
    xb = xb_ref[...]
    h = (_silu(_dot(xb, wa_ref[...])) * _dot(xb, wb_ref[...])).astype(BF16)
    contrib = _dot(h, wo_ref[...])

    @pl.when(f == 0)
    def _():
        acc_ref[...] = contrib

    @pl.when(f > 0)
    def _():
        acc_ref[...] += contrib

    @pl.when(f == pl.num_programs(1) - 1)
    def _():
        o = _layer_norm(DN_ALPHA * x_ref[...] + acc_ref[...], g_ref[...], b_ref[...])
        o_ref[...] = o
        ob_ref[...] = o.astype(BF16)


def _ffn(xb, w_in, w_out, x, g, b, tm, tf):
    T, D = xb.shape
    F = w_out.shape[0]
    nf = F // tf
    return pl.pallas_call(
        _ffn_kernel,
        grid=(T // tm, nf),
        in_specs=[
            pl.BlockSpec((tm, D), lambda i, f: (i, 0)),
            pl.BlockSpec((D, tf), lambda i, f: (0, f)),
            pl.BlockSpec((D, tf), lambda i, f: (0, nf + f)),
            pl.BlockSpec((tf, D), lambda i, f: (f, 0)),
            pl.BlockSpec((tm, D), lambda i, f: (i, 0)),
            pl.BlockSpec((1, D), lambda i, f: (0, 0)),
            pl.BlockSpec((1, D), lambda i, f: (0, 0)),
        ],
        out_specs=[pl.BlockSpec((tm, D), lambda i, f: (i, 0)), pl.BlockSpec((tm, D), lambda i, f: (i, 0))],
        out_shape=[jax.ShapeDtypeStruct((T, D), F32), jax.ShapeDtypeStruct((T, D), BF16)],
        scratch_shapes=[pltpu.VMEM((tm, D), F32)],
        compiler_params=_params("parallel", "arbitrary"),
    )(xb, w_in, w_in, w_out, x, g.reshape(1, D), b.reshape(1, D))


def _rope64(acc, cos, sin_signed):
    half = NSA_HD // 2
    lane = lax.broadcasted_iota(jnp.int32, acc.shape, 1)
    first = (lane % NSA_HD) < half
    rot = jnp.where(first, pltpu.roll(acc, LANES - half, 1), pltpu.roll(acc, half, 1))
    return acc * cos + rot * sin_signed


def _kv_proj_kernel(x_ref, w_ref, cos_ref, sin_ref, o_ref):
    j = pl.program_id(1)
    acc = _dot(x_ref[...], w_ref[...])

    @pl.when(j % 2 == 0)
    def _():
        for s in range(acc.shape[1] // LANES):
            cols = slice(s * LANES, (s + 1) * LANES)
            o_ref[:, cols] = _rope64(acc[:, cols], cos_ref[...], sin_ref[...]).astype(o_ref.dtype)

    @pl.when(j % 2 == 1)
    def _():
        o_ref[...] = acc.astype(o_ref.dtype)


def _kv_proj(xb, w, cos, sin, seq, tm):
    T, D = xb.shape
    N = w.shape[1]
    tn = NSA_GROUPS * NSA_HD
    n_pos = seq // tm
    return pl.pallas_call(
        _kv_proj_kernel,
        grid=(T // tm, N // tn),
        in_specs=[
            pl.BlockSpec((tm, D), lambda i, j: (i, 0)),
            pl.BlockSpec((D, tn), lambda i, j: (0, j)),
            pl.BlockSpec((tm, LANES), lambda i, j: (i % n_pos, 0)),
            pl.BlockSpec((tm, LANES), lambda i, j: (i % n_pos, 0)),
        ],
        out_specs=pl.BlockSpec((tm, tn), lambda i, j: (i, j)),
        out_shape=jax.ShapeDtypeStruct((T, N), BF16),
        compiler_params=_params("parallel", "arbitrary"),
    )(xb, w, cos, sin)


def _q_proj_kernel(x_ref, w_ref, cos_ref, sin_ref, q_ref, gate_ref, *, nq):
    acc = _dot(x_ref[...], w_ref[...])
    cos = cos_ref[...] * NSA_HD ** -0.5
    sin = sin_ref[...] * NSA_HD ** -0.5
    for s in range(nq // LANES):
        cols = slice(s * LANES, (s + 1) * LANES)
        q_ref[:, cols] = _rope64(acc[:, cols], cos, sin).astype(q_ref.dtype)
    gate_ref[...] = jax.nn.sigmoid(acc[:, nq:])


def _q_proj(xb, w_pad, cos, sin, seq, tm):
    T, D = xb.shape
    nq = NSA_HEADS * NSA_HD
    N = w_pad.shape[1]
    n_pos = seq // tm
    return pl.pallas_call(
        functools.partial(_q_proj_kernel, nq=nq),
        grid=(T // tm,),
        in_specs=[
            pl.BlockSpec((tm, D), lambda i: (i, 0)),
            pl.BlockSpec((D, N), lambda i: (0, 0)),
            pl.BlockSpec((tm, LANES), lambda i: (i % n_pos, 0)),
            pl.BlockSpec((tm, LANES), lambda i: (i % n_pos, 0)),
        ],
        out_specs=[pl.BlockSpec((tm, nq), lambda i: (i, 0)), pl.BlockSpec((tm, N - nq), lambda i: (i, 0))],
        out_shape=[jax.ShapeDtypeStruct((T, nq), BF16), jax.ShapeDtypeStruct((T, N - nq), F32)],
        compiler_params=_params("parallel"),
    )(xb, w_pad, cos, sin)


def _compress_kernel(a_ref, w1_ref, w2_ref, pe_ref, o_ref):
    a = a_ref[0]
    w1 = w1_ref[0]
    half = w1.shape[0] // 2
    first = _dot(a, w1[:half])
    second = _dot(a, w1[half:])
    bias = _dot(pe_ref[0], w1)[0:1]
    m = a.shape[0]
    hid = first + pltpu.roll(second, m - 1, 0) + bias
    o_ref[0] = _dot(jax.nn.gelu(hid).astype(BF16), w2_ref[0]).astype(o_ref.dtype)


def _compress(a, w1, w2, pe, tm):
    _, rows, width = a.shape
    hid = w1.shape[2]
    dh = w2.shape[2]
    return pl.pallas_call(
        _compress_kernel,
        grid=(2, rows // tm),
        in_specs=[
            pl.BlockSpec((1, tm, width), lambda s, i: (s, i, 0)),
            pl.BlockSpec((1, 2 * width, hid), lambda s, i: (s, 0, 0)),
            pl.BlockSpec((1, hid, dh), lambda s, i: (s, 0, 0)),
            pl.BlockSpec((1, 8, 2 * width), lambda s, i: (s, 0, 0)),
        ],
        out_specs=pl.BlockSpec((1, tm, dh), lambda s, i: (s, i, 0)),
        out_shape=jax.ShapeDtypeStruct((2, rows, dh), BF16),
        compiler_params=_params("parallel", "parallel"),
    )(a, w1, w2, pe)


def _nsa_kernel(q_ref, gt_ref, kc_ref, vc_ref, ks_ref, vs_ref, kw_ref, vw_ref, ov_ref, o_ref, s_ref,
                *, tq, n_slc, n_top):
    R, dh = NSA_REP, NSA_HD
    i = pl.program_id(2)
    s0 = pl.multiple_of(i * tq, tq)
    qb = q_ref[...]
    q_heads = [qb[:, r * dh:(r + 1) * dh] for r in range(R)]
    q4 = jnp.concatenate(q_heads, axis=0)
    t_col = s0 + lax.broadcasted_iota(jnp.int32, (tq, 1), 0)
    lane = lax.broadcasted_iota(jnp.int32, (tq, LANES), 1)

    n_c = kc_ref.shape[1]
    lane_c = lax.broadcasted_iota(jnp.int32, (tq, n_c), 1)
    valid_c = (lane_c * CMP_STRIDE + (CMP_LEN - 1) <= t_col)[None]
    sc = _dot_nt(q4, kc_ref[0]).reshape(R, tq, n_c)
    sc = jnp.where(valid_c, sc, -jnp.inf)
    mc = jnp.max(sc, -1, keepdims=True)
    mc = jnp.where(mc == -jnp.inf, 0.0, mc)
    ec = jnp.where(valid_c, jnp.exp(sc - mc), 0.0)
    p_cmp = ec / jnp.maximum(jnp.sum(ec, -1, keepdims=True), 1e-30)
    o_cmp = _dot(p_cmp.reshape(R * tq, n_c).astype(BF16), vc_ref[0])

    p_sum = jnp.sum(p_cmp, axis=0)
    p_hi = p_sum.astype(BF16)
    p_lo = (p_sum - p_hi.astype(F32)).astype(BF16)
    imp = _dot(p_hi, ov_ref[...]) + _dot(p_lo, ov_ref[...])
    cur = t_col // SLC_LEN
    forced = (lane == 0) | (lane == cur) | (lane == cur - 1)
    imp = jnp.where(forced, FORCE_SCORE, imp)
    imp = jnp.where(lane <= cur, imp, -jnp.inf)
    rank = jnp.zeros((tq, LANES), jnp.int32)
    for b in range(n_slc):
        col = imp[:, b:b + 1]
        beats = (col > imp) | ((col == imp) & (lane > b))
        rank = rank + beats.astype(jnp.int32)
    selected = (rank < n_top) & (lane <= cur)
    sel_bias = jnp.where(selected, 0.0, MASK_NEG).astype(BF16)[:, :dh]
    qp = jnp.concatenate([jnp.concatenate([qh, sel_bias], axis=1) for qh in q_heads], axis=0)

    def score_block(kb, m_run):
        k0 = pl.multiple_of(kb * tq, tq)
        s = _dot_nt(qp, ks_ref[0, pl.ds(k0, tq), :])
        s_ref[:, pl.ds(k0, tq)] = s
        return jnp.maximum(m_run, jnp.max(s, -1, keepdims=True))

    m_run = lax.fori_loop(0, i, score_block, jnp.full((R * tq, 1), MASK_NEG, F32))
    causal = (lax.broadcasted_iota(jnp.int32, (tq, tq), 1) <= lax.broadcasted_iota(jnp.int32, (tq, tq), 0))[None]
    s_diag = _dot_nt(qp, ks_ref[0, pl.ds(s0, tq), :]).reshape(R, tq, tq)
    s_diag = jnp.where(causal, s_diag, MASK_NEG).reshape(R * tq, tq)
    s_ref[:, pl.ds(s0, tq)] = s_diag
    m_sel = jnp.maximum(m_run, jnp.max(s_diag, -1, keepdims=True))

    def value_block(kb, carry):
        l_run, acc = carry
        k0 = pl.multiple_of(kb * tq, tq)
        p = jnp.exp(s_ref[:, pl.ds(k0, tq)] - m_sel)
        l_run = l_run + jnp.sum(p, -1, keepdims=True)
        acc = acc + _dot(p.astype(BF16), vs_ref[0, pl.ds(k0, tq), :])
        return l_run, acc

    l_sel, acc_sel = lax.fori_loop(0, i + 1, value_block,
                                   (jnp.zeros((R * tq, 1), F32), jnp.zeros((R * tq, dh), F32)))
    o_slc = acc_sel / l_sel

    n_kw = WIN + tq
    w0 = pl.multiple_of(jnp.maximum(i - WIN // tq, 0) * tq, tq)
    sw = _dot_nt(q4, kw_ref[0, pl.ds(w0, n_kw), :]).reshape(R, tq, n_kw)
    kpos = w0 + lax.broadcasted_iota(jnp.int32, (tq, n_kw), 1)
    valid_w = ((kpos <= t_col) & (kpos > t_col - WIN))[None]
    sw = jnp.where(valid_w, sw, MASK_NEG)
    pw = jnp.exp(sw - jnp.max(sw, -1, keepdims=True))
    l_win = jnp.sum(pw, -1, keepdims=True).reshape(R * tq, 1)
    o_win = _dot(pw.reshape(R * tq, n_kw).astype(BF16), vw_ref[0, pl.ds(w0, n_kw), :]) / l_win

    gates = gt_ref[0, 0]
    outs = []
    for r in range(R):
        rows = slice(r * tq, (r + 1) * tq)
        g = gates[r]
        outs.append(g[:, 0:1] * o_cmp[rows] + g[:, 1:2] * o_slc[rows] + g[:, 2:3] * o_win[rows])
    o_ref[...] = jnp.concatenate(outs, axis=1).astype(o_ref.dtype)


def _nsa_attention(q, gates, kc, vc, ks_aug, vs, kw, vw, overlap, batch, seq, tq):
    G, R, dh = NSA_GROUPS, NSA_REP, NSA_HD
    T = batch * seq
    nq = seq // tq
    n_slc = seq // SLC_LEN
    n_c = kc.shape[1]
    per_bg = lambda b, g, i: (b * G + g, 0, 0)
    return pl.pallas_call(
        functools.partial(_nsa_kernel, tq=tq, n_slc=n_slc, n_top=min(SLC_TOPK, n_slc)),
        grid=(batch, G, nq),
        in_specs=[
            pl.BlockSpec((tq, R * dh), lambda b, g, i: (b * nq + i, g)),
            pl.BlockSpec((1, 1, R, tq, 3), lambda b, g, i: (b, g, 0, i, 0)),
            pl.BlockSpec((1, n_c, dh), per_bg),
            pl.BlockSpec((1, n_c, dh), per_bg),
            pl.BlockSpec((1, seq, 2 * dh), per_bg),
            pl.BlockSpec((1, seq, dh), per_bg),
            pl.BlockSpec((1, seq, dh), per_bg),
            pl.BlockSpec((1, seq, dh), per_bg),
            pl.BlockSpec((n_c, LANES), lambda b, g, i: (0, 0)),
        ],
        out_specs=pl.BlockSpec((tq, R * dh), lambda b, g, i: (b * nq + i, g)),
        out_shape=jax.ShapeDtypeStruct((T, G * R * dh), BF16),
        scratch_shapes=[pltpu.VMEM((R * tq, seq), F32)],
        compiler_params=_params("parallel", "parallel", "arbitrary"),
    )(q, gates, kc, vc, ks_aug, vs, kw, vw, overlap)


def _router_kernel(x_ref, whi_ref, wlo_ref, gate_ref):
    x = x_ref[...]
    x_hi = x.astype(BF16)
    x_lo = (x - x_hi.astype(F32)).astype(BF16)
    logits = _dot(x_hi, whi_ref[...]) + _dot(x_hi, wlo_ref[...]) + _dot(x_lo, whi_ref[...])
    lane = lax.broadcasted_iota(jnp.int32, logits.shape, 1).astype(F32)
    logits = jnp.where(lane < N_EXPERTS, logits, -jnp.inf)
    m1 = jnp.max(logits, -1, keepdims=True)
    i1 = jnp.min(jnp.where(logits == m1, lane, float(LANES)), -1, keepdims=True)
    rest = jnp.where(lane == i1, -jnp.inf, logits)
    m2 = jnp.max(rest, -1, keepdims=True)
    i2 = jnp.min(jnp.where(rest == m2, lane, float(LANES)), -1, keepdims=True)
    e2 = jnp.exp(m2 - m1)
    den = 1.0 + e2
    gate_ref[...] = jnp.where(lane == i1, 1.0 / den, 0.0) + jnp.where(lane == i2, e2 / den, 0.0)


def _router(x, w_hi, w_lo, tm):
    T, D = x.shape
    return pl.pallas_call(
        _router_kernel,
        grid=(T // tm,),
        in_specs=[
            pl.BlockSpec((tm, D), lambda i: (i, 0)),
            pl.BlockSpec((D, LANES), lambda i: (0, 0)),
            pl.BlockSpec((D, LANES), lambda i: (0, 0)),
        ],
        out_specs=pl.BlockSpec((tm, LANES), lambda i: (i, 0)),
        out_shape=jax.ShapeDtypeStruct((T, LANES), F32),
        compiler_params=_params("parallel"),
    )(x, w_hi, w_lo)


def _moe_kernel(xb_ref, gate_ref, wa_ref, wb_ref, wo_ref, x_ref, g_ref, b_ref, o_ref, acc_ref):
    e = pl.program_id(1)
    f = pl.program_id(2)
    xb = xb_ref[...]
    gates = gate_ref[...]
    lane = lax.broadcasted_iota(jnp.int32, gates.shape, 1)
    gate_e = jnp.sum(jnp.where(lane == e, gates, 0.0), -1, keepdims=True)
    h = (_silu(_dot(xb, wa_ref[0])) * _dot(xb, wb_ref[0]) * gate_e).astype(BF16)
    contrib = _dot(h, wo_ref[0])
    first = (e == 0) & (f == 0)

    @pl.when(first)
    def _():
        acc_ref[...] = contrib

    @pl.when(jnp.logical_not(first))
    def _():
        acc_ref[...] += contrib

    @pl.when((e == pl.num_programs(1) - 1) & (f == pl.num_programs(2) - 1))
    def _():
        o_ref[...] = _layer_norm(DN_ALPHA * x_ref[...] + acc_ref[...], g_ref[...], b_ref[...])


def _moe(xb, gates, w_in, w_out, x, g, b, tm, tf):
    T, D = xb.shape
    E, F, _ = w_out.shape
    nf = F // tf
    return pl.pallas_call(
        _moe_kernel,
        grid=(T // tm, E, nf),
        in_specs=[
            pl.BlockSpec((tm, D), lambda i, e, f: (i, 0)),
            pl.BlockSpec((tm, LANES), lambda i, e, f: (i, 0)),
            pl.BlockSpec((1, D, tf), lambda i, e, f: (e, 0, f)),
            pl.BlockSpec((1, D, tf), lambda i, e, f: (e, 0, nf + f)),
            pl.BlockSpec((1, tf, D), lambda i, e, f: (e, f, 0)),
            pl.BlockSpec((tm, D), lambda i, e, f: (i, 0)),
            pl.BlockSpec((1, D), lambda i, e, f: (0, 0)),
            pl.BlockSpec((1, D), lambda i, e, f: (0, 0)),
        ],
        out_specs=pl.BlockSpec((tm, D), lambda i, e, f: (i, 0)),
        out_shape=jax.ShapeDtypeStruct((T, D), F32),
        scratch_shapes=[pltpu.VMEM((tm, D), F32)],
        compiler_params=_params("parallel", "arbitrary", "arbitrary"),
    )(xb, gates, w_in, w_in, w_out, x, g.reshape(1, D), b.reshape(1, D))


def kernel(x, ret_w_in, ret_gn_g, ret_w_out, nsa_w_kv, cmp_k_pe, cmp_k_w1, cmp_k_w2, cmp_v_pe, cmp_v_w1, cmp_v_w2,
           nsa_w_q, nsa_w_out, ffn_w_in, ffn_w_out, moe_router, moe_w_in, moe_w_out, ln_g, ln_b):
    B, S, D = x.shape
    T = B * S
    G, R, dh = NSA_GROUPS, NSA_REP, NSA_HD
    assert ret_w_in.shape[0] == 1 and nsa_w_q.shape[0] == 1 and ln_g.shape[0] == DEPTH
    dk = D // RET_HEADS
    dv = 2 * dk
    n_slc = S // SLC_LEN
    assert n_slc <= dh and S % 256 == 0 and S >= WIN + 256
    tm = min(1024, S)
    tq = 256

    xf = x.reshape(T, D)
    xb = xf.astype(BF16)

    cos_r, sin_r = _rope_tables(S, dk, dk)
    qkvg = _ret_proj(xb, ret_w_in[0].astype(BF16), cos_r[:, :dk // 2], -sin_r[:, :dk // 2], S, dk, tm)
    y = _retention(qkvg, ret_gn_g[0], B, S, dk, dv, rows=min(512, S))
    x1, x1b = _proj_ln(y, ret_w_out[0].astype(BF16), xf, ln_g[0, 0], ln_b[0, 0], tm=512)
    x2, x2b = _ffn(x1b, ffn_w_in[0].astype(BF16), ffn_w_out[0].astype(BF16), x1, ln_g[0, 1], ln_b[0, 1],
                   tm=tm, tf=256)

    cos_n, sin_n = _rope_tables(S, dh, LANES)
    kv = _kv_proj(x2b, nsa_w_kv.astype(BF16), cos_n, sin_n, S, tm)
    kv6 = kv.reshape(B, S, 6, G, dh).transpose(2, 0, 3, 1, 4)
    n_c = S // CMP_STRIDE
    cmp_in = kv6[0:2].reshape(2, B * G * n_c, CMP_STRIDE * dh)
    w1 = jnp.stack([cmp_k_w1, cmp_v_w1]).astype(BF16)
    w2 = jnp.stack([cmp_k_w2, cmp_v_w2]).astype(BF16)
    pe = jnp.stack([cmp_k_pe, cmp_v_pe]).reshape(2, 1, CMP_LEN * dh)
    pe = jnp.broadcast_to(pe, (2, 8, CMP_LEN * dh)).astype(BF16)
    kvc = _compress(cmp_in, w1, w2, pe, tm=min(1024, B * G * n_c)).reshape(2, B * G, n_c, dh)
    block_of_key = jnp.arange(S) // SLC_LEN
    onehot = (block_of_key[:, None] == jnp.arange(dh)[None, :]).astype(BF16)
    ks_aug = jnp.concatenate([kv6[2], jnp.broadcast_to(onehot, (B, G, S, dh))], -1).reshape(B * G, S, 2 * dh)
    vs = kv6[3].reshape(B * G, S, dh)
    kw = kv6[4].reshape(B * G, S, dh)
    vw = kv6[5].reshape(B * G, S, dh)
    cs = jnp.arange(n_c) * CMP_STRIDE
    ss = jnp.arange(LANES) * SLC_LEN
    overlap = jnp.clip(jnp.minimum(cs[:, None] + CMP_LEN, ss[None, :] + SLC_LEN)
                       - jnp.maximum(cs[:, None], ss[None, :]), 0).astype(F32) / CMP_STRIDE
    overlap = jnp.where(jnp.arange(LANES)[None, :] < n_slc, overlap, 0.0).astype(BF16)

    nq_cols = NSA_HEADS * dh
    wq = nsa_w_q[0]
    wq_pad = jnp.pad(wq, ((0, 0), (0, nq_cols + LANES - wq.shape[1]))).astype(BF16)
    q, gates = _q_proj(x2b, wq_pad, cos_n, sin_n, S, tm=512)
    gates = gates[:, :NSA_HEADS * 3].reshape(B, S, G, R, 3).transpose(0, 2, 3, 1, 4)
    attn = _nsa_attention(q, gates, kvc[0], kvc[1], ks_aug, vs, kw, vw, overlap, B, S, tq)
    x3, x3b = _proj_ln(attn, nsa_w_out[0].astype(BF16), x2, ln_g[1, 0], ln_b[1, 0], tm=512)

    wr = jnp.pad(moe_router[0], ((0, 0), (0, LANES - N_EXPERTS)))
    wr_hi = wr.astype(BF16)
    wr_lo = (wr - wr_hi.astype(F32)).astype(BF16)
    route = _router(x3, wr_hi, wr_lo, tm=tm)
    out = _moe(x3b, route, moe_w_in[0].astype(BF16), moe_w_out[0].astype(BF16), x3, ln_g[1, 1], ln_b[1, 1],
               tm=tm, tf=512)
    return out.reshape(B, S, D)
```

```python
import functools

import jax
import jax.numpy as jnp
from jax import lax
from jax.experimental import pallas as pl
from jax.experimental.pallas import tpu as pltpu

F32 = jnp.float32
BF16 = jnp.bfloat16

DEPTH = 2
ROPE_THETA = 10000.0
LN_EPS = 1e-5
DN_ALPHA = (2.0 * DEPTH) ** 0.25

RET_HEADS = 4
RET_CHUNK = 128

NSA_HEADS = 16
NSA_GROUPS = 4
NSA_REP = NSA_HEADS // NSA_GROUPS
NSA_HD = 64
CMP_LEN = 32
CMP_STRIDE = 16
SLC_LEN = 64
SLC_TOPK = 8
WIN = 512
FORCE_SCORE = 1e9

N_EXPERTS = 8

LANES = 128
MASK_NEG = -1e30
VMEM_LIMIT = 56 * 1024 * 1024


def _params(*sem):
    return pltpu.CompilerParams(dimension_semantics=sem, vmem_limit_bytes=VMEM_LIMIT)


def _dot(a, b):
    return jnp.dot(a, b, preferred_element_type=F32)


def _dot_nt(a, b):
    return lax.dot_general(a, b, (((1,), (1,)), ((), ())), preferred_element_type=F32)


def _dot_tn(a, b):
    return lax.dot_general(a, b, (((0,), (0,)), ((), ())), preferred_element_type=F32)


def _layer_norm(z, g, b):
    mu = jnp.mean(z, -1, keepdims=True)
    zc = z - mu
    var = jnp.mean(zc * zc, -1, keepdims=True)
    return zc * lax.rsqrt(var + LN_EPS) * g + b


def _silu(a):
    return a * jax.nn.sigmoid(a)


def _rope_tables(seq, dim, width):
    inv = 1.0 / (ROPE_THETA ** (jnp.arange(0, dim, 2, dtype=F32) / dim))
    ang = jnp.arange(seq, dtype=F32)[:, None] * inv[None, :]
    cos, sin = jnp.cos(ang), jnp.sin(ang)
    cos_h = jnp.concatenate([cos, cos], -1)
    sin_h = jnp.concatenate([-sin, sin], -1)
    rep = width // dim
    return jnp.tile(cos_h, (1, rep)), jnp.tile(sin_h, (1, rep))


def _ret_proj_kernel(x_ref, w_ref, cos_ref, sin_ref, o_ref, *, dk, n_rot):
    j = pl.program_id(1)
    acc = _dot(x_ref[...], w_ref[...])
    tn = acc.shape[1]
    half = dk // 2

    @pl.when(j < n_rot)
    def _():
        scale = jnp.where(j == n_rot - 1, dk ** -0.5, 1.0).astype(F32)
        cos = cos_ref[...] * scale
        sin = sin_ref[...] * scale
        for h in range(tn // dk):
            a = acc[:, h * dk:h * dk + half]
            b = acc[:, h * dk + half:(h + 1) * dk]
            o_ref[:, h * dk:h * dk + half] = (a * cos - b * sin).astype(o_ref.dtype)
            o_ref[:, h * dk + half:(h + 1) * dk] = (a * sin + b * cos).astype(o_ref.dtype)

    @pl.when(j >= n_rot)
    def _():
        o_ref[...] = acc.astype(o_ref.dtype)


def _ret_proj(xb, w, cos, sin, seq, dk, tm):
    T, D = xb.shape
    N = w.shape[1]
    tn = RET_HEADS * dk
    half = dk // 2
    n_pos = seq // tm
    return pl.pallas_call(
        functools.partial(_ret_proj_kernel, dk=dk, n_rot=2),
        grid=(T // tm, N // tn),
        in_specs=[
            pl.BlockSpec((tm, D), lambda i, j: (i, 0)),
            pl.BlockSpec((D, tn), lambda i, j: (0, j)),
            pl.BlockSpec((tm, half), lambda i, j: (i % n_pos, 0)),
            pl.BlockSpec((tm, half), lambda i, j: (i % n_pos, 0)),
        ],
        out_specs=pl.BlockSpec((tm, tn), lambda i, j: (i, j)),
        out_shape=jax.ShapeDtypeStruct((T, N), BF16),
        compiler_params=_params("parallel", "arbitrary"),
    )(xb, w, cos, sin)


def _retention_kernel(q_ref, k_ref, v_ref, g_ref, dec_ref, qd_ref, kd_ref, gn_ref, o_ref, state_ref,
                      *, chunk, n_chunks):
    @pl.when(pl.program_id(2) == 0)
    def _():
        state_ref[...] = jnp.zeros_like(state_ref)

    decay = dec_ref[0]
    qd = qd_ref[0]
    kd = kd_ref[0]
    cd = qd[chunk - 1:chunk, :]
    gn = gn_ref[0]
    for c in range(n_chunks):
        rows = slice(c * chunk, (c + 1) * chunk)
        q = q_ref[rows, :]
        k = k_ref[rows, :]
        v = v_ref[rows, :]
        scores = _dot_nt(q, k) * decay
        inner = _dot(scores.astype(BF16), v)
        state = state_ref[...]
        cross = _dot(q, state.astype(BF16)) * qd
        out = inner + cross
        k_dec = (k.astype(F32) * kd).astype(BF16)
        state_ref[...] = state * cd + _dot_tn(k_dec, v)
        mu = jnp.mean(out, -1, keepdims=True)
        oc = out - mu
        var = jnp.mean(oc * oc, -1, keepdims=True)
        normed = oc * lax.rsqrt(var + LN_EPS) * gn
        o_ref[rows, :] = (_silu(g_ref[rows, :].astype(F32)) * normed).astype(o_ref.dtype)


def _retention(qkvg, gn_g, batch, seq, dk, dv, rows):
    H, C = RET_HEADS, RET_CHUNK
    T = batch * seq
    n_steps = seq // rows
    log_gamma = jnp.log1p(-(2.0 ** (-5.0 - jnp.arange(H, dtype=F32))))
    i = jnp.arange(C, dtype=F32)
    rel = i[:, None] - i[None, :]
    intra = jnp.where(rel >= 0, jnp.exp(log_gamma[:, None, None] * jnp.maximum(rel, 0.0)), 0.0)
    q_decay = jnp.exp(log_gamma[:, None] * (i + 1.0))[:, :, None]
    k_decay = jnp.exp(log_gamma[:, None] * (C - 1.0 - i))[:, :, None]
    v_off = (2 * H * dk) // dv
    g_off = (2 * H * dk + H * dv) // dv
    row = lambda b, h, n: b * n_steps + n
    return pl.pallas_call(
        functools.partial(_retention_kernel, chunk=C, n_chunks=rows // C),
        grid=(batch, H, n_steps),
        in_specs=[
            pl.BlockSpec((rows, dk), lambda b, h, n: (row(b, h, n), h)),
            pl.BlockSpec((rows, dk), lambda b, h, n: (row(b, h, n), H + h)),
            pl.BlockSpec((rows, dv), lambda b, h, n: (row(b, h, n), v_off + h)),
            pl.BlockSpec((rows, dv), lambda b, h, n: (row(b, h, n), g_off + h)),
            pl.BlockSpec((1, C, C), lambda b, h, n: (h, 0, 0)),
            pl.BlockSpec((1, C, 1), lambda b, h, n: (h, 0, 0)),
            pl.BlockSpec((1, C, 1), lambda b, h, n: (h, 0, 0)),
            pl.BlockSpec((1, 1, dv), lambda b, h, n: (h, 0, 0)),
        ],
        out_specs=pl.BlockSpec((rows, dv), lambda b, h, n: (row(b, h, n), h)),
        out_shape=jax.ShapeDtypeStruct((T, H * dv), BF16),
        scratch_shapes=[pltpu.VMEM((dk, dv), F32)],
        compiler_params=_params("parallel", "parallel", "arbitrary"),
    )(qkvg, qkvg, qkvg, qkvg, intra, q_decay, k_decay, gn_g.reshape(H, 1, dv))


def _proj_ln_kernel(y_ref, w_ref, x_ref, g_ref, b_ref, o_ref, ob_ref):
    z = DN_ALPHA * x_ref[...] + _dot(y_ref[...], w_ref[...])
    o = _layer_norm(z, g_ref[...], b_ref[...])
    o_ref[...] = o
    ob_ref[...] = o.astype(BF16)


def _proj_ln(y, w, x, g, b, tm):
    T, K = y.shape
    D = w.shape[1]
    return pl.pallas_call(
        _proj_ln_kernel,
        grid=(T // tm,),
        in_specs=[
            pl.BlockSpec((tm, K), lambda i: (i, 0)),
            pl.BlockSpec((K, D), lambda i: (0, 0)),
            pl.BlockSpec((tm, D), lambda i: (i, 0)),
            pl.BlockSpec((1, D), lambda i: (0, 0)),
            pl.BlockSpec((1, D), lambda i: (0, 0)),
        ],
        out_specs=[pl.BlockSpec((tm, D), lambda i: (i, 0)), pl.BlockSpec((tm, D), lambda i: (i, 0))],
        out_shape=[jax.ShapeDtypeStruct((T, D), F32), jax.ShapeDtypeStruct((T, D), BF16)],
        compiler_params=_params("parallel"),
    )(y, w, x, g.reshape(1, D), b.reshape(1, D))


def _ffn_kernel(xb_ref, wa_ref, wb_ref, wo_ref, x_ref, g_ref, b_ref, o_ref, ob_ref, acc_ref):
    f = pl.program_id(1)
    xb = xb_ref[...]
    h = (_silu(_dot(xb, wa_ref[...])) * _dot(xb, wb_ref[...])).astype(BF16)
    contrib = _dot(h, wo_ref[...])

    @pl.when(f == 0)
    def _():
        acc_ref[...] = contrib

    @pl.when(f > 0)
    def _():
        acc_ref[...] += contrib

    @pl.when(f == pl.num_programs(1) - 1)
    def _():
        o = _layer_norm(DN_ALPHA * x_ref[...] + acc_ref[...], g_ref[...], b_ref[...])
        o_ref[...] = o
        ob_ref[...] = o.astype(BF16)


def _ffn(xb, w_in, w_out, x, g, b, tm, tf):
    T, D = xb.shape
    F = w_out.shape[0]
    nf = F // tf
    return pl.pallas_call(
        _ffn_kernel,
        grid=(T // tm, nf),
        in_specs=[
            pl.BlockSpec((tm, D), lambda i, f: (i, 0)),
            pl.BlockSpec((D, tf), lambda i, f: (0, f)),
            pl.BlockSpec((D, tf), lambda i, f: (0, nf + f)),
            pl.BlockSpec((tf, D), lambda i, f: (f, 0)),
            pl.BlockSpec((tm, D), lambda i, f: (i, 0)),
            pl.BlockSpec((1, D), lambda i, f: (0, 0)),
            pl.BlockSpec((1, D), lambda i, f: (0, 0)),
        ],
        out_specs=[pl.BlockSpec((tm, D), lambda i, f: (i, 0)), pl.BlockSpec((tm, D), lambda i, f: (i, 0))],
        out_shape=[jax.ShapeDtypeStruct((T, D), F32), jax.ShapeDtypeStruct((T, D), BF16)],
        scratch_shapes=[pltpu.VMEM((tm, D), F32)],
        compiler_params=_params("parallel", "arbitrary"),
    )(xb, w_in, w_in, w_out, x, g.reshape(1, D), b.reshape(1, D))


def _rope64(acc, cos, sin_signed):
    half = NSA_HD // 2
    lane = lax.broadcasted_iota(jnp.int32, acc.shape, 1)
    first = (lane % NSA_HD) < half
    rot = jnp.where(first, pltpu.roll(acc, LANES - half, 1), pltpu.roll(acc, half, 1))
    return acc * cos + rot * sin_signed


def _kv_proj_kernel(x_ref, w_ref, cos_ref, sin_ref, o_ref):
    j = pl.program_id(1)
    acc = _dot(x_ref[...], w_ref[...])

    @pl.when(j % 2 == 0)
    def _():
        for s in range(acc.shape[1] // LANES):
            cols = slice(s * LANES, (s + 1) * LANES)
            o_ref[:, cols] = _rope64(acc[:, cols], cos_ref[...], sin_ref[...]).astype(o_ref.dtype)

    @pl.when(j % 2 == 1)
    def _():
        o_ref[...] = acc.astype(o_ref.dtype)


def _kv_proj(xb, w, cos, sin, seq, tm):
    T, D = xb.shape
    N = w.shape[1]
    tn = NSA_GROUPS * NSA_HD
    n_pos = seq // tm
    return pl.pallas_call(
        _kv_proj_kernel,
        grid=(T // tm, N // tn),
        in_specs=[
            pl.BlockSpec((tm, D), lambda i, j: (i, 0)),
            pl.BlockSpec((D, tn), lambda i, j: (0, j)),
            pl.BlockSpec((tm, LANES), lambda i, j: (i % n_pos, 0)),
            pl.BlockSpec((tm, LANES), lambda i, j: (i % n_pos, 0)),
        ],
        out_specs=pl.BlockSpec((tm, tn), lambda i, j: (i, j)),
        out_shape=jax.ShapeDtypeStruct((T, N), BF16),
        compiler_params=_params("parallel", "arbitrary"),
    )(xb, w, cos, sin)


def _q_proj_kernel(x_ref, w_ref, cos_ref, sin_ref, q_ref, gate_ref, *, nq):
    acc = _dot(x_ref[...], w_ref[...])
    cos = cos_ref[...] * NSA_HD ** -0.5
    sin = sin_ref[...] * NSA_HD ** -0.5
    for s in range(nq // LANES):
        cols = slice(s * LANES, (s + 1) * LANES)
        q_ref[:, cols] = _rope64(acc[:, cols], cos, sin).astype(q_ref.dtype)
    gate_ref[...] = jax.nn.sigmoid(acc[:, nq:])


def _q_proj(xb, w_pad, cos, sin, seq, tm):
    T, D = xb.shape
    nq = NSA_HEADS * NSA_HD
    N = w_pad.shape[1]
    n_pos = seq // tm
    return pl.pallas_call(
        functools.partial(_q_proj_kernel, nq=nq),
        grid=(T // tm,),
        in_specs=[
            pl.BlockSpec((tm, D), lambda i: (i, 0)),
            pl.BlockSpec((D, N), lambda i: (0, 0)),
            pl.BlockSpec((tm, LANES), lambda i: (i % n_pos, 0)),
            pl.BlockSpec((tm, LANES), lambda i: (i % n_pos, 0)),
        ],
        out_specs=[pl.BlockSpec((tm, nq), lambda i: (i, 0)), pl.BlockSpec((tm, N - nq), lambda i: (i, 0))],
        out_shape=[jax.ShapeDtypeStruct((T, nq), BF16), jax.ShapeDtypeStruct((T, N - nq), F32)],
        compiler_params=_params("parallel"),
    )(xb, w_pad, cos, sin)


def _compress_kernel(a_ref, w1_ref, w2_ref, pe_ref, o_ref):
    a = a_ref[0]
    w1 = w1_ref[0]
    half = w1.shape[0] // 2
    first = _dot(a, w1[:half])
    second = _dot(a, w1[half:])
    bias = _dot(pe_ref[0], w1)[0:1]
    m = a.shape[0]
    hid = first + pltpu.roll(second, m - 1, 0) + bias
    o_ref[0] = _dot(jax.nn.gelu(hid).astype(BF16), w2_ref[0]).astype(o_ref.dtype)


def _compress(a, w1, w2, pe, tm):
    _, rows, width = a.shape
    hid = w1.shape[2]
    dh = w2.shape[2]
    return pl.pallas_call(
        _compress_kernel,
        grid=(2, rows // tm),
        in_specs=[
            pl.BlockSpec((1, tm, width), lambda s, i: (s, i, 0)),
            pl.BlockSpec((1, 2 * width, hid), lambda s, i: (s, 0, 0)),
            pl.BlockSpec((1, hid, dh), lambda s, i: (s, 0, 0)),
            pl.BlockSpec((1, 8, 2 * width), lambda s, i: (s, 0, 0)),
        ],
        out_specs=pl.BlockSpec((1, tm, dh), lambda s, i: (s, i, 0)),
        out_shape=jax.ShapeDtypeStruct((2, rows, dh), BF16),
        compiler_params=_params("parallel", "parallel"),
    )(a, w1, w2, pe)


def _nsa_kernel(q_ref, gt_ref, kc_ref, vc_ref, ks_ref, vs_ref, kw_ref, vw_ref, ov_ref, o_ref, s_ref,
                *, tq, n_slc, n_top):
    R, dh = NSA_REP, NSA_HD
    i = pl.program_id(2)
    s0 = pl.multiple_of(i * tq, tq)
    qb = q_ref[...]
    q_heads = [qb[:, r * dh:(r + 1) * dh] for r in range(R)]
    q4 = jnp.concatenate(q_heads, axis=0)
    t_col = s0 + lax.broadcasted_iota(jnp.int32, (tq, 1), 0)
    lane = lax.broadcasted_iota(jnp.int32, (tq, LANES), 1)

    n_c = kc_ref.shape[1]
    lane_c = lax.broadcasted_iota(jnp.int32, (tq, n_c), 1)
    valid_c = (lane_c * CMP_STRIDE + (CMP_LEN - 1) <= t_col)[None]
    sc = _dot_nt(q4, kc_ref[0]).reshape(R, tq, n_c)
    sc = jnp.where(valid_c, sc, -jnp.inf)
    mc = jnp.max(sc, -1, keepdims=True)
    mc = jnp.where(mc == -jnp.inf, 0.0, mc)
    ec = jnp.where(valid_c, jnp.exp(sc - mc), 0.0)
    p_cmp = ec / jnp.maximum(jnp.sum(ec, -1, keepdims=True), 1e-30)
    o_cmp = _dot(p_cmp.reshape(R * tq, n_c).astype(BF16), vc_ref[0])

    p_sum = jnp.sum(p_cmp, axis=0)
    p_hi = p_sum.astype(BF16)
    p_lo = (p_sum - p_hi.astype(F32)).astype(BF16)
    imp = _dot(p_hi, ov_ref[...]) + _dot(p_lo, ov_ref[...])
    cur = t_col // SLC_LEN
    forced = (lane == 0) | (lane == cur) | (lane == cur - 1)
    imp = jnp.where(forced, FORCE_SCORE, imp)
    imp = jnp.where(lane <= cur, imp, -jnp.inf)
    rank = jnp.zeros((tq, LANES), jnp.int32)
    for b in range(n_slc):
        col = imp[:, b:b + 1]
        wins_tie = jnp.where(lane > b, 1, 0)
        rank = rank + jnp.where(col > imp, 1, jnp.where(col == imp, wins_tie, 0))
    selected = (rank < n_top) & (lane <= cur)
    sel_bias = jnp.where(selected, 0.0, MASK_NEG).astype(BF16)[:, :dh]
    qp = jnp.concatenate([jnp.concatenate([qh, sel_bias], axis=1) for qh in q_heads], axis=0)

    def score_block(kb, m_run):
        k0 = pl.multiple_of(kb * tq, tq)
        s = _dot_nt(qp, ks_ref[0, pl.ds(k0, tq), :])
        s_ref[:, pl.ds(k0, tq)] = s
        return jnp.maximum(m_run, jnp.max(s, -1, keepdims=True))

    m_run = lax.fori_loop(0, i, score_block, jnp.full((R * tq, 1), MASK_NEG, F32))
    causal = (lax.broadcasted_iota(jnp.int32, (tq, tq), 1) <= lax.broadcasted_iota(jnp.int32, (tq, tq), 0))[None]
    s_diag = _dot_nt(qp, ks_ref[0, pl.ds(s0, tq), :]).reshape(R, tq, tq)
    s_diag = jnp.where(causal, s_diag, MASK_NEG).reshape(R * tq, tq)
    s_ref[:, pl.ds(s0, tq)] = s_diag
    m_sel = jnp.maximum(m_run, jnp.max(s_diag, -1, keepdims=True))

    def value_block(kb, carry):
        l_run, acc = carry
        k0 = pl.multiple_of(kb * tq, tq)
        p = jnp.exp(s_ref[:, pl.ds(k0, tq)] - m_sel)
        l_run = l_run + jnp.sum(p, -1, keepdims=True)
        acc = acc + _dot(p.astype(BF16), vs_ref[0, pl.ds(k0, tq), :])
        return l_run, acc

    l_sel, acc_sel = lax.fori_loop(0, i + 1, value_block,
                                   (jnp.zeros((R * tq, 1), F32), jnp.zeros((R * tq, dh), F32)))
    o_slc = acc_sel / l_sel

    n_kw = WIN + tq
    w0 = pl.multiple_of(jnp.maximum(i - WIN // tq, 0) * tq, tq)
    sw = _dot_nt(q4, kw_ref[0, pl.ds(w0, n_kw), :]).reshape(R, tq, n_kw)
    kpos = w0 + lax.broadcasted_iota(jnp.int32, (tq, n_kw), 1)
    valid_w = ((kpos <= t_col) & (kpos > t_col - WIN))[None]
    sw = jnp.where(valid_w, sw, MASK_NEG)
    pw = jnp.exp(sw - jnp.max(sw, -1, keepdims=True))
    l_win = jnp.sum(pw, -1, keepdims=True).reshape(R * tq, 1)
    o_win = _dot(pw.reshape(R * tq, n_kw).astype(BF16), vw_ref[0, pl.ds(w0, n_kw), :]) / l_win

    gates = gt_ref[...]
    outs = []
    for r in range(R):
        rows = slice(r * tq, (r + 1) * tq)
        g = gates[:, 3 * r:3 * r + 3]
        outs.append(g[:, 0:1] * o_cmp[rows] + g[:, 1:2] * o_slc[rows] + g[:, 2:3] * o_win[rows])
    o_ref[...] = jnp.concatenate(outs, axis=1).astype(o_ref.dtype)


def _nsa_attention(q, gates, kc, vc, ks_aug, vs, kw, vw, overlap, batch, seq, tq):
    G, R, dh = NSA_GROUPS, NSA_REP, NSA_HD
    T = batch * seq
    nq = seq // tq
    n_slc = seq // SLC_LEN
    n_c = kc.shape[1]
    per_bg = lambda b, g, i: (b * G + g, 0, 0)
    return pl.pallas_call(
        functools.partial(_nsa_kernel, tq=tq, n_slc=n_slc, n_top=min(SLC_TOPK, n_slc)),
        grid=(batch, G, nq),
        in_specs=[
            pl.BlockSpec((tq, R * dh), lambda b, g, i: (b * nq + i, g)),
            pl.BlockSpec((tq, LANES), lambda b, g, i: (b * nq + i, g)),
            pl.BlockSpec((1, n_c, dh), per_bg),
            pl.BlockSpec((1, n_c, dh), per_bg),
            pl.BlockSpec((1, seq, 2 * dh), per_bg),
            pl.BlockSpec((1, seq, dh), per_bg),
            pl.BlockSpec((1, seq, dh), per_bg),
            pl.BlockSpec((1, seq, dh), per_bg),
            pl.BlockSpec((n_c, LANES), lambda b, g, i: (0, 0)),
        ],
        out_specs=pl.BlockSpec((tq, R * dh), lambda b, g, i: (b * nq + i, g)),
        out_shape=jax.ShapeDtypeStruct((T, G * R * dh), BF16),
        scratch_shapes=[pltpu.VMEM((R * tq, seq), F32)],
        compiler_params=_params("parallel", "parallel", "arbitrary"),
    )(q, gates, kc, vc, ks_aug, vs, kw, vw, overlap)


R_E1, R_E2, R_W1, R_W2, R_RANK1, R_RANK2 = range(6)


def _router_kernel(x_ref, whi_ref, wlo_ref, route_ref, before_ref, total_ref, cnt_ref):
    @pl.when(pl.program_id(0) == 0)
    def _():
        cnt_ref[...] = jnp.zeros_like(cnt_ref)

    x = x_ref[...]
    x_hi = x.astype(BF16)
    x_lo = (x - x_hi.astype(F32)).astype(BF16)
    logits = _dot(x_hi, whi_ref[...]) + _dot(x_hi, wlo_ref[...]) + _dot(x_lo, whi_ref[...])
    tc = logits.shape[0]
    lane = lax.broadcasted_iota(jnp.int32, logits.shape, 1).astype(F32)
    logits = jnp.where(lane < N_EXPERTS, logits, -jnp.inf)
    m1 = jnp.max(logits, -1, keepdims=True)
    i1 = jnp.min(jnp.where(logits == m1, lane, float(LANES)), -1, keepdims=True)
    rest = jnp.where(lane == i1, -jnp.inf, logits)
    m2 = jnp.max(rest, -1, keepdims=True)
    i2 = jnp.min(jnp.where(rest == m2, lane, float(LANES)), -1, keepdims=True)
    e2 = jnp.exp(m2 - m1)
    den = 1.0 + e2
    chosen = jnp.where(lane == i1, 1.0, jnp.where(lane == i2, 1.0, 0.0))
    earlier = (lax.broadcasted_iota(jnp.int32, (tc, tc), 1) < lax.broadcasted_iota(jnp.int32, (tc, tc), 0))
    before = cnt_ref[0:1, :]
    excl = _dot(jnp.where(earlier, 1.0, 0.0).astype(BF16), chosen.astype(BF16)) + before
    rank1 = jnp.sum(jnp.where(lane == i1, excl, 0.0), -1, keepdims=True)
    rank2 = jnp.sum(jnp.where(lane == i2, excl, 0.0), -1, keepdims=True)
    cols = {R_E1: i1, R_E2: i2, R_W1: 1.0 / den, R_W2: e2 / den, R_RANK1: rank1, R_RANK2: rank2}
    route = jnp.zeros_like(logits)
    for k, v in cols.items():
        route = jnp.where(lane == float(k), v, route)
    route_ref[...] = route
    before_ref[0] = jnp.broadcast_to(before, before_ref.shape[1:])
    after = before + jnp.sum(chosen, axis=0, keepdims=True)
    cnt_ref[...] = jnp.broadcast_to(after, cnt_ref.shape)
    total_ref[...] = jnp.broadcast_to(after, total_ref.shape)


def _router(x, w_hi, w_lo, tc):
    T, D = x.shape
    return pl.pallas_call(
        _router_kernel,
        grid=(T // tc,),
        in_specs=[
            pl.BlockSpec((tc, D), lambda i: (i, 0)),
            pl.BlockSpec((D, LANES), lambda i: (0, 0)),
            pl.BlockSpec((D, LANES), lambda i: (0, 0)),
        ],
        out_specs=[pl.BlockSpec((tc, LANES), lambda i: (i, 0)),
                   pl.BlockSpec((1, 8, LANES), lambda i: (i, 0, 0)),
                   pl.BlockSpec((8, LANES), lambda i: (0, 0))],
        out_shape=[jax.ShapeDtypeStruct((T, LANES), F32),
                   jax.ShapeDtypeStruct((T // tc, 8, LANES), F32),
                   jax.ShapeDtypeStruct((8, LANES), F32)],
        scratch_shapes=[pltpu.VMEM((8, LANES), F32)],
        compiler_params=_params("arbitrary"),
    )(x, w_hi, w_lo)


def _moe_plan(before, total, n_tok, tc, ts, tm):
    E = N_EXPERTS
    n_c = n_tok // tc
    n_rows = 2 * n_tok + E * tm
    n_sub, n_tiles = n_rows // ts, n_rows // tm
    n_work = E * n_c + 2 * n_sub
    i32 = jnp.int32
    tot = total[0, :E].astype(i32)
    cum = jnp.concatenate([before[:, 0, :E].astype(i32).T, tot[:, None]], axis=1)
    tiles_e = (tot + tm - 1) // tm
    tile_end = jnp.cumsum(tiles_e)
    row_off = (tile_end - tiles_e) * tm
    tile_id = jnp.arange(n_tiles, dtype=i32)
    tile_expert = jnp.minimum(jnp.sum(tile_id[:, None] >= tile_end[None, :], axis=1), E - 1).astype(i32)
    tile_valid = (tile_id < tile_end[-1]).astype(i32)
    sub = jnp.arange(n_sub, dtype=i32)
    sub_e = tile_expert[(sub * ts) // tm]
    rel0 = sub * ts - row_off[sub_e]
    rel1 = rel0 + ts
    cum_s = cum[sub_e]
    c_lo = jnp.sum(cum_s[:, 1:] <= rel0[:, None], axis=1).astype(i32)
    c_hi = jnp.sum(cum_s[:, :-1] < rel1[:, None], axis=1).astype(i32) - 1
    c_lo = jnp.clip(c_lo, 0, n_c - 1)
    c_hi = jnp.clip(c_hi, c_lo, n_c - 1)
    n_j = c_hi - c_lo + 1
    ends = jnp.cumsum(n_j)
    starts = ends - n_j
    n_valid = ends[-1]
    w = jnp.arange(n_work, dtype=i32)
    valid = w < n_valid
    jw = jnp.minimum(jnp.sum(w[:, None] >= ends[None, :], axis=1), n_sub - 1).astype(i32)
    cw = jnp.where(valid, c_lo[jw] + (w - starts[jw]), c_hi[n_sub - 1]).astype(i32)
    first = valid & (w == starts[jw])
    last = valid & (w == ends[jw] - 1)
    disp = (jw, cw, valid.astype(i32), first.astype(i32), last.astype(i32))
    order = jnp.argsort(jnp.where(valid, cw * n_sub + jw, n_c * n_sub + w))
    cj, cc, cv = jw[order], cw[order], valid[order]
    last_j, last_c = cj[n_valid - 1], cc[n_valid - 1]
    cj = jnp.where(cv, cj, last_j)
    cc = jnp.where(cv, cc, last_c)
    prev_c = jnp.concatenate([jnp.full((1,), -1, i32), cc[:-1]])
    next_c = jnp.concatenate([cc[1:], jnp.full((1,), -1, i32)])
    next_v = jnp.concatenate([cv[1:], jnp.zeros((1,), bool)])
    cfirst = cv & (cc != prev_c)
    clast = cv & ((cc != next_c) | ~next_v)
    comb = (cj.astype(i32), cc.astype(i32), cv.astype(i32), cfirst.astype(i32), clast.astype(i32))
    return row_off, tile_expert, tile_valid, disp, comb, n_rows, n_work


def _dispatch_kernel(jw, cw, vw, fw, lw, x_ref, route_ref, pos_ref, xs_ref, gs_ref, acc_ref, gacc_ref):
    w = pl.program_id(0)

    @pl.when(vw[w] == 1)
    def _():
        ts, tc = acc_ref.shape[0], x_ref.shape[0]
        row_id = jw[w] * ts + lax.broadcasted_iota(jnp.int32, (ts, 1), 0)
        pos = pos_ref[0]
        hit1 = jnp.where(pos[0:1, :] == row_id, 1.0, 0.0).astype(BF16)
        hit2 = jnp.where(pos[1:2, :] == row_id, 1.0, 0.0).astype(BF16)
        rows = _dot(hit1 + hit2, x_ref[...])
        route = route_ref[...]
        lane = lax.broadcasted_iota(jnp.int32, (tc, LANES), 1)
        pieces = jnp.zeros((tc, LANES), F32)
        for slot, col in enumerate((R_W1, R_W2)):
            rem = route[:, col:col + 1]
            for k in range(3):
                piece = rem.astype(BF16).astype(F32)
                pieces = jnp.where(lane == 3 * slot + k, piece, pieces)
                rem = rem - piece
        pieces = pieces.astype(BF16)
        lane_s = lax.broadcasted_iota(jnp.int32, (ts, LANES), 1)
        g = jnp.where(lane_s < 3, _dot(hit1, pieces), 0.0) + jnp.where((lane_s >= 3) & (lane_s < 6),
                                                                        _dot(hit2, pieces), 0.0)
        g = jnp.broadcast_to(jnp.sum(g, -1, keepdims=True), (ts, LANES))

        @pl.when(fw[w] == 1)
        def _():
            acc_ref[...] = rows
            gacc_ref[...] = g

        @pl.when(fw[w] == 0)
        def _():
            acc_ref[...] += rows
            gacc_ref[...] += g

        @pl.when(lw[w] == 1)
        def _():
            xs_ref[...] = acc_ref[...].astype(xs_ref.dtype)
            gs_ref[...] = gacc_ref[...]


def _dispatch(xb, route, pos_rows, disp, n_rows, n_work, tc, ts):
    T, D = xb.shape
    grid_spec = pltpu.PrefetchScalarGridSpec(
        num_scalar_prefetch=5,
        grid=(n_work,),
        in_specs=[
            pl.BlockSpec((tc, D), lambda w, jw, cw, vw, fw, lw: (cw[w], 0)),
            pl.BlockSpec((tc, LANES), lambda w, jw, cw, vw, fw, lw: (cw[w], 0)),
            pl.BlockSpec((1, 8, tc), lambda w, jw, cw, vw, fw, lw: (cw[w], 0, 0)),
        ],
        out_specs=[pl.BlockSpec((ts, D), lambda w, jw, cw, vw, fw, lw: (jw[w], 0)),
                   pl.BlockSpec((ts, LANES), lambda w, jw, cw, vw, fw, lw: (jw[w], 0))],
        scratch_shapes=[pltpu.VMEM((ts, D), F32), pltpu.VMEM((ts, LANES), F32)],
    )
    return pl.pallas_call(
        _dispatch_kernel,
        grid_spec=grid_spec,
        out_shape=[jax.ShapeDtypeStruct((n_rows, D), BF16), jax.ShapeDtypeStruct((n_rows, LANES), F32)],
        compiler_params=_params("arbitrary"),
    )(*disp, xb, route, pos_rows)


def _moe_kernel(te, tv, xs_ref, gs_ref, wa_ref, wb_ref, wo_ref, y_ref, acc_ref):
    i = pl.program_id(0)
    f = pl.program_id(1)
    last_f = pl.num_programs(1) - 1

    @pl.when(tv[i] == 1)
    def _():
        xs = xs_ref[...]
        h = (_silu(_dot(xs, wa_ref[0])) * _dot(xs, wb_ref[0]) * gs_ref[:, 0:1]).astype(BF16)
        contrib = _dot(h, wo_ref[0])

        @pl.when(f == 0)
        def _():
            acc_ref[...] = contrib

        @pl.when(f > 0)
        def _():
            acc_ref[...] += contrib

        @pl.when(f == last_f)
        def _():
            y_ref[...] = acc_ref[...].astype(y_ref.dtype)

    @pl.when((tv[i] == 0) & (f == last_f))
    def _():
        y_ref[...] = jnp.zeros_like(y_ref)


def _moe(xs, gs, tile_expert, tile_valid, w_in, w_out, tm, tf):
    n_rows, D = xs.shape
    E, F, _ = w_out.shape
    nf = F // tf
    f_of = lambda i, f, te, tv: jnp.where(tv[i] == 1, f, nf - 1)
    grid_spec = pltpu.PrefetchScalarGridSpec(
        num_scalar_prefetch=2,
        grid=(n_rows // tm, nf),
        in_specs=[
            pl.BlockSpec((tm, D), lambda i, f, te, tv: (i, 0)),
            pl.BlockSpec((tm, LANES), lambda i, f, te, tv: (i, 0)),
            pl.BlockSpec((1, D, tf), lambda i, f, te, tv: (te[i], 0, f_of(i, f, te, tv))),
            pl.BlockSpec((1, D, tf), lambda i, f, te, tv: (te[i], 0, nf + f_of(i, f, te, tv))),
            pl.BlockSpec((1, tf, D), lambda i, f, te, tv: (te[i], f_of(i, f, te, tv), 0)),
        ],
        out_specs=pl.BlockSpec((tm, D), lambda i, f, te, tv: (i, 0)),
        scratch_shapes=[pltpu.VMEM((tm, D), F32)],
    )
    return pl.pallas_call(
        _moe_kernel,
        grid_spec=grid_spec,
        out_shape=jax.ShapeDtypeStruct((n_rows, D), BF16),
        compiler_params=_params("parallel", "arbitrary"),
    )(tile_expert, tile_valid, xs, gs, w_in, w_in, w_out)


def _combine_kernel(cj, cc, cv, cf, cl, y_ref, pos_ref, x_ref, g_ref, b_ref, o_ref, acc_ref):
    w = pl.program_id(0)

    @pl.when(cv[w] == 1)
    def _():
        ts = y_ref.shape[0]
        col_id = cj[w] * ts + lax.broadcasted_iota(jnp.int32, (1, ts), 1)
        pos = pos_ref[...]
        hit = (jnp.where(pos[:, 0:1] == col_id, 1.0, 0.0).astype(BF16)
               + jnp.where(pos[:, 1:2] == col_id, 1.0, 0.0).astype(BF16))
        contrib = _dot(hit, y_ref[...])

        @pl.when(cf[w] == 1)
        def _():
            acc_ref[...] = contrib

        @pl.when(cf[w] == 0)
        def _():
            acc_ref[...] += contrib

        @pl.when(cl[w] == 1)
        def _():
            o_ref[...] = _layer_norm(DN_ALPHA * x_ref[...] + acc_ref[...], g_ref[...], b_ref[...])


def _combine(y, pos_cols, x, g, b, comb, n_work, tc, ts):
    T, D = x.shape
    grid_spec = pltpu.PrefetchScalarGridSpec(
        num_scalar_prefetch=5,
        grid=(n_work,),
        in_specs=[
            pl.BlockSpec((ts, D), lambda w, cj, cc, cv, cf, cl: (cj[w], 0)),
            pl.BlockSpec((tc, LANES), lambda w, cj, cc, cv, cf, cl: (cc[w], 0)),
            pl.BlockSpec((tc, D), lambda w, cj, cc, cv, cf, cl: (cc[w], 0)),
            pl.BlockSpec((1, D), lambda w, cj, cc, cv, cf, cl: (0, 0)),
            pl.BlockSpec((1, D), lambda w, cj, cc, cv, cf, cl: (0, 0)),
        ],
        out_specs=pl.BlockSpec((tc, D), lambda w, cj, cc, cv, cf, cl: (cc[w], 0)),
        scratch_shapes=[pltpu.VMEM((tc, D), F32)],
    )
    return pl.pallas_call(
        _combine_kernel,
        grid_spec=grid_spec,
        out_shape=jax.ShapeDtypeStruct((T, D), F32),
        compiler_params=_params("arbitrary"),
    )(*comb, y, pos_cols, x, g.reshape(1, D), b.reshape(1, D))


def _moe_layer(x, xb, w_router, w_in, w_out, g, b, tc, tm, tf):
    T, D = x.shape
    ts = tc
    wr = jnp.pad(w_router, ((0, 0), (0, LANES - N_EXPERTS)))
    wr_hi = wr.astype(BF16)
    wr_lo = (wr - wr_hi.astype(F32)).astype(BF16)
    route, before, total = _router(x, wr_hi, wr_lo, tc)
    row_off, tile_expert, tile_valid, disp, comb, n_rows, n_work = _moe_plan(before, total, T, tc, ts, tm)
    e1 = route[:, R_E1].astype(jnp.int32)
    e2 = route[:, R_E2].astype(jnp.int32)
    pos1 = row_off[e1] + route[:, R_RANK1].astype(jnp.int32)
    pos2 = row_off[e2] + route[:, R_RANK2].astype(jnp.int32)
    pos = jnp.stack([pos1, pos2])
    pos_rows = jnp.pad(pos.reshape(2, T // tc, tc).transpose(1, 0, 2), ((0, 0), (0, 6), (0, 0)),
                       constant_values=-1)
    pos_cols = jnp.pad(pos.T, ((0, 0), (0, LANES - 2)), constant_values=-1)
    xs, gs = _dispatch(xb, route, pos_rows, disp, n_rows, n_work, tc, ts)
    y = _moe(xs, gs, tile_expert, tile_valid, w_in, w_out, tm, tf)
    return _combine(y, pos_cols, x, g, b, comb, n_work, tc, ts)


def kernel(x, ret_w_in, ret_gn_g, ret_w_out, nsa_w_kv, cmp_k_pe, cmp_k_w1, cmp_k_w2, cmp_v_pe, cmp_v_w1, cmp_v_w2,
           nsa_w_q, nsa_w_out, ffn_w_in, ffn_w_out, moe_router, moe_w_in, moe_w_out, ln_g, ln_b):
    B, S, D = x.shape
    T = B * S
    G, R, dh = NSA_GROUPS, NSA_REP, NSA_HD
    assert ret_w_in.shape[0] == 1 and nsa_w_q.shape[0] == 1 and ln_g.shape[0] == DEPTH
    dk = D // RET_HEADS
    dv = 2 * dk
    n_slc = S // SLC_LEN
    assert n_slc <= dh and S % 256 == 0 and S >= WIN + 256
    tm = min(1024, S)
    tq = 256

    xf = x.reshape(T, D)
    xb = xf.astype(BF16)

    cos_r, sin_r = _rope_tables(S, dk, dk)
    qkvg = _ret_proj(xb, ret_w_in[0].astype(BF16), cos_r[:, :dk // 2], -sin_r[:, :dk // 2], S, dk, tm)
    y = _retention(qkvg, ret_gn_g[0], B, S, dk, dv, rows=min(512, S))
    x1, x1b = _proj_ln(y, ret_w_out[0].astype(BF16), xf, ln_g[0, 0], ln_b[0, 0], tm=512)
    x2, x2b = _ffn(x1b, ffn_w_in[0].astype(BF16), ffn_w_out[0].astype(BF16), x1, ln_g[0, 1], ln_b[0, 1],
                   tm=tm, tf=256)

    cos_n, sin_n = _rope_tables(S, dh, LANES)
    kv = _kv_proj(x2b, nsa_w_kv.astype(BF16), cos_n, sin_n, S, tm)
    kv6 = kv.reshape(B, S, 6, G, dh).transpose(2, 0, 3, 1, 4)
    n_c = S // CMP_STRIDE
    cmp_in = kv6[0:2].reshape(2, B * G * n_c, CMP_STRIDE * dh)
    w1 = jnp.stack([cmp_k_w1, cmp_v_w1]).astype(BF16)
    w2 = jnp.stack([cmp_k_w2, cmp_v_w2]).astype(BF16)
    pe = jnp.stack([cmp_k_pe, cmp_v_pe]).reshape(2, 1, CMP_LEN * dh)
    pe = jnp.broadcast_to(pe, (2, 8, CMP_LEN * dh)).astype(BF16)
    kvc = _compress(cmp_in, w1, w2, pe, tm=min(1024, B * G * n_c)).reshape(2, B * G, n_c, dh)
    block_of_key = jnp.arange(S) // SLC_LEN
    onehot = (block_of_key[:, None] == jnp.arange(dh)[None, :]).astype(BF16)
    ks_aug = jnp.concatenate([kv6[2], jnp.broadcast_to(onehot, (B, G, S, dh))], -1).reshape(B * G, S, 2 * dh)
    vs = kv6[3].reshape(B * G, S, dh)
    kw = kv6[4].reshape(B * G, S, dh)
    vw = kv6[5].reshape(B * G, S, dh)
    cs = jnp.arange(n_c) * CMP_STRIDE
    ss = jnp.arange(LANES) * SLC_LEN
    overlap = jnp.clip(jnp.minimum(cs[:, None] + CMP_LEN, ss[None, :] + SLC_LEN)
                       - jnp.maximum(cs[:, None], ss[None, :]), 0).astype(F32) / CMP_STRIDE
    overlap = jnp.where(jnp.arange(LANES)[None, :] < n_slc, overlap, 0.0).astype(BF16)

    nq_cols = NSA_HEADS * dh
    wq = nsa_w_q[0]
    w_gate = jnp.pad(wq[:, nq_cols:].reshape(D, G, R * 3), ((0, 0), (0, 0), (0, LANES - R * 3)))
    wq_pad = jnp.concatenate([wq[:, :nq_cols], w_gate.reshape(D, G * LANES)], axis=1).astype(BF16)
    q, gates = _q_proj(x2b, wq_pad, cos_n, sin_n, S, tm=512)
    attn = _nsa_attention(q, gates, kvc[0], kvc[1], ks_aug, vs, kw, vw, overlap, B, S, tq)
    x3, x3b = _proj_ln(attn, nsa_w_out[0].astype(BF16), x2, ln_g[1, 0], ln_b[1, 0], tm=512)

    out = _moe_layer(x3, x3b, moe_router[0], moe_w_in[0].astype(BF16), moe_w_out[0].astype(BF16),
                     ln_g[1, 1], ln_b[1, 1], tc=512, tm=1024, tf=512)
    return out.reshape(B, S, D)
```

```python
import functools

import jax
import jax.numpy as jnp
from jax import lax
from jax.experimental import pallas as pl
from jax.experimental.pallas import tpu as pltpu

F32 = jnp.float32
BF16 = jnp.bfloat16

DEPTH = 2
ROPE_THETA = 10000.0
LN_EPS = 1e-5
DN_ALPHA = (2.0 * DEPTH) ** 0.25

RET_HEADS = 4
RET_CHUNK = 128

NSA_HEADS = 16
NSA_GROUPS = 4
NSA_REP = NSA_HEADS // NSA_GROUPS
NSA_HD = 64
CMP_LEN = 32
CMP_STRIDE = 16
SLC_LEN = 64
SLC_TOPK = 8
WIN = 512
FORCE_SCORE = 1e9

N_EXPERTS = 8

LANES = 128
GATE_ROWS = 16
MASK_NEG = -1e30
VMEM_LIMIT = 56 * 1024 * 1024


def _params(*sem):
    return pltpu.CompilerParams(dimension_semantics=sem, vmem_limit_bytes=VMEM_LIMIT)


def _dot(a, b):
    return jnp.dot(a, b, preferred_element_type=F32)


def _dot_nt(a, b):
    return lax.dot_general(a, b, (((1,), (1,)), ((), ())), preferred_element_type=F32)


def _dot_tn(a, b):
    return lax.dot_general(a, b, (((0,), (0,)), ((), ())), preferred_element_type=F32)


def _layer_norm(z, g, b):
    mu = jnp.mean(z, -1, keepdims=True)
    zc = z - mu
    var = jnp.mean(zc * zc, -1, keepdims=True)
    return zc * lax.rsqrt(var + LN_EPS) * g + b


def _silu(a):
    return a * jax.nn.sigmoid(a)


def _rope_angles(seq, dim):
    inv = 1.0 / (ROPE_THETA ** (jnp.arange(0, dim, 2, dtype=F32) / dim))
    return jnp.arange(seq, dtype=F32)[:, None] * inv[None, :]


def _rope_tables(seq, dim, width):
    ang = _rope_angles(seq, dim)
    cos, sin = jnp.cos(ang), jnp.sin(ang)
    cos_h = jnp.concatenate([cos, cos], -1)
    sin_h = jnp.concatenate([-sin, sin], -1)
    rep = width // dim
    return jnp.tile(cos_h, (1, rep)), jnp.tile(sin_h, (1, rep))


def _ret_proj_kernel(x_ref, w_ref, cos_ref, sin_ref, o_ref, *, dk, n_rot):
    j = pl.program_id(1)
    acc = _dot(x_ref[...], w_ref[...])
    tn = acc.shape[1]
    half = dk // 2

    @pl.when(j < n_rot)
    def _():
        scale = jnp.where(j == n_rot - 1, dk ** -0.5, 1.0).astype(F32)
        cos = cos_ref[...] * scale
        sin = sin_ref[...] * scale
        for h in range(tn // dk):
            a = acc[:, h * dk:h * dk + half]
            b = acc[:, h * dk + half:(h + 1) * dk]
            o_ref[:, h * dk:h * dk + half] = (a * cos - b * sin).astype(o_ref.dtype)
            o_ref[:, h * dk + half:(h + 1) * dk] = (a * sin + b * cos).astype(o_ref.dtype)

    @pl.when(j >= n_rot)
    def _():
        o_ref[...] = acc.astype(o_ref.dtype)


def _ret_proj(xb, w, cos, sin, seq, dk, tm):
    T, D = xb.shape
    N = w.shape[1]
    tn = RET_HEADS * dk
    half = dk // 2
    n_pos = seq // tm
    return pl.pallas_call(
        functools.partial(_ret_proj_kernel, dk=dk, n_rot=2),
        grid=(T // tm, N // tn),
        in_specs=[
            pl.BlockSpec((tm, D), lambda i, j: (i, 0)),
            pl.BlockSpec((D, tn), lambda i, j: (0, j)),
            pl.BlockSpec((tm, half), lambda i, j: (i % n_pos, 0)),
            pl.BlockSpec((tm, half), lambda i, j: (i % n_pos, 0)),
        ],
        out_specs=pl.BlockSpec((tm, tn), lambda i, j: (i, j)),
        out_shape=jax.ShapeDtypeStruct((T, N), BF16),
        compiler_params=_params("parallel", "arbitrary"),
    )(xb, w, cos, sin)


def _retention_kernel(q_ref, k_ref, v_ref, g_ref, dec_ref, qd_ref, kd_ref, gn_ref, o_ref, state_ref,
                      *, chunk, n_chunks):
    @pl.when(pl.program_id(2) == 0)
    def _():
        state_ref[...] = jnp.zeros_like(state_ref)

    decay = dec_ref[0]
    qd = qd_ref[0]
    kd = kd_ref[0]
    cd = qd[chunk - 1:chunk, :]
    gn = gn_ref[0]
    for c in range(n_chunks):
        rows = slice(c * chunk, (c + 1) * chunk)
        q = q_ref[rows, :]
        k = k_ref[rows, :]
        v = v_ref[rows, :]
        scores = _dot_nt(q, k) * decay
        inner = _dot(scores.astype(BF16), v)
        state = state_ref[...]
        cross = _dot(q, state.astype(BF16)) * qd
        out = inner + cross
        k_dec = (k.astype(F32) * kd).astype(BF16)
        state_ref[...] = state * cd + _dot_tn(k_dec, v)
        mu = jnp.mean(out, -1, keepdims=True)
        oc = out - mu
        var = jnp.mean(oc * oc, -1, keepdims=True)
        normed = oc * lax.rsqrt(var + LN_EPS) * gn
        o_ref[rows, :] = (_silu(g_ref[rows, :].astype(F32)) * normed).astype(o_ref.dtype)


def _retention(qkvg, gn_g, batch, seq, dk, dv, rows):
    H, C = RET_HEADS, RET_CHUNK
    T = batch * seq
    n_steps = seq // rows
    log_gamma = jnp.log1p(-(2.0 ** (-5.0 - jnp.arange(H, dtype=F32))))
    i = jnp.arange(C, dtype=F32)
    rel = i[:, None] - i[None, :]
    intra = jnp.where(rel >= 0, jnp.exp(log_gamma[:, None, None] * jnp.maximum(rel, 0.0)), 0.0)
    q_decay = jnp.exp(log_gamma[:, None] * (i + 1.0))[:, :, None]
    k_decay = jnp.exp(log_gamma[:, None] * (C - 1.0 - i))[:, :, None]
    v_off = (2 * H * dk) // dv
    g_off = (2 * H * dk + H * dv) // dv
    row = lambda b, h, n: b * n_steps + n
    return pl.pallas_call(
        functools.partial(_retention_kernel, chunk=C, n_chunks=rows // C),
        grid=(batch, H, n_steps),
        in_specs=[
            pl.BlockSpec((rows, dk), lambda b, h, n: (row(b, h, n), h)),
            pl.BlockSpec((rows, dk), lambda b, h, n: (row(b, h, n), H + h)),
            pl.BlockSpec((rows, dv), lambda b, h, n: (row(b, h, n), v_off + h)),
            pl.BlockSpec((rows, dv), lambda b, h, n: (row(b, h, n), g_off + h)),
            pl.BlockSpec((1, C, C), lambda b, h, n: (h, 0, 0)),
            pl.BlockSpec((1, C, 1), lambda b, h, n: (h, 0, 0)),
            pl.BlockSpec((1, C, 1), lambda b, h, n: (h, 0, 0)),
            pl.BlockSpec((1, 1, dv), lambda b, h, n: (h, 0, 0)),
        ],
        out_specs=pl.BlockSpec((rows, dv), lambda b, h, n: (row(b, h, n), h)),
        out_shape=jax.ShapeDtypeStruct((T, H * dv), BF16),
        scratch_shapes=[pltpu.VMEM((dk, dv), F32)],
        compiler_params=_params("parallel", "parallel", "arbitrary"),
    )(qkvg, qkvg, qkvg, qkvg, intra, q_decay, k_decay, gn_g.reshape(H, 1, dv))


def _proj_ln_kernel(y_ref, w_ref, x_ref, g_ref, b_ref, o_ref, ob_ref, *, y_transposed):
    mix = _dot_tn(y_ref[...], w_ref[...]) if y_transposed else _dot(y_ref[...], w_ref[...])
    o = _layer_norm(DN_ALPHA * x_ref[...] + mix, g_ref[...], b_ref[...])
    o_ref[...] = o
    ob_ref[...] = o.astype(BF16)


def _proj_ln(y, w, x, g, b, tm, y_transposed=False):
    K, D = w.shape
    T = x.shape[0]
    y_spec = pl.BlockSpec((K, tm), lambda i: (0, i)) if y_transposed else pl.BlockSpec((tm, K), lambda i: (i, 0))
    return pl.pallas_call(
        functools.partial(_proj_ln_kernel, y_transposed=y_transposed),
        grid=(T // tm,),
        in_specs=[
            y_spec,
            pl.BlockSpec((K, D), lambda i: (0, 0)),
            pl.BlockSpec((tm, D), lambda i: (i, 0)),
            pl.BlockSpec((1, D), lambda i: (0, 0)),
            pl.BlockSpec((1, D), lambda i: (0, 0)),
        ],
        out_specs=[pl.BlockSpec((tm, D), lambda i: (i, 0)), pl.BlockSpec((tm, D), lambda i: (i, 0))],
        out_shape=[jax.ShapeDtypeStruct((T, D), F32), jax.ShapeDtypeStruct((T, D), BF16)],
        compiler_params=_params("parallel"),
    )(y, w, x, g.reshape(1, D), b.reshape(1, D))


def _ffn_kernel(xb_ref, wa_ref, wb_ref, wo_ref, x_ref, g_ref, b_ref, o_ref, ob_ref, acc_ref):
    f = pl.program_id(1)
    xb = xb_ref[...]
    h = (_silu(_dot(xb, wa_ref[...])) * _dot(xb, wb_ref[...])).astype(BF16)
    contrib = _dot(h, wo_ref[...])

    @pl.when(f == 0)
    def _():
        acc_ref[...] = contrib

    @pl.when(f > 0)
    def _():
        acc_ref[...] += contrib

    @pl.when(f == pl.num_programs(1) - 1)
    def _():
        o = _layer_norm(DN_ALPHA * x_ref[...] + acc_ref[...], g_ref[...], b_ref[...])
        o_ref[...] = o
        ob_ref[...] = o.astype(BF16)


def _ffn(xb, w_in, w_out, x, g, b, tm, tf):
    T, D = xb.shape
    F = w_out.shape[0]
    nf = F // tf
    return pl.pallas_call(
        _ffn_kernel,
        grid=(T // tm, nf),
        in_specs=[
            pl.BlockSpec((tm, D), lambda i, f: (i, 0)),
            pl.BlockSpec((D, tf), lambda i, f: (0, f)),
            pl.BlockSpec((D, tf), lambda i, f: (0, nf + f)),
            pl.BlockSpec((tf, D), lambda i, f: (f, 0)),
            pl.BlockSpec((tm, D), lambda i, f: (i, 0)),
            pl.BlockSpec((1, D), lambda i, f: (0, 0)),
            pl.BlockSpec((1, D), lambda i, f: (0, 0)),
        ],
        out_specs=[pl.BlockSpec((tm, D), lambda i, f: (i, 0)), pl.BlockSpec((tm, D), lambda i, f: (i, 0))],
        out_shape=[jax.ShapeDtypeStruct((T, D), F32), jax.ShapeDtypeStruct((T, D), BF16)],
        scratch_shapes=[pltpu.VMEM((tm, D), F32)],
        compiler_params=_params("parallel", "arbitrary"),
    )(xb, w_in, w_in, w_out, x, g.reshape(1, D), b.reshape(1, D))


def _rope64(acc, cos, sin_signed):
    half = NSA_HD // 2
    lane = lax.broadcasted_iota(jnp.int32, acc.shape, 1)
    first = (lane % NSA_HD) < half
    rot = jnp.where(first, pltpu.roll(acc, LANES - half, 1), pltpu.roll(acc, half, 1))
    return acc * cos + rot * sin_signed


def _kv_proj_kernel(x_ref, w_ref, cos_ref, sin_ref, ks_ref, vs_ref, kw_ref, cmp_ref, *, n_pos):
    G, dh = NSA_GROUPS, NSA_HD
    acc = _dot(x_ref[...], w_ref[...])
    tm = acc.shape[0]
    cos, sin = cos_ref[...], sin_ref[...]
    lane = lax.broadcasted_iota(jnp.int32, (tm, LANES), 1)
    low = lane < dh
    pos = (pl.program_id(0) % n_pos) * tm + lax.broadcasted_iota(jnp.int32, (tm, LANES), 0)
    block_onehot = jnp.where(lane - dh == pos // SLC_LEN, 1.0, 0.0)
    swap = lambda a: pltpu.roll(a, dh, 1)

    def slab(branch, s):
        off = branch * G * dh + s * LANES
        return acc[:, off:off + LANES]

    for s in range(G * dh // LANES):
        k_cmp, v_cmp = _rope64(slab(0, s), cos, sin), slab(1, s)
        k_slc, v_slc = _rope64(slab(2, s), cos, sin), slab(3, s)
        k_win, v_win = _rope64(slab(4, s), cos, sin), slab(5, s)
        for h in range(2):
            g = 2 * s + h
            to_low = (lambda a: a) if h == 0 else swap
            to_high = swap if h == 0 else (lambda a: a)
            ks_ref[g] = jnp.where(low, to_low(k_slc), block_onehot).astype(BF16)
            vs_ref[g] = jnp.where(low, to_low(v_slc), 0.0).astype(BF16)
            kw_ref[g] = jnp.where(low, to_low(k_win), to_high(v_win)).astype(BF16)
            cmp_ref[0, g] = to_low(k_cmp)[:, :dh].astype(BF16)
            cmp_ref[1, g] = to_low(v_cmp)[:, :dh].astype(BF16)


def _kv_proj(xb, w, cos, sin, seq, tm):
    T, D = xb.shape
    N = w.shape[1]
    G, dh = NSA_GROUPS, NSA_HD
    n_pos = seq // tm
    wide = jax.ShapeDtypeStruct((G, T, LANES), BF16)
    wide_spec = pl.BlockSpec((G, tm, LANES), lambda i: (0, i, 0))
    return pl.pallas_call(
        functools.partial(_kv_proj_kernel, n_pos=n_pos),
        grid=(T // tm,),
        in_specs=[
            pl.BlockSpec((tm, D), lambda i: (i, 0)),
            pl.BlockSpec((D, N), lambda i: (0, 0)),
            pl.BlockSpec((tm, LANES), lambda i: (i % n_pos, 0)),
            pl.BlockSpec((tm, LANES), lambda i: (i % n_pos, 0)),
        ],
        out_specs=[wide_spec, wide_spec, wide_spec, pl.BlockSpec((2, G, tm, dh), lambda i: (0, 0, i, 0))],
        out_shape=[wide, wide, wide, jax.ShapeDtypeStruct((2, G, T, dh), BF16)],
        compiler_params=_params("parallel"),
    )(xb, w, cos, sin)


def _q_proj_kernel(x_ref, wq_ref, wg_ref, cos_ref, sin_ref, q_ref, gate_ref):
    dh = NSA_HD
    half = dh // 2
    x = x_ref[...]
    acc = _dot_nt(wq_ref[...], x)
    cos = cos_ref[...] * dh ** -0.5
    sin = sin_ref[...] * dh ** -0.5
    for h in range(acc.shape[0] // dh):
        a = acc[h * dh:h * dh + half]
        b = acc[h * dh + half:(h + 1) * dh]
        q_ref[h * dh:h * dh + half, :] = (a * cos - b * sin).astype(q_ref.dtype)
        q_ref[h * dh + half:(h + 1) * dh, :] = (a * sin + b * cos).astype(q_ref.dtype)
    gate_ref[...] = jax.nn.sigmoid(_dot_nt(wg_ref[...], x))


def _q_proj(xb, wq_t, wg_t, cos_t, sin_t, seq, tm):
    T, D = xb.shape
    nq, ng = wq_t.shape[0], wg_t.shape[0]
    n_pos = seq // tm
    half = NSA_HD // 2
    return pl.pallas_call(
        _q_proj_kernel,
        grid=(T // tm,),
        in_specs=[
            pl.BlockSpec((tm, D), lambda i: (i, 0)),
            pl.BlockSpec((nq, D), lambda i: (0, 0)),
            pl.BlockSpec((ng, D), lambda i: (0, 0)),
            pl.BlockSpec((half, tm), lambda i: (0, i % n_pos)),
            pl.BlockSpec((half, tm), lambda i: (0, i % n_pos)),
        ],
        out_specs=[pl.BlockSpec((nq, tm), lambda i: (0, i)), pl.BlockSpec((ng, tm), lambda i: (0, i))],
        out_shape=[jax.ShapeDtypeStruct((nq, T), BF16), jax.ShapeDtypeStruct((ng, T), F32)],
        compiler_params=_params("parallel"),
    )(xb, wq_t, wg_t, cos_t, sin_t)


def _compress_kernel(a_ref, w1_ref, w2_ref, pe_ref, o_ref):
    a = a_ref[0]
    w1 = w1_ref[0]
    half = w1.shape[0] // 2
    first = _dot(a, w1[:half])
    second = _dot(a, w1[half:])
    bias = _dot(pe_ref[0], w1)[0:1]
    m = a.shape[0]
    hid = first + pltpu.roll(second, m - 1, 0) + bias
    o_ref[0] = _dot(jax.nn.gelu(hid).astype(BF16), w2_ref[0]).astype(o_ref.dtype)


def _compress(a, w1, w2, pe, tm):
    _, rows, width = a.shape
    hid = w1.shape[2]
    dh = w2.shape[2]
    return pl.pallas_call(
        _compress_kernel,
        grid=(2, rows // tm),
        in_specs=[
            pl.BlockSpec((1, tm, width), lambda s, i: (s, i, 0)),
            pl.BlockSpec((1, 2 * width, hid), lambda s, i: (s, 0, 0)),
            pl.BlockSpec((1, hid, dh), lambda s, i: (s, 0, 0)),
            pl.BlockSpec((1, 8, 2 * width), lambda s, i: (s, 0, 0)),
        ],
        out_specs=pl.BlockSpec((1, tm, dh), lambda s, i: (s, i, 0)),
        out_shape=jax.ShapeDtypeStruct((2, rows, dh), BF16),
        compiler_params=_params("parallel", "parallel"),
    )(a, w1, w2, pe)


def _nsa_kernel(q_ref, gt_ref, kc_ref, vc_ref, ks_ref, vs_ref, kw_ref, ov_ref, bc_ref, bl_ref, o_ref,
                *, tq, n_slc, n_top):
    R, dh = NSA_REP, NSA_HD
    i = pl.program_id(2)
    s0 = pl.multiple_of(i * tq, tq)
    q_all = q_ref[...]
    q_heads = [q_all[r * dh:(r + 1) * dh, :] for r in range(R)]
    t_row = s0 + lax.broadcasted_iota(jnp.int32, (1, tq), 1)

    q4 = jnp.concatenate(q_heads, axis=1)
    n_c = kc_ref.shape[1]
    sc = _dot(kc_ref[0], q4)
    t4 = s0 + (lax.broadcasted_iota(jnp.int32, sc.shape, 1) & (tq - 1))
    valid_c = lax.broadcasted_iota(jnp.int32, sc.shape, 0) * CMP_STRIDE + (CMP_LEN - 1) <= t4
    sc = jnp.where(valid_c, sc, -jnp.inf)
    mc = jnp.max(sc, 0, keepdims=True)
    mc = jnp.where(mc == -jnp.inf, 0.0, mc)
    ec = jnp.where(valid_c, jnp.exp(sc - mc), 0.0)
    p_cmp = ec / jnp.maximum(jnp.sum(ec, 0, keepdims=True), 1e-30)
    o_cmp = _dot_tn(vc_ref[0], p_cmp.astype(BF16))

    p_sum = p_cmp[:, 0:tq]
    for r in range(1, R):
        p_sum = p_sum + p_cmp[:, r * tq:(r + 1) * tq]
    p_hi = p_sum.astype(BF16)
    p_lo = (p_sum - p_hi.astype(F32)).astype(BF16)
    imp = _dot(ov_ref[...], p_hi) + _dot(ov_ref[...], p_lo)
    blk = lax.broadcasted_iota(jnp.int32, imp.shape, 0)
    cur = t_row // SLC_LEN
    forced = (blk == 0) | (blk == cur) | (blk == cur - 1)
    imp = jnp.where(forced, FORCE_SCORE, imp)
    imp = jnp.where(blk <= cur, imp, -jnp.inf)
    rank = jnp.zeros(imp.shape, jnp.int32)
    for b in range(n_slc):
        row = imp[b:b + 1, :]
        wins_tie = jnp.where(blk > b, 1, 0)
        rank = rank + jnp.where(row > imp, 1, jnp.where(row == imp, wins_tie, 0))
    selected = (rank < n_top) & (blk <= cur)
    sel_bias = jnp.where(selected, 0.0, MASK_NEG).astype(BF16)
    zero_half = jnp.zeros((dh, tq), BF16)
    q_sel = jnp.concatenate([jnp.concatenate([qh, sel_bias], axis=0) for qh in q_heads], axis=1)
    q_win = jnp.concatenate([jnp.concatenate([qh, zero_half], axis=0) for qh in q_heads], axis=1)

    def key_block(ref, blk_idx):
        return ref[0, pl.ds(pl.multiple_of(blk_idx * tq, tq), tq), :]

    def attend(state, k_blk, v_blk, queries, add_bias):
        m, l, acc = state
        s = _dot(k_blk, queries)
        if add_bias is not None:
            s = add_bias(s)
        m_new = jnp.maximum(m, jnp.max(s, 0, keepdims=True))
        alpha = jnp.exp(m - m_new)
        p = jnp.exp(s - m_new)
        l = alpha * l + jnp.sum(p, 0, keepdims=True)
        acc = alpha * acc + _dot_tn(v_blk, p.astype(BF16))
        return m_new, l, acc

    init = (jnp.full((1, R * tq), MASK_NEG, F32), jnp.zeros((1, R * tq), F32), jnp.zeros((2 * dh, R * tq), F32))
    causal_bias = bc_ref[...]

    def sel_pair(kb, state):
        k0 = pl.multiple_of(kb * 2 * tq, 2 * tq)
        return attend(state, ks_ref[0, pl.ds(k0, 2 * tq), :], vs_ref[0, pl.ds(k0, 2 * tq), :], q_sel, None)

    st_sel = lax.fori_loop(0, i // 2, sel_pair, init)
    prev_open = jnp.where(i % 2 == 1, 0.0, MASK_NEG)
    prev = jnp.maximum(i - 1, 0)
    k_fin = jnp.concatenate([key_block(ks_ref, prev), key_block(ks_ref, i)], axis=0)
    v_fin = jnp.concatenate([key_block(vs_ref, prev), key_block(vs_ref, i)], axis=0)
    m_sel, l_sel, acc_sel = attend(
        st_sel, k_fin, v_fin, q_sel,
        lambda s: jnp.concatenate([s[:tq] + prev_open, s[tq:] + causal_bias], axis=0))

    n_back = WIN // tq
    kv_win = jnp.concatenate([key_block(kw_ref, jnp.maximum(i - back, 0)) for back in range(n_back, -1, -1)],
                             axis=0)

    def window_bias(s):
        parts = []
        for n, back in enumerate(range(n_back, -1, -1)):
            blk = s[n * tq:(n + 1) * tq]
            if back == 0:
                parts.append(blk + causal_bias)
            else:
                exists = jnp.where(i >= back, 0.0, MASK_NEG)
                parts.append(blk + (bl_ref[...] + exists) if back == n_back else blk + exists)
        return jnp.concatenate(parts, axis=0)

    m_win, l_win, acc_win = attend(init, kv_win, kv_win, q_win, window_bias)

    gates = gt_ref[...]
    gate = [jnp.concatenate([gates[3 * r + c:3 * r + c + 1] for r in range(R)], axis=1) for c in range(3)]
    o = (gate[0] * o_cmp + (gate[1] * (1.0 / l_sel)) * acc_sel[:dh] + (gate[2] * (1.0 / l_win)) * acc_win[dh:])
    for r in range(R):
        o_ref[r * dh:(r + 1) * dh, :] = o[:, r * tq:(r + 1) * tq].astype(o_ref.dtype)


def _nsa_attention(q_t, gates_t, kc, vc, ks_aug, vs, kw, overlap_t, batch, seq, tq):
    G, R, dh = NSA_GROUPS, NSA_REP, NSA_HD
    T = batch * seq
    nq = seq // tq
    n_slc = seq // SLC_LEN
    n_c = kc.shape[1]
    key = lax.broadcasted_iota(jnp.int32, (tq, R * tq), 0)
    qry = lax.broadcasted_iota(jnp.int32, (tq, R * tq), 1) % tq
    causal_bias = jnp.where(key <= qry, 0.0, MASK_NEG).astype(F32)
    lower_bias = jnp.where(key > qry, 0.0, MASK_NEG).astype(F32)
    per_bg = lambda b, g, i: (g * batch + b, 0, 0)
    seq_spec = pl.BlockSpec((1, seq, LANES), lambda b, g, i: (g, b, 0))
    const = lambda b, g, i: (0, 0)
    return pl.pallas_call(
        functools.partial(_nsa_kernel, tq=tq, n_slc=n_slc, n_top=min(SLC_TOPK, n_slc)),
        grid=(batch, G, nq),
        in_specs=[
            pl.BlockSpec((R * dh, tq), lambda b, g, i: (g, b * nq + i)),
            pl.BlockSpec((GATE_ROWS, tq), lambda b, g, i: (g, b * nq + i)),
            pl.BlockSpec((1, n_c, dh), per_bg),
            pl.BlockSpec((1, n_c, dh), per_bg),
            seq_spec, seq_spec, seq_spec,
            pl.BlockSpec((dh, n_c), const),
            pl.BlockSpec((tq, R * tq), const),
            pl.BlockSpec((tq, R * tq), const),
        ],
        out_specs=pl.BlockSpec((R * dh, tq), lambda b, g, i: (g, b * nq + i)),
        out_shape=jax.ShapeDtypeStruct((G * R * dh, T), BF16),
        compiler_params=_params("parallel", "parallel", "arbitrary"),
    )(q_t, gates_t, kc, vc, ks_aug, vs, kw, overlap_t, causal_bias, lower_bias)


R_E1, R_E2, R_W1, R_W2, R_RANK1, R_RANK2 = range(6)


def _router_kernel(x_ref, whi_ref, wlo_ref, route_ref, before_ref, total_ref, cnt_ref):
    @pl.when(pl.program_id(0) == 0)
    def _():
        cnt_ref[...] = jnp.zeros_like(cnt_ref)

    x = x_ref[...]
    x_hi = x.astype(BF16)
    x_lo = (x - x_hi.astype(F32)).astype(BF16)
    logits = _dot(x_hi, whi_ref[...]) + _dot(x_hi, wlo_ref[...]) + _dot(x_lo, whi_ref[...])
    tc = logits.shape[0]
    lane = lax.broadcasted_iota(jnp.int32, logits.shape, 1).astype(F32)
    logits = jnp.where(lane < N_EXPERTS, logits, -jnp.inf)
    m1 = jnp.max(logits, -1, keepdims=True)
    i1 = jnp.min(jnp.where(logits == m1, lane, float(LANES)), -1, keepdims=True)
    rest = jnp.where(lane == i1, -jnp.inf, logits)
    m2 = jnp.max(rest, -1, keepdims=True)
    i2 = jnp.min(jnp.where(rest == m2, lane, float(LANES)), -1, keepdims=True)
    e2 = jnp.exp(m2 - m1)
    den = 1.0 + e2
    chosen = jnp.where(lane == i1, 1.0, jnp.where(lane == i2, 1.0, 0.0))
    earlier = (lax.broadcasted_iota(jnp.int32, (tc, tc), 1) < lax.broadcasted_iota(jnp.int32, (tc, tc), 0))
    before = cnt_ref[0:1, :]
    excl = _dot(jnp.where(earlier, 1.0, 0.0).astype(BF16), chosen.astype(BF16)) + before
    rank1 = jnp.sum(jnp.where(lane == i1, excl, 0.0), -1, keepdims=True)
    rank2 = jnp.sum(jnp.where(lane == i2, excl, 0.0), -1, keepdims=True)
    cols = {R_E1: i1, R_E2: i2, R_W1: 1.0 / den, R_W2: e2 / den, R_RANK1: rank1, R_RANK2: rank2}
    route = jnp.zeros_like(logits)
    for k, v in cols.items():
        route = jnp.where(lane == float(k), v, route)
    route_ref[...] = route
    before_ref[0] = jnp.broadcast_to(before, before_ref.shape[1:])
    after = before + jnp.sum(chosen, axis=0, keepdims=True)
    cnt_ref[...] = jnp.broadcast_to(after, cnt_ref.shape)
    total_ref[...] = jnp.broadcast_to(after, total_ref.shape)


def _router(x, w_hi, w_lo, tc):
    T, D = x.shape
    return pl.pallas_call(
        _router_kernel,
        grid=(T // tc,),
        in_specs=[
            pl.BlockSpec((tc, D), lambda i: (i, 0)),
            pl.BlockSpec((D, LANES), lambda i: (0, 0)),
            pl.BlockSpec((D, LANES), lambda i: (0, 0)),
        ],
        out_specs=[pl.BlockSpec((tc, LANES), lambda i: (i, 0)),
                   pl.BlockSpec((1, 8, LANES), lambda i: (i, 0, 0)),
                   pl.BlockSpec((8, LANES), lambda i: (0, 0))],
        out_shape=[jax.ShapeDtypeStruct((T, LANES), F32),
                   jax.ShapeDtypeStruct((T // tc, 8, LANES), F32),
                   jax.ShapeDtypeStruct((8, LANES), F32)],
        scratch_shapes=[pltpu.VMEM((8, LANES), F32)],
        compiler_params=_params("arbitrary"),
    )(x, w_hi, w_lo)


def _moe_plan(before, total, n_tok, tc, ts, tm):
    E = N_EXPERTS
    n_c = n_tok // tc
    n_rows = 2 * n_tok + E * tm
    n_sub, n_tiles = n_rows // ts, n_rows // tm
    n_work = E * n_c + 2 * n_sub
    i32 = jnp.int32
    tot = total[0, :E].astype(i32)
    cum = jnp.concatenate([before[:, 0, :E].astype(i32).T, tot[:, None]], axis=1)
    tiles_e = (tot + tm - 1) // tm
    tile_end = jnp.cumsum(tiles_e)
    row_off = (tile_end - tiles_e) * tm
    tile_id = jnp.arange(n_tiles, dtype=i32)
    tile_expert = jnp.minimum(jnp.sum(tile_id[:, None] >= tile_end[None, :], axis=1), E - 1).astype(i32)
    tile_valid = (tile_id < tile_end[-1]).astype(i32)
    sub = jnp.arange(n_sub, dtype=i32)
    sub_e = tile_expert[(sub * ts) // tm]
    rel0 = sub * ts - row_off[sub_e]
    rel1 = rel0 + ts
    cum_s = cum[sub_e]
    c_lo = jnp.sum(cum_s[:, 1:] <= rel0[:, None], axis=1).astype(i32)
    c_hi = jnp.sum(cum_s[:, :-1] < rel1[:, None], axis=1).astype(i32) - 1
    c_lo = jnp.clip(c_lo, 0, n_c - 1)
    c_hi = jnp.clip(c_hi, c_lo, n_c - 1)
    n_j = c_hi - c_lo + 1
    ends = jnp.cumsum(n_j)
    starts = ends - n_j
    n_valid = ends[-1]
    w = jnp.arange(n_work, dtype=i32)
    valid = w < n_valid
    jw = jnp.minimum(jnp.sum(w[:, None] >= ends[None, :], axis=1), n_sub - 1).astype(i32)
    cw = jnp.where(valid, c_lo[jw] + (w - starts[jw]), c_hi[n_sub - 1]).astype(i32)
    first = valid & (w == starts[jw])
    last = valid & (w == ends[jw] - 1)
    disp = (jw, cw, valid.astype(i32), first.astype(i32), last.astype(i32))
    order = jnp.argsort(jnp.where(valid, cw * n_sub + jw, n_c * n_sub + w))
    cj, cc, cv = jw[order], cw[order], valid[order]
    last_j, last_c = cj[n_valid - 1], cc[n_valid - 1]
    cj = jnp.where(cv, cj, last_j)
    cc = jnp.where(cv, cc, last_c)
    prev_c = jnp.concatenate([jnp.full((1,), -1, i32), cc[:-1]])
    next_c = jnp.concatenate([cc[1:], jnp.full((1,), -1, i32)])
    next_v = jnp.concatenate([cv[1:], jnp.zeros((1,), bool)])
    cfirst = cv & (cc != prev_c)
    clast = cv & ((cc != next_c) | ~next_v)
    comb = (cj.astype(i32), cc.astype(i32), cv.astype(i32), cfirst.astype(i32), clast.astype(i32))
    return row_off, tile_expert, tile_valid, disp, comb, n_rows, n_work


def _dispatch_kernel(jw, cw, vw, fw, lw, x_ref, route_ref, pos_ref, xs_ref, gs_ref, acc_ref, gacc_ref):
    w = pl.program_id(0)

    @pl.when(vw[w] == 1)
    def _():
        ts, tc = acc_ref.shape[0], x_ref.shape[0]
        row_id = jw[w] * ts + lax.broadcasted_iota(jnp.int32, (ts, 1), 0)
        pos = pos_ref[0]
        hit1 = jnp.where(pos[0:1, :] == row_id, 1.0, 0.0).astype(BF16)
        hit2 = jnp.where(pos[1:2, :] == row_id, 1.0, 0.0).astype(BF16)
        rows = _dot(hit1 + hit2, x_ref[...])
        route = route_ref[...]
        lane = lax.broadcasted_iota(jnp.int32, (tc, LANES), 1)
        pieces = jnp.zeros((tc, LANES), F32)
        for slot, col in enumerate((R_W1, R_W2)):
            rem = route[:, col:col + 1]
            for k in range(3):
                piece = rem.astype(BF16).astype(F32)
                pieces = jnp.where(lane == 3 * slot + k, piece, pieces)
                rem = rem - piece
        pieces = pieces.astype(BF16)
        lane_s = lax.broadcasted_iota(jnp.int32, (ts, LANES), 1)
        g = jnp.where(lane_s < 3, _dot(hit1, pieces), 0.0) + jnp.where((lane_s >= 3) & (lane_s < 6),
                                                                        _dot(hit2, pieces), 0.0)
        g = jnp.broadcast_to(jnp.sum(g, -1, keepdims=True), (ts, LANES))

        @pl.when(fw[w] == 1)
        def _():
            acc_ref[...] = rows
            gacc_ref[...] = g

        @pl.when(fw[w] == 0)
        def _():
            acc_ref[...] += rows
            gacc_ref[...] += g

        @pl.when(lw[w] == 1)
        def _():
            xs_ref[...] = acc_ref[...].astype(xs_ref.dtype)
            gs_ref[...] = gacc_ref[...]


def _dispatch(xb, route, pos_rows, disp, n_rows, n_work, tc, ts):
    T, D = xb.shape
    grid_spec = pltpu.PrefetchScalarGridSpec(
        num_scalar_prefetch=5,
        grid=(n_work,),
        in_specs=[
            pl.BlockSpec((tc, D), lambda w, jw, cw, vw, fw, lw: (cw[w], 0)),
            pl.BlockSpec((tc, LANES), lambda w, jw, cw, vw, fw, lw: (cw[w], 0)),
            pl.BlockSpec((1, 8, tc), lambda w, jw, cw, vw, fw, lw: (cw[w], 0, 0)),
        ],
        out_specs=[pl.BlockSpec((ts, D), lambda w, jw, cw, vw, fw, lw: (jw[w], 0)),
                   pl.BlockSpec((ts, LANES), lambda w, jw, cw, vw, fw, lw: (jw[w], 0))],
        scratch_shapes=[pltpu.VMEM((ts, D), F32), pltpu.VMEM((ts, LANES), F32)],
    )
    return pl.pallas_call(
        _dispatch_kernel,
        grid_spec=grid_spec,
        out_shape=[jax.ShapeDtypeStruct((n_rows, D), BF16), jax.ShapeDtypeStruct((n_rows, LANES), F32)],
        compiler_params=_params("arbitrary"),
    )(*disp, xb, route, pos_rows)


def _moe_kernel(te, tv, xs_ref, gs_ref, wa_ref, wb_ref, wo_ref, y_ref, acc_ref):
    i = pl.program_id(0)
    f = pl.program_id(1)
    last_f = pl.num_programs(1) - 1

    @pl.when(tv[i] == 1)
    def _():
        xs = xs_ref[...]
        h = (_silu(_dot(xs, wa_ref[0])) * _dot(xs, wb_ref[0]) * gs_ref[:, 0:1]).astype(BF16)
        contrib = _dot(h, wo_ref[0])

        @pl.when(f == 0)
        def _():
            acc_ref[...] = contrib

        @pl.when(f > 0)
        def _():
            acc_ref[...] += contrib

        @pl.when(f == last_f)
        def _():
            y_ref[...] = acc_ref[...].astype(y_ref.dtype)

    @pl.when((tv[i] == 0) & (f == last_f))
    def _():
        y_ref[...] = jnp.zeros_like(y_ref)


def _moe(xs, gs, tile_expert, tile_valid, w_in, w_out, tm, tf):
    n_rows, D = xs.shape
    E, F, _ = w_out.shape
    nf = F // tf
    f_of = lambda i, f, te, tv: jnp.where(tv[i] == 1, f, nf - 1)
    grid_spec = pltpu.PrefetchScalarGridSpec(
        num_scalar_prefetch=2,
        grid=(n_rows // tm, nf),
        in_specs=[
            pl.BlockSpec((tm, D), lambda i, f, te, tv: (i, 0)),
            pl.BlockSpec((tm, LANES), lambda i, f, te, tv: (i, 0)),
            pl.BlockSpec((1, D, tf), lambda i, f, te, tv: (te[i], 0, f_of(i, f, te, tv))),
            pl.BlockSpec((1, D, tf), lambda i, f, te, tv: (te[i], 0, nf + f_of(i, f, te, tv))),
            pl.BlockSpec((1, tf, D), lambda i, f, te, tv: (te[i], f_of(i, f, te, tv), 0)),
        ],
        out_specs=pl.BlockSpec((tm, D), lambda i, f, te, tv: (i, 0)),
        scratch_shapes=[pltpu.VMEM((tm, D), F32)],
    )
    return pl.pallas_call(
        _moe_kernel,
        grid_spec=grid_spec,
        out_shape=jax.ShapeDtypeStruct((n_rows, D), BF16),
        compiler_params=_params("parallel", "arbitrary"),
    )(tile_expert, tile_valid, xs, gs, w_in, w_in, w_out)


def _combine_kernel(cj, cc, cv, cf, cl, y_ref, pos_ref, x_ref, g_ref, b_ref, o_ref, acc_ref):
    w = pl.program_id(0)

    @pl.when(cv[w] == 1)
    def _():
        ts = y_ref.shape[0]
        col_id = cj[w] * ts + lax.broadcasted_iota(jnp.int32, (1, ts), 1)
        pos = pos_ref[...]
        hit = (jnp.where(pos[:, 0:1] == col_id, 1.0, 0.0).astype(BF16)
               + jnp.where(pos[:, 1:2] == col_id, 1.0, 0.0).astype(BF16))
        contrib = _dot(hit, y_ref[...])

        @pl.when(cf[w] == 1)
        def _():
            acc_ref[...] = contrib

        @pl.when(cf[w] == 0)
        def _():
            acc_ref[...] += contrib

        @pl.when(cl[w] == 1)
        def _():
            o_ref[...] = _layer_norm(DN_ALPHA * x_ref[...] + acc_ref[...], g_ref[...], b_ref[...])


def _combine(y, pos_cols, x, g, b, comb, n_work, tc, ts):
    T, D = x.shape
    grid_spec = pltpu.PrefetchScalarGridSpec(
        num_scalar_prefetch=5,
        grid=(n_work,),
        in_specs=[
            pl.BlockSpec((ts, D), lambda w, cj, cc, cv, cf, cl: (cj[w], 0)),
            pl.BlockSpec((tc, LANES), lambda w, cj, cc, cv, cf, cl: (cc[w], 0)),
            pl.BlockSpec((tc, D), lambda w, cj, cc, cv, cf, cl: (cc[w], 0)),
            pl.BlockSpec((1, D), lambda w, cj, cc, cv, cf, cl: (0, 0)),
            pl.BlockSpec((1, D), lambda w, cj, cc, cv, cf, cl: (0, 0)),
        ],
        out_specs=pl.BlockSpec((tc, D), lambda w, cj, cc, cv, cf, cl: (cc[w], 0)),
        scratch_shapes=[pltpu.VMEM((tc, D), F32)],
    )
    return pl.pallas_call(
        _combine_kernel,
        grid_spec=grid_spec,
        out_shape=jax.ShapeDtypeStruct((T, D), F32),
        compiler_params=_params("arbitrary"),
    )(*comb, y, pos_cols, x, g.reshape(1, D), b.reshape(1, D))


def _moe_layer(x, xb, w_router, w_in, w_out, g, b, tc, tm, tf):
    T, D = x.shape
    ts = tc
    wr = jnp.pad(w_router, ((0, 0), (0, LANES - N_EXPERTS)))
    wr_hi = wr.astype(BF16)
    wr_lo = (wr - wr_hi.astype(F32)).astype(BF16)
    route, before, total = _router(x, wr_hi, wr_lo, tc)
    row_off, tile_expert, tile_valid, disp, comb, n_rows, n_work = _moe_plan(before, total, T, tc, ts, tm)
    e1 = route[:, R_E1].astype(jnp.int32)
    e2 = route[:, R_E2].astype(jnp.int32)
    pos1 = row_off[e1] + route[:, R_RANK1].astype(jnp.int32)
    pos2 = row_off[e2] + route[:, R_RANK2].astype(jnp.int32)
    pos = jnp.stack([pos1, pos2])
    pos_rows = jnp.pad(pos.reshape(2, T // tc, tc).transpose(1, 0, 2), ((0, 0), (0, 6), (0, 0)),
                       constant_values=-1)
    pos_cols = jnp.pad(pos.T, ((0, 0), (0, LANES - 2)), constant_values=-1)
    xs, gs = _dispatch(xb, route, pos_rows, disp, n_rows, n_work, tc, ts)
    y = _moe(xs, gs, tile_expert, tile_valid, w_in, w_out, tm, tf)
    return _combine(y, pos_cols, x, g, b, comb, n_work, tc, ts)


def kernel(x, ret_w_in, ret_gn_g, ret_w_out, nsa_w_kv, cmp_k_pe, cmp_k_w1, cmp_k_w2, cmp_v_pe, cmp_v_w1, cmp_v_w2,
           nsa_w_q, nsa_w_out, ffn_w_in, ffn_w_out, moe_router, moe_w_in, moe_w_out, ln_g, ln_b):
    B, S, D = x.shape
    T = B * S
    G, R, dh = NSA_GROUPS, NSA_REP, NSA_HD
    assert ret_w_in.shape[0] == 1 and nsa_w_q.shape[0] == 1 and ln_g.shape[0] == DEPTH
    dk = D // RET_HEADS
    dv = 2 * dk
    n_slc = S // SLC_LEN
    tq = 256
    assert n_slc <= dh and S % tq == 0 and WIN % tq == 0 and R * 3 <= GATE_ROWS
    tm = min(1024, S)

    xf = x.reshape(T, D)
    xb = xf.astype(BF16)

    cos_r, sin_r = _rope_tables(S, dk, dk)
    qkvg = _ret_proj(xb, ret_w_in[0].astype(BF16), cos_r[:, :dk // 2], -sin_r[:, :dk // 2], S, dk, tm)
    y = _retention(qkvg, ret_gn_g[0], B, S, dk, dv, rows=min(512, S))
    x1, x1b = _proj_ln(y, ret_w_out[0].astype(BF16), xf, ln_g[0, 0], ln_b[0, 0], tm=512)
    x2, x2b = _ffn(x1b, ffn_w_in[0].astype(BF16), ffn_w_out[0].astype(BF16), x1, ln_g[0, 1], ln_b[0, 1],
                   tm=tm, tf=256)

    cos_n, sin_n = _rope_tables(S, dh, LANES)
    ks_aug, vs, kw, cmp_kv = _kv_proj(x2b, nsa_w_kv.astype(BF16), cos_n, sin_n, S, tm=512)
    n_c = S // CMP_STRIDE
    cmp_in = cmp_kv.reshape(2, G * B * n_c, CMP_STRIDE * dh)
    w1 = jnp.stack([cmp_k_w1, cmp_v_w1]).astype(BF16)
    w2 = jnp.stack([cmp_k_w2, cmp_v_w2]).astype(BF16)
    pe = jnp.stack([cmp_k_pe, cmp_v_pe]).reshape(2, 1, CMP_LEN * dh)
    pe = jnp.broadcast_to(pe, (2, 8, CMP_LEN * dh)).astype(BF16)
    kvc = _compress(cmp_in, w1, w2, pe, tm=min(1024, G * B * n_c)).reshape(2, G * B, n_c, dh)
    cs = jnp.arange(n_c) * CMP_STRIDE
    ss = jnp.arange(dh) * SLC_LEN
    overlap_t = jnp.clip(jnp.minimum(cs[None, :] + CMP_LEN, ss[:, None] + SLC_LEN)
                         - jnp.maximum(cs[None, :], ss[:, None]), 0).astype(F32) / CMP_STRIDE
    overlap_t = jnp.where(jnp.arange(dh)[:, None] < n_slc, overlap_t, 0.0).astype(BF16)

    nq_cols = NSA_HEADS * dh
    wq = nsa_w_q[0]
    wq_t = wq[:, :nq_cols].T.astype(BF16)
    wg_t = jnp.pad(wq[:, nq_cols:].T.reshape(G, R * 3, D), ((0, 0), (0, GATE_ROWS - R * 3), (0, 0)))
    wg_t = wg_t.reshape(G * GATE_ROWS, D).astype(BF16)
    ang = _rope_angles(S, dh).T
    q_t, gates_t = _q_proj(x2b, wq_t, wg_t, jnp.cos(ang), jnp.sin(ang), S, tm=512)
    attn_t = _nsa_attention(q_t, gates_t, kvc[0], kvc[1], ks_aug, vs, kw, overlap_t, B, S, tq)
    x3, x3b = _proj_ln(attn_t, nsa_w_out[0].astype(BF16), x2, ln_g[1, 0], ln_b[1, 0], tm=512, y_transposed=True)

    out = _moe_layer(x3, x3b, moe_router[0], moe_w_in[0].astype(BF16), moe_w_out[0].astype(BF16),
                     ln_g[1, 1], ln_b[1, 1], tc=512, tm=1024, tf=512)
    return out.reshape(B, S, D)
```

```python
import functools

import jax
import jax.numpy as jnp
from jax import lax
from jax.experimental import pallas as pl
from jax.experimental.pallas import tpu as pltpu

F32 = jnp.float32
BF16 = jnp.bfloat16

DEPTH = 2
ROPE_THETA = 10000.0
LN_EPS = 1e-5
DN_ALPHA = (2.0 * DEPTH) ** 0.25

RET_HEADS = 4
RET_CHUNK = 128

NSA_HEADS = 16
NSA_GROUPS = 4
NSA_REP = NSA_HEADS // NSA_GROUPS
NSA_HD = 64
CMP_LEN = 32
CMP_STRIDE = 16
SLC_LEN = 64
SLC_TOPK = 8
WIN = 512
FORCE_SCORE = 1e9

N_EXPERTS = 8

LANES = 128
GATE_ROWS = 16
MASK_NEG = -1e30
VMEM_LIMIT = 56 * 1024 * 1024


def _params(*sem):
    return pltpu.CompilerParams(dimension_semantics=sem, vmem_limit_bytes=VMEM_LIMIT)


def _dot(a, b):
    return jnp.dot(a, b, preferred_element_type=F32)


def _dot_nt(a, b):
    return lax.dot_general(a, b, (((1,), (1,)), ((), ())), preferred_element_type=F32)


def _dot_tn(a, b):
    return lax.dot_general(a, b, (((0,), (0,)), ((), ())), preferred_element_type=F32)


def _layer_norm(z, g, b):
    mu = jnp.mean(z, -1, keepdims=True)
    zc = z - mu
    var = jnp.mean(zc * zc, -1, keepdims=True)
    return zc * lax.rsqrt(var + LN_EPS) * g + b


def _silu(a):
    return a * jax.nn.sigmoid(a)


def _rope_angles(seq, dim):
    inv = 1.0 / (ROPE_THETA ** (jnp.arange(0, dim, 2, dtype=F32) / dim))
    return jnp.arange(seq, dtype=F32)[:, None] * inv[None, :]


def _rope_tables(seq, dim, width):
    ang = _rope_angles(seq, dim)
    cos, sin = jnp.cos(ang), jnp.sin(ang)
    cos_h = jnp.concatenate([cos, cos], -1)
    sin_h = jnp.concatenate([-sin, sin], -1)
    rep = width // dim
    return jnp.tile(cos_h, (1, rep)), jnp.tile(sin_h, (1, rep))


def _ret_proj_kernel(x_ref, w_ref, cos_ref, sin_ref, o_ref, *, dk, n_rot):
    j = pl.program_id(1)
    acc = _dot(x_ref[...], w_ref[...])
    tn = acc.shape[1]
    half = dk // 2

    @pl.when(j < n_rot)
    def _():
        scale = jnp.where(j == n_rot - 1, dk ** -0.5, 1.0).astype(F32)
        cos = cos_ref[...] * scale
        sin = sin_ref[...] * scale
        for h in range(tn // dk):
            a = acc[:, h * dk:h * dk + half]
            b = acc[:, h * dk + half:(h + 1) * dk]
            o_ref[:, h * dk:h * dk + half] = (a * cos - b * sin).astype(o_ref.dtype)
            o_ref[:, h * dk + half:(h + 1) * dk] = (a * sin + b * cos).astype(o_ref.dtype)

    @pl.when(j >= n_rot)
    def _():
        o_ref[...] = acc.astype(o_ref.dtype)


def _ret_proj(xb, w, cos, sin, seq, dk, tm):
    T, D = xb.shape
    N = w.shape[1]
    tn = RET_HEADS * dk
    half = dk // 2
    n_pos = seq // tm
    return pl.pallas_call(
        functools.partial(_ret_proj_kernel, dk=dk, n_rot=2),
        grid=(T // tm, N // tn),
        in_specs=[
            pl.BlockSpec((tm, D), lambda i, j: (i, 0)),
            pl.BlockSpec((D, tn), lambda i, j: (0, j)),
            pl.BlockSpec((tm, half), lambda i, j: (i % n_pos, 0)),
            pl.BlockSpec((tm, half), lambda i, j: (i % n_pos, 0)),
        ],
        out_specs=pl.BlockSpec((tm, tn), lambda i, j: (i, j)),
        out_shape=jax.ShapeDtypeStruct((T, N), BF16),
        compiler_params=_params("parallel", "arbitrary"),
    )(xb, w, cos, sin)


def _retention_kernel(q_ref, k_ref, v_ref, g_ref, dec_ref, qd_ref, kd_ref, gn_ref, o_ref, state_ref,
                      *, chunk, n_chunks):
    @pl.when(pl.program_id(2) == 0)
    def _():
        state_ref[...] = jnp.zeros_like(state_ref)

    decay = dec_ref[0]
    qd = qd_ref[0]
    kd = kd_ref[0]
    cd = qd[chunk - 1:chunk, :]
    gn = gn_ref[0]
    for c in range(n_chunks):
        rows = slice(c * chunk, (c + 1) * chunk)
        q = q_ref[rows, :]
        k = k_ref[rows, :]
        v = v_ref[rows, :]
        scores = _dot_nt(q, k) * decay
        inner = _dot(scores.astype(BF16), v)
        state = state_ref[...]
        cross = _dot(q, state.astype(BF16)) * qd
        out = inner + cross
        k_dec = (k.astype(F32) * kd).astype(BF16)
        state_ref[...] = state * cd + _dot_tn(k_dec, v)
        mu = jnp.mean(out, -1, keepdims=True)
        oc = out - mu
        var = jnp.mean(oc * oc, -1, keepdims=True)
        normed = oc * lax.rsqrt(var + LN_EPS) * gn
        o_ref[rows, :] = (_silu(g_ref[rows, :].astype(F32)) * normed).astype(o_ref.dtype)


def _retention(qkvg, gn_g, batch, seq, dk, dv, rows):
    H, C = RET_HEADS, RET_CHUNK
    T = batch * seq
    n_steps = seq // rows
    log_gamma = jnp.log1p(-(2.0 ** (-5.0 - jnp.arange(H, dtype=F32))))
    i = jnp.arange(C, dtype=F32)
    rel = i[:, None] - i[None, :]
    intra = jnp.where(rel >= 0, jnp.exp(log_gamma[:, None, None] * jnp.maximum(rel, 0.0)), 0.0)
    q_decay = jnp.exp(log_gamma[:, None] * (i + 1.0))[:, :, None]
    k_decay = jnp.exp(log_gamma[:, None] * (C - 1.0 - i))[:, :, None]
    v_off = (2 * H * dk) // dv
    g_off = (2 * H * dk + H * dv) // dv
    row = lambda b, h, n: b * n_steps + n
    return pl.pallas_call(
        functools.partial(_retention_kernel, chunk=C, n_chunks=rows // C),
        grid=(batch, H, n_steps),
        in_specs=[
            pl.BlockSpec((rows, dk), lambda b, h, n: (row(b, h, n), h)),
            pl.BlockSpec((rows, dk), lambda b, h, n: (row(b, h, n), H + h)),
            pl.BlockSpec((rows, dv), lambda b, h, n: (row(b, h, n), v_off + h)),
            pl.BlockSpec((rows, dv), lambda b, h, n: (row(b, h, n), g_off + h)),
            pl.BlockSpec((1, C, C), lambda b, h, n: (h, 0, 0)),
            pl.BlockSpec((1, C, 1), lambda b, h, n: (h, 0, 0)),
            pl.BlockSpec((1, C, 1), lambda b, h, n: (h, 0, 0)),
            pl.BlockSpec((1, 1, dv), lambda b, h, n: (h, 0, 0)),
        ],
        out_specs=pl.BlockSpec((rows, dv), lambda b, h, n: (row(b, h, n), h)),
        out_shape=jax.ShapeDtypeStruct((T, H * dv), BF16),
        scratch_shapes=[pltpu.VMEM((dk, dv), F32)],
        compiler_params=_params("parallel", "parallel", "arbitrary"),
    )(qkvg, qkvg, qkvg, qkvg, intra, q_decay, k_decay, gn_g.reshape(H, 1, dv))


def _proj_ln_kernel(y_ref, w_ref, x_ref, g_ref, b_ref, o_ref, ob_ref, *, y_transposed):
    mix = _dot_tn(y_ref[...], w_ref[...]) if y_transposed else _dot(y_ref[...], w_ref[...])
    o = _layer_norm(DN_ALPHA * x_ref[...] + mix, g_ref[...], b_ref[...])
    o_ref[...] = o
    ob_ref[...] = o.astype(BF16)


def _proj_ln(y, w, x, g, b, tm, y_transposed=False):
    K, D = w.shape
    T = x.shape[0]
    y_spec = pl.BlockSpec((K, tm), lambda i: (0, i)) if y_transposed else pl.BlockSpec((tm, K), lambda i: (i, 0))
    return pl.pallas_call(
        functools.partial(_proj_ln_kernel, y_transposed=y_transposed),
        grid=(T // tm,),
        in_specs=[
            y_spec,
            pl.BlockSpec((K, D), lambda i: (0, 0)),
            pl.BlockSpec((tm, D), lambda i: (i, 0)),
            pl.BlockSpec((1, D), lambda i: (0, 0)),
            pl.BlockSpec((1, D), lambda i: (0, 0)),
        ],
        out_specs=[pl.BlockSpec((tm, D), lambda i: (i, 0)), pl.BlockSpec((tm, D), lambda i: (i, 0))],
        out_shape=[jax.ShapeDtypeStruct((T, D), F32), jax.ShapeDtypeStruct((T, D), BF16)],
        compiler_params=_params("parallel"),
    )(y, w, x, g.reshape(1, D), b.reshape(1, D))


def _ffn_kernel(xb_ref, wi_ref, wo_ref, x_ref, g_ref, b_ref, o_ref, ob_ref, *, tf):
    xb = xb_ref[...]
    F = wo_ref.shape[0]
    acc = None
    for f in range(F // tf):
        cols = slice(f * tf, (f + 1) * tf)
        gate_cols = slice(F + f * tf, F + (f + 1) * tf)
        h = (_silu(_dot(xb, wi_ref[:, cols])) * _dot(xb, wi_ref[:, gate_cols])).astype(BF16)
        contrib = _dot(h, wo_ref[cols, :])
        acc = contrib if acc is None else acc + contrib
    o = _layer_norm(DN_ALPHA * x_ref[...] + acc, g_ref[...], b_ref[...])
    o_ref[...] = o
    ob_ref[...] = o.astype(BF16)


def _ffn(xb, w_in, w_out, x, g, b, tm, tf):
    T, D = xb.shape
    F = w_out.shape[0]
    assert F % tf == 0
    return pl.pallas_call(
        functools.partial(_ffn_kernel, tf=tf),
        grid=(T // tm,),
        in_specs=[
            pl.BlockSpec((tm, D), lambda i: (i, 0)),
            pl.BlockSpec((D, 2 * F), lambda i: (0, 0)),
            pl.BlockSpec((F, D), lambda i: (0, 0)),
            pl.BlockSpec((tm, D), lambda i: (i, 0)),
            pl.BlockSpec((1, D), lambda i: (0, 0)),
            pl.BlockSpec((1, D), lambda i: (0, 0)),
        ],
        out_specs=[pl.BlockSpec((tm, D), lambda i: (i, 0)), pl.BlockSpec((tm, D), lambda i: (i, 0))],
        out_shape=[jax.ShapeDtypeStruct((T, D), F32), jax.ShapeDtypeStruct((T, D), BF16)],
        compiler_params=_params("parallel"),
    )(xb, w_in, w_out, x, g.reshape(1, D), b.reshape(1, D))


def _rope64(acc, cos, sin_signed):
    half = NSA_HD // 2
    lane = lax.broadcasted_iota(jnp.int32, acc.shape, 1)
    first = (lane % NSA_HD) < half
    rot = jnp.where(first, pltpu.roll(acc, LANES - half, 1), pltpu.roll(acc, half, 1))
    return acc * cos + rot * sin_signed


def _kv_proj_kernel(x_ref, w_ref, cos_ref, sin_ref, ks_ref, vs_ref, kw_ref, cmp_ref, *, n_pos):
    G, dh = NSA_GROUPS, NSA_HD
    acc = _dot(x_ref[...], w_ref[...])
    tm = acc.shape[0]
    cos, sin = cos_ref[...], sin_ref[...]
    lane = lax.broadcasted_iota(jnp.int32, (tm, LANES), 1)
    low = lane < dh
    pos = (pl.program_id(0) % n_pos) * tm + lax.broadcasted_iota(jnp.int32, (tm, LANES), 0)
    block_onehot = jnp.where(lane - dh == pos // SLC_LEN, 1.0, 0.0)
    swap = lambda a: pltpu.roll(a, dh, 1)

    def slab(branch, s):
        off = branch * G * dh + s * LANES
        return acc[:, off:off + LANES]

    for s in range(G * dh // LANES):
        k_cmp, v_cmp = _rope64(slab(0, s), cos, sin), slab(1, s)
        k_slc, v_slc = _rope64(slab(2, s), cos, sin), slab(3, s)
        k_win, v_win = _rope64(slab(4, s), cos, sin), slab(5, s)
        for h in range(2):
            g = 2 * s + h
            to_low = (lambda a: a) if h == 0 else swap
            to_high = swap if h == 0 else (lambda a: a)
            ks_ref[g] = jnp.where(low, to_low(k_slc), block_onehot).astype(BF16)
            vs_ref[g] = jnp.where(low, to_low(v_slc), 0.0).astype(BF16)
            kw_ref[g] = jnp.where(low, to_low(k_win), to_high(v_win)).astype(BF16)
            cmp_ref[0, g] = to_low(k_cmp)[:, :dh].astype(BF16)
            cmp_ref[1, g] = to_low(v_cmp)[:, :dh].astype(BF16)


def _kv_proj(xb, w, cos, sin, seq, tm):
    T, D = xb.shape
    N = w.shape[1]
    G, dh = NSA_GROUPS, NSA_HD
    n_pos = seq // tm
    wide = jax.ShapeDtypeStruct((G, T, LANES), BF16)
    wide_spec = pl.BlockSpec((G, tm, LANES), lambda i: (0, i, 0))
    return pl.pallas_call(
        functools.partial(_kv_proj_kernel, n_pos=n_pos),
        grid=(T // tm,),
        in_specs=[
            pl.BlockSpec((tm, D), lambda i: (i, 0)),
            pl.BlockSpec((D, N), lambda i: (0, 0)),
            pl.BlockSpec((tm, LANES), lambda i: (i % n_pos, 0)),
            pl.BlockSpec((tm, LANES), lambda i: (i % n_pos, 0)),
        ],
        out_specs=[wide_spec, wide_spec, wide_spec, pl.BlockSpec((2, G, tm, dh), lambda i: (0, 0, i, 0))],
        out_shape=[wide, wide, wide, jax.ShapeDtypeStruct((2, G, T, dh), BF16)],
        compiler_params=_params("parallel"),
    )(xb, w, cos, sin)


def _q_proj_kernel(x_ref, wq_ref, wg_ref, cos_ref, sin_ref, q_ref, gate_ref):
    dh = NSA_HD
    half = dh // 2
    x = x_ref[...]
    acc = _dot_nt(wq_ref[...], x)
    cos = cos_ref[...] * dh ** -0.5
    sin = sin_ref[...] * dh ** -0.5
    for h in range(acc.shape[0] // dh):
        a = acc[h * dh:h * dh + half]
        b = acc[h * dh + half:(h + 1) * dh]
        q_ref[h * dh:h * dh + half, :] = (a * cos - b * sin).astype(q_ref.dtype)
        q_ref[h * dh + half:(h + 1) * dh, :] = (a * sin + b * cos).astype(q_ref.dtype)
    gate_ref[...] = jax.nn.sigmoid(_dot_nt(wg_ref[...], x))


def _q_proj(xb, wq_t, wg_t, cos_t, sin_t, seq, tm):
    T, D = xb.shape
    nq, ng = wq_t.shape[0], wg_t.shape[0]
    n_pos = seq // tm
    half = NSA_HD // 2
    return pl.pallas_call(
        _q_proj_kernel,
        grid=(T // tm,),
        in_specs=[
            pl.BlockSpec((tm, D), lambda i: (i, 0)),
            pl.BlockSpec((nq, D), lambda i: (0, 0)),
            pl.BlockSpec((ng, D), lambda i: (0, 0)),
            pl.BlockSpec((half, tm), lambda i: (0, i % n_pos)),
            pl.BlockSpec((half, tm), lambda i: (0, i % n_pos)),
        ],
        out_specs=[pl.BlockSpec((nq, tm), lambda i: (0, i)), pl.BlockSpec((ng, tm), lambda i: (0, i))],
        out_shape=[jax.ShapeDtypeStruct((nq, T), BF16), jax.ShapeDtypeStruct((ng, T), F32)],
        compiler_params=_params("parallel"),
    )(xb, wq_t, wg_t, cos_t, sin_t)


def _compress_kernel(a_ref, w1_ref, w2_ref, pe_ref, o_ref):
    a = a_ref[0]
    w1 = w1_ref[0]
    half = w1.shape[0] // 2
    first = _dot(a, w1[:half])
    second = _dot(a, w1[half:])
    bias = _dot(pe_ref[0], w1)[0:1]
    m = a.shape[0]
    hid = first + pltpu.roll(second, m - 1, 0) + bias
    o_ref[0] = _dot(jax.nn.gelu(hid).astype(BF16), w2_ref[0]).astype(o_ref.dtype)


def _compress(a, w1, w2, pe, tm):
    _, rows, width = a.shape
    hid = w1.shape[2]
    dh = w2.shape[2]
    return pl.pallas_call(
        _compress_kernel,
        grid=(2, rows // tm),
        in_specs=[
            pl.BlockSpec((1, tm, width), lambda s, i: (s, i, 0)),
            pl.BlockSpec((1, 2 * width, hid), lambda s, i: (s, 0, 0)),
            pl.BlockSpec((1, hid, dh), lambda s, i: (s, 0, 0)),
            pl.BlockSpec((1, 8, 2 * width), lambda s, i: (s, 0, 0)),
        ],
        out_specs=pl.BlockSpec((1, tm, dh), lambda s, i: (s, i, 0)),
        out_shape=jax.ShapeDtypeStruct((2, rows, dh), BF16),
        compiler_params=_params("parallel", "parallel"),
    )(a, w1, w2, pe)


def _nsa_kernel(q_ref, gt_ref, kc_ref, vc_ref, ks_ref, vs_ref, kw_ref, ov_ref, bc_ref, bl_ref, o_ref,
                *, tq, n_slc, n_top):
    R, dh = NSA_REP, NSA_HD
    i = pl.program_id(2)
    s0 = pl.multiple_of(i * tq, tq)
    q_all = q_ref[...]
    q_heads = [q_all[r * dh:(r + 1) * dh, :] for r in range(R)]
    t_row = s0 + lax.broadcasted_iota(jnp.int32, (1, tq), 1)

    q4 = jnp.concatenate(q_heads, axis=1)
    n_c = kc_ref.shape[1]
    sc = _dot(kc_ref[0], q4)
    t4 = s0 + (lax.broadcasted_iota(jnp.int32, sc.shape, 1) & (tq - 1))
    valid_c = lax.broadcasted_iota(jnp.int32, sc.shape, 0) * CMP_STRIDE + (CMP_LEN - 1) <= t4
    sc = jnp.where(valid_c, sc, -jnp.inf)
    mc = jnp.max(sc, 0, keepdims=True)
    mc = jnp.where(mc == -jnp.inf, 0.0, mc)
    ec = jnp.where(valid_c, jnp.exp(sc - mc), 0.0)
    p_cmp = ec / jnp.maximum(jnp.sum(ec, 0, keepdims=True), 1e-30)
    o_cmp = _dot_tn(vc_ref[0], p_cmp.astype(BF16))

    p_sum = p_cmp[:, 0:tq]
    for r in range(1, R):
        p_sum = p_sum + p_cmp[:, r * tq:(r + 1) * tq]
    p_hi = p_sum.astype(BF16)
    p_lo = (p_sum - p_hi.astype(F32)).astype(BF16)
    imp = _dot(ov_ref[...], p_hi) + _dot(ov_ref[...], p_lo)
    blk = lax.broadcasted_iota(jnp.int32, imp.shape, 0)
    cur = t_row // SLC_LEN
    forced = (blk == 0) | (blk == cur) | (blk == cur - 1)
    imp = jnp.where(forced, FORCE_SCORE, imp)
    imp = jnp.where(blk <= cur, imp, -jnp.inf)
    rank = jnp.zeros(imp.shape, jnp.int32)
    for b in range(n_slc):
        row = imp[b:b + 1, :]
        wins_tie = jnp.where(blk > b, 1, 0)
        rank = rank + jnp.where(row > imp, 1, jnp.where(row == imp, wins_tie, 0))
    selected = (rank < n_top) & (blk <= cur)
    sel_bias = jnp.where(selected, 0.0, MASK_NEG).astype(BF16)
    zero_half = jnp.zeros((dh, tq), BF16)
    q_sel = jnp.concatenate([jnp.concatenate([qh, sel_bias], axis=0) for qh in q_heads], axis=1)
    q_win = jnp.concatenate([jnp.concatenate([qh, zero_half], axis=0) for qh in q_heads], axis=1)

    def key_block(ref, blk_idx):
        return ref[0, pl.ds(pl.multiple_of(blk_idx * tq, tq), tq), :]

    def attend(state, k_blk, v_blk, queries, add_bias):
        m, l, acc = state
        s = _dot(k_blk, queries)
        if add_bias is not None:
            s = add_bias(s)
        m_new = jnp.maximum(m, jnp.max(s, 0, keepdims=True))
        alpha = jnp.exp(m - m_new)
        p = jnp.exp(s - m_new)
        l = alpha * l + jnp.sum(p, 0, keepdims=True)
        acc = alpha * acc + _dot_tn(v_blk, p.astype(BF16))
        return m_new, l, acc

    init = (jnp.full((1, R * tq), MASK_NEG, F32), jnp.zeros((1, R * tq), F32), jnp.zeros((2 * dh, R * tq), F32))
    causal_bias = bc_ref[...]

    def sel_pair(kb, state):
        k0 = pl.multiple_of(kb * 2 * tq, 2 * tq)
        return attend(state, ks_ref[0, pl.ds(k0, 2 * tq), :], vs_ref[0, pl.ds(k0, 2 * tq), :], q_sel, None)

    st_sel = lax.fori_loop(0, i // 2, sel_pair, init)
    prev_open = jnp.where(i % 2 == 1, 0.0, MASK_NEG)
    prev = jnp.maximum(i - 1, 0)
    k_fin = jnp.concatenate([key_block(ks_ref, prev), key_block(ks_ref, i)], axis=0)
    v_fin = jnp.concatenate([key_block(vs_ref, prev), key_block(vs_ref, i)], axis=0)
    m_sel, l_sel, acc_sel = attend(
        st_sel, k_fin, v_fin, q_sel,
        lambda s: jnp.concatenate([s[:tq] + prev_open, s[tq:] + causal_bias], axis=0))

    n_back = WIN // tq
    kv_win = jnp.concatenate([key_block(kw_ref, jnp.maximum(i - back, 0)) for back in range(n_back, -1, -1)],
                             axis=0)

    def window_bias(s):
        parts = []
        for n, back in enumerate(range(n_back, -1, -1)):
            blk = s[n * tq:(n + 1) * tq]
            if back == 0:
                parts.append(blk + causal_bias)
            else:
                exists = jnp.where(i >= back, 0.0, MASK_NEG)
                parts.append(blk + (bl_ref[...] + exists) if back == n_back else blk + exists)
        return jnp.concatenate(parts, axis=0)

    m_win, l_win, acc_win = attend(init, kv_win, kv_win, q_win, window_bias)

    gates = gt_ref[...]
    gate = [jnp.concatenate([gates[3 * r + c:3 * r + c + 1] for r in range(R)], axis=1) for c in range(3)]
    o = (gate[0] * o_cmp + (gate[1] * (1.0 / l_sel)) * acc_sel[:dh] + (gate[2] * (1.0 / l_win)) * acc_win[dh:])
    for r in range(R):
        o_ref[r * dh:(r + 1) * dh, :] = o[:, r * tq:(r + 1) * tq].astype(o_ref.dtype)


def _nsa_attention(q_t, gates_t, kc, vc, ks_aug, vs, kw, overlap_t, batch, seq, tq):
    G, R, dh = NSA_GROUPS, NSA_REP, NSA_HD
    T = batch * seq
    nq = seq // tq
    n_slc = seq // SLC_LEN
    n_c = kc.shape[1]
    key = lax.broadcasted_iota(jnp.int32, (tq, R * tq), 0)
    qry = lax.broadcasted_iota(jnp.int32, (tq, R * tq), 1) % tq
    causal_bias = jnp.where(key <= qry, 0.0, MASK_NEG).astype(F32)
    lower_bias = jnp.where(key > qry, 0.0, MASK_NEG).astype(F32)
    per_bg = lambda b, g, i: (g * batch + b, 0, 0)
    seq_spec = pl.BlockSpec((1, seq, LANES), lambda b, g, i: (g, b, 0))
    const = lambda b, g, i: (0, 0)
    return pl.pallas_call(
        functools.partial(_nsa_kernel, tq=tq, n_slc=n_slc, n_top=min(SLC_TOPK, n_slc)),
        grid=(batch, G, nq),
        in_specs=[
            pl.BlockSpec((R * dh, tq), lambda b, g, i: (g, b * nq + i)),
            pl.BlockSpec((GATE_ROWS, tq), lambda b, g, i: (g, b * nq + i)),
            pl.BlockSpec((1, n_c, dh), per_bg),
            pl.BlockSpec((1, n_c, dh), per_bg),
            seq_spec, seq_spec, seq_spec,
            pl.BlockSpec((dh, n_c), const),
            pl.BlockSpec((tq, R * tq), const),
            pl.BlockSpec((tq, R * tq), const),
        ],
        out_specs=pl.BlockSpec((R * dh, tq), lambda b, g, i: (g, b * nq + i)),
        out_shape=jax.ShapeDtypeStruct((G * R * dh, T), BF16),
        compiler_params=_params("parallel", "parallel", "arbitrary"),
    )(q_t, gates_t, kc, vc, ks_aug, vs, kw, overlap_t, causal_bias, lower_bias)


R_E1, R_E2, R_W1, R_W2, R_RANK1, R_RANK2 = range(6)
P_E1 = 6


def _router_kernel(x_ref, whi_ref, wlo_ref, route_ref, pieces_ref, before_ref, total_ref, cnt_ref):
    @pl.when(pl.program_id(0) == 0)
    def _():
        cnt_ref[...] = jnp.zeros_like(cnt_ref)

    x = x_ref[...]
    x_hi = x.astype(BF16)
    x_lo = (x - x_hi.astype(F32)).astype(BF16)
    logits = _dot(x_hi, whi_ref[...]) + _dot(x_hi, wlo_ref[...]) + _dot(x_lo, whi_ref[...])
    tc = logits.shape[0]
    lane = lax.broadcasted_iota(jnp.int32, logits.shape, 1).astype(F32)
    logits = jnp.where(lane < N_EXPERTS, logits, -jnp.inf)
    m1 = jnp.max(logits, -1, keepdims=True)
    i1 = jnp.min(jnp.where(logits == m1, lane, float(LANES)), -1, keepdims=True)
    rest = jnp.where(lane == i1, -jnp.inf, logits)
    m2 = jnp.max(rest, -1, keepdims=True)
    i2 = jnp.min(jnp.where(rest == m2, lane, float(LANES)), -1, keepdims=True)
    e2 = jnp.exp(m2 - m1)
    den = 1.0 + e2
    chosen = jnp.where(lane == i1, 1.0, jnp.where(lane == i2, 1.0, 0.0))
    earlier = (lax.broadcasted_iota(jnp.int32, (tc, tc), 1) < lax.broadcasted_iota(jnp.int32, (tc, tc), 0))
    before = cnt_ref[0:1, :]
    excl = _dot(jnp.where(earlier, 1.0, 0.0).astype(BF16), chosen.astype(BF16)) + before
    rank1 = jnp.sum(jnp.where(lane == i1, excl, 0.0), -1, keepdims=True)
    rank2 = jnp.sum(jnp.where(lane == i2, excl, 0.0), -1, keepdims=True)
    cols = {R_E1: i1, R_E2: i2, R_W1: 1.0 / den, R_W2: e2 / den, R_RANK1: rank1, R_RANK2: rank2}
    route = jnp.zeros_like(logits)
    for k, v in cols.items():
        route = jnp.where(lane == float(k), v, route)
    route_ref[...] = route
    pieces = jnp.where(lane == float(P_E1), i1, 0.0)
    for slot, weight in enumerate((1.0 / den, e2 / den)):
        rem = weight
        for k in range(3):
            piece = rem.astype(BF16).astype(F32)
            pieces = jnp.where(lane == float(3 * slot + k), piece, pieces)
            rem = rem - piece
    pieces_ref[...] = pieces.astype(BF16)
    before_ref[0] = jnp.broadcast_to(before, before_ref.shape[1:])
    after = before + jnp.sum(chosen, axis=0, keepdims=True)
    cnt_ref[...] = jnp.broadcast_to(after, cnt_ref.shape)
    total_ref[...] = jnp.broadcast_to(after, total_ref.shape)


def _router(x, w_hi, w_lo, tc):
    T, D = x.shape
    return pl.pallas_call(
        _router_kernel,
        grid=(T // tc,),
        in_specs=[
            pl.BlockSpec((tc, D), lambda i: (i, 0)),
            pl.BlockSpec((D, LANES), lambda i: (0, 0)),
            pl.BlockSpec((D, LANES), lambda i: (0, 0)),
        ],
        out_specs=[pl.BlockSpec((tc, LANES), lambda i: (i, 0)),
                   pl.BlockSpec((tc, LANES), lambda i: (i, 0)),
                   pl.BlockSpec((1, 8, LANES), lambda i: (i, 0, 0)),
                   pl.BlockSpec((8, LANES), lambda i: (0, 0))],
        out_shape=[jax.ShapeDtypeStruct((T, LANES), F32),
                   jax.ShapeDtypeStruct((T, LANES), BF16),
                   jax.ShapeDtypeStruct((T // tc, 8, LANES), F32),
                   jax.ShapeDtypeStruct((8, LANES), F32)],
        scratch_shapes=[pltpu.VMEM((8, LANES), F32)],
        compiler_params=_params("arbitrary"),
    )(x, w_hi, w_lo)


def _moe_plan(before, total, n_tok, tc, ts, tm):
    E = N_EXPERTS
    n_c = n_tok // tc
    n_rows = 2 * n_tok + E * tm
    n_sub, n_tiles = n_rows // ts, n_rows // tm
    n_work = E * n_c + 2 * n_sub
    i32 = jnp.int32
    tot = total[0, :E].astype(i32)
    cum = jnp.concatenate([before[:, 0, :E].astype(i32).T, tot[:, None]], axis=1)
    tiles_e = (tot + tm - 1) // tm
    tile_end = jnp.cumsum(tiles_e)
    row_off = (tile_end - tiles_e) * tm
    tile_id = jnp.arange(n_tiles, dtype=i32)
    tile_expert = jnp.minimum(jnp.sum(tile_id[:, None] >= tile_end[None, :], axis=1), E - 1).astype(i32)
    tile_valid = (tile_id < tile_end[-1]).astype(i32)
    sub = jnp.arange(n_sub, dtype=i32)
    sub_e = tile_expert[(sub * ts) // tm]
    rel0 = sub * ts - row_off[sub_e]
    rel1 = rel0 + ts
    cum_s = cum[sub_e]
    c_lo = jnp.sum(cum_s[:, 1:] <= rel0[:, None], axis=1).astype(i32)
    c_hi = jnp.sum(cum_s[:, :-1] < rel1[:, None], axis=1).astype(i32) - 1
    c_lo = jnp.clip(c_lo, 0, n_c - 1)
    c_hi = jnp.clip(c_hi, c_lo, n_c - 1)
    n_j = c_hi - c_lo + 1
    ends = jnp.cumsum(n_j)
    starts = ends - n_j
    n_valid = ends[-1]
    w = jnp.arange(n_work, dtype=i32)
    valid = w < n_valid
    jw = jnp.minimum(jnp.sum(w[:, None] >= ends[None, :], axis=1), n_sub - 1).astype(i32)
    cw = jnp.where(valid, c_lo[jw] + (w - starts[jw]), c_hi[n_sub - 1]).astype(i32)
    first = valid & (w == starts[jw])
    last = valid & (w == ends[jw] - 1)
    disp = (jw, cw, valid.astype(i32), first.astype(i32), last.astype(i32), sub_e[jw].astype(i32))
    order = jnp.argsort(jnp.where(valid, cw * n_sub + jw, n_c * n_sub + w))
    cj, cc, cv = jw[order], cw[order], valid[order]
    last_j, last_c = cj[n_valid - 1], cc[n_valid - 1]
    cj = jnp.where(cv, cj, last_j)
    cc = jnp.where(cv, cc, last_c)
    prev_c = jnp.concatenate([jnp.full((1,), -1, i32), cc[:-1]])
    next_c = jnp.concatenate([cc[1:], jnp.full((1,), -1, i32)])
    next_v = jnp.concatenate([cv[1:], jnp.zeros((1,), bool)])
    cfirst = cv & (cc != prev_c)
    clast = cv & ((cc != next_c) | ~next_v)
    comb = (cj.astype(i32), cc.astype(i32), cv.astype(i32), cfirst.astype(i32), clast.astype(i32))
    return row_off, tile_expert, tile_valid, disp, comb, n_rows, n_work


def _dispatch_kernel(jw, cw, vw, fw, lw, ew, x_ref, pc_ref, pos_ref, xs_ref, gs_ref, acc_ref, gacc_ref):
    w = pl.program_id(0)

    @pl.when(vw[w] == 1)
    def _():
        ts = acc_ref.shape[0]
        row_id = jw[w] * ts + lax.broadcasted_iota(jnp.int32, (ts, 1), 0)
        pos = pos_ref[0]
        hit = jnp.where(pos[0:1, :] == row_id, 1.0,
                        jnp.where(pos[1:2, :] == row_id, 1.0, 0.0)).astype(BF16)
        rows = _dot(hit, x_ref[...])
        side = _dot(hit, pc_ref[...])

        @pl.when(fw[w] == 1)
        def _():
            acc_ref[...] = rows
            gacc_ref[...] = side

        @pl.when(fw[w] == 0)
        def _():
            acc_ref[...] += rows
            gacc_ref[...] += side

        @pl.when(lw[w] == 1)
        def _():
            xs_ref[...] = acc_ref[...].astype(xs_ref.dtype)
            side_all = gacc_ref[...]
            lane = lax.broadcasted_iota(jnp.int32, side_all.shape, 1)
            lane_sum = lambda keep: jnp.sum(jnp.where(keep, side_all, 0.0), -1, keepdims=True)
            weight1 = lane_sum(lane < 3)
            weight2 = lane_sum((lane >= 3) & (lane < 6))
            expert1 = lane_sum(lane == P_E1)
            gate = jnp.where(expert1 == ew[w].astype(F32), weight1, weight2)
            gs_ref[...] = jnp.broadcast_to(gate, gs_ref.shape)


def _dispatch(xb, pieces, pos_rows, disp, n_rows, n_work, tc, ts):
    T, D = xb.shape
    by_chunk = lambda w, jw, cw, vw, fw, lw, ew: (cw[w], 0)
    by_sub = lambda w, jw, cw, vw, fw, lw, ew: (jw[w], 0)
    grid_spec = pltpu.PrefetchScalarGridSpec(
        num_scalar_prefetch=6,
        grid=(n_work,),
        in_specs=[
            pl.BlockSpec((tc, D), by_chunk),
            pl.BlockSpec((tc, LANES), by_chunk),
            pl.BlockSpec((1, 8, tc), lambda w, jw, cw, vw, fw, lw, ew: (cw[w], 0, 0)),
        ],
        out_specs=[pl.BlockSpec((ts, D), by_sub), pl.BlockSpec((ts, LANES), by_sub)],
        scratch_shapes=[pltpu.VMEM((ts, D), F32), pltpu.VMEM((ts, LANES), F32)],
    )
    return pl.pallas_call(
        _dispatch_kernel,
        grid_spec=grid_spec,
        out_shape=[jax.ShapeDtypeStruct((n_rows, D), BF16), jax.ShapeDtypeStruct((n_rows, LANES), F32)],
        compiler_params=_params("arbitrary"),
    )(*disp, xb, pieces, pos_rows)


def _moe_kernel(te, tv, xs_ref, gs_ref, wa_ref, wb_ref, wo_ref, y_ref, acc_ref):
    i = pl.program_id(0)
    f = pl.program_id(1)
    last_f = pl.num_programs(1) - 1

    @pl.when(tv[i] == 1)
    def _():
        xs = xs_ref[...]
        h = (_silu(_dot(xs, wa_ref[0])) * _dot(xs, wb_ref[0]) * gs_ref[:, 0:1]).astype(BF16)
        contrib = _dot(h, wo_ref[0])

        @pl.when(f == 0)
        def _():
            acc_ref[...] = contrib

        @pl.when(f > 0)
        def _():
            acc_ref[...] += contrib

        @pl.when(f == last_f)
        def _():
            y_ref[...] = acc_ref[...].astype(y_ref.dtype)

    @pl.when((tv[i] == 0) & (f == last_f))
    def _():
        y_ref[...] = jnp.zeros_like(y_ref)


def _moe(xs, gs, tile_expert, tile_valid, w_in, w_out, tm, tf):
    n_rows, D = xs.shape
    E, F, _ = w_out.shape
    nf = F // tf
    f_of = lambda i, f, te, tv: jnp.where(tv[i] == 1, f, nf - 1)
    grid_spec = pltpu.PrefetchScalarGridSpec(
        num_scalar_prefetch=2,
        grid=(n_rows // tm, nf),
        in_specs=[
            pl.BlockSpec((tm, D), lambda i, f, te, tv: (i, 0)),
            pl.BlockSpec((tm, LANES), lambda i, f, te, tv: (i, 0)),
            pl.BlockSpec((1, D, tf), lambda i, f, te, tv: (te[i], 0, f_of(i, f, te, tv))),
            pl.BlockSpec((1, D, tf), lambda i, f, te, tv: (te[i], 0, nf + f_of(i, f, te, tv))),
            pl.BlockSpec((1, tf, D), lambda i, f, te, tv: (te[i], f_of(i, f, te, tv), 0)),
        ],
        out_specs=pl.BlockSpec((tm, D), lambda i, f, te, tv: (i, 0)),
        scratch_shapes=[pltpu.VMEM((tm, D), F32)],
    )
    return pl.pallas_call(
        _moe_kernel,
        grid_spec=grid_spec,
        out_shape=jax.ShapeDtypeStruct((n_rows, D), BF16),
        compiler_params=_params("parallel", "arbitrary"),
    )(tile_expert, tile_valid, xs, gs, w_in, w_in, w_out)


def _combine_kernel(cj, cc, cv, cf, cl, y_ref, pos_ref, x_ref, g_ref, b_ref, o_ref, acc_ref):
    w = pl.program_id(0)

    @pl.when(cv[w] == 1)
    def _():
        ts = y_ref.shape[0]
        col_id = cj[w] * ts + lax.broadcasted_iota(jnp.int32, (1, ts), 1)
        pos = pos_ref[...]
        hit = (jnp.where(pos[:, 0:1] == col_id, 1.0, 0.0).astype(BF16)
               + jnp.where(pos[:, 1:2] == col_id, 1.0, 0.0).astype(BF16))
        contrib = _dot(hit, y_ref[...])

        @pl.when(cf[w] == 1)
        def _():
            acc_ref[...] = contrib

        @pl.when(cf[w] == 0)
        def _():
            acc_ref[...] += contrib

        @pl.when(cl[w] == 1)
        def _():
            o_ref[...] = _layer_norm(DN_ALPHA * x_ref[...] + acc_ref[...], g_ref[...], b_ref[...])


def _combine(y, pos_cols, x, g, b, comb, n_work, tc, ts):
    T, D = x.shape
    grid_spec = pltpu.PrefetchScalarGridSpec(
        num_scalar_prefetch=5,
        grid=(n_work,),
        in_specs=[
            pl.BlockSpec((ts, D), lambda w, cj, cc, cv, cf, cl: (cj[w], 0)),
            pl.BlockSpec((tc, LANES), lambda w, cj, cc, cv, cf, cl: (cc[w], 0)),
            pl.BlockSpec((tc, D), lambda w, cj, cc, cv, cf, cl: (cc[w], 0)),
            pl.BlockSpec((1, D), lambda w, cj, cc, cv, cf, cl: (0, 0)),
            pl.BlockSpec((1, D), lambda w, cj, cc, cv, cf, cl: (0, 0)),
        ],
        out_specs=pl.BlockSpec((tc, D), lambda w, cj, cc, cv, cf, cl: (cc[w], 0)),
        scratch_shapes=[pltpu.VMEM((tc, D), F32)],
    )
    return pl.pallas_call(
        _combine_kernel,
        grid_spec=grid_spec,
        out_shape=jax.ShapeDtypeStruct((T, D), F32),
        compiler_params=_params("arbitrary"),
    )(*comb, y, pos_cols, x, g.reshape(1, D), b.reshape(1, D))


def _moe_layer(x, xb, w_router, w_in, w_out, g, b, tc, tm, tf):
    T, D = x.shape
    ts = tc
    wr = jnp.pad(w_router, ((0, 0), (0, LANES - N_EXPERTS)))
    wr_hi = wr.astype(BF16)
    wr_lo = (wr - wr_hi.astype(F32)).astype(BF16)
    route, pieces, before, total = _router(x, wr_hi, wr_lo, tc)
    row_off, tile_expert, tile_valid, disp, comb, n_rows, n_work = _moe_plan(before, total, T, tc, ts, tm)
    e1 = route[:, R_E1].astype(jnp.int32)
    e2 = route[:, R_E2].astype(jnp.int32)
    pos1 = row_off[e1] + route[:, R_RANK1].astype(jnp.int32)
    pos2 = row_off[e2] + route[:, R_RANK2].astype(jnp.int32)
    pos = jnp.stack([pos1, pos2])
    pos_rows = jnp.pad(pos.reshape(2, T // tc, tc).transpose(1, 0, 2), ((0, 0), (0, 6), (0, 0)),
                       constant_values=-1)
    pos_cols = jnp.pad(pos.T, ((0, 0), (0, LANES - 2)), constant_values=-1)
    xs, gs = _dispatch(xb, pieces, pos_rows, disp, n_rows, n_work, tc, ts)
    y = _moe(xs, gs, tile_expert, tile_valid, w_in, w_out, tm, tf)
    return _combine(y, pos_cols, x, g, b, comb, n_work, tc, ts)


def kernel(x, ret_w_in, ret_gn_g, ret_w_out, nsa_w_kv, cmp_k_pe, cmp_k_w1, cmp_k_w2, cmp_v_pe, cmp_v_w1, cmp_v_w2,
           nsa_w_q, nsa_w_out, ffn_w_in, ffn_w_out, moe_router, moe_w_in, moe_w_out, ln_g, ln_b):
    B, S, D = x.shape
    T = B * S
    G, R, dh = NSA_GROUPS, NSA_REP, NSA_HD
    assert ret_w_in.shape[0] == 1 and nsa_w_q.shape[0] == 1 and ln_g.shape[0] == DEPTH
    dk = D // RET_HEADS
    dv = 2 * dk
    n_slc = S // SLC_LEN
    tq = 256
    assert n_slc <= dh and S % tq == 0 and WIN % tq == 0 and R * 3 <= GATE_ROWS
    tm = min(1024, S)

    xf = x.reshape(T, D)
    xb = xf.astype(BF16)

    cos_r, sin_r = _rope_tables(S, dk, dk)
    qkvg = _ret_proj(xb, ret_w_in[0].astype(BF16), cos_r[:, :dk // 2], -sin_r[:, :dk // 2], S, dk, tm)
    y = _retention(qkvg, ret_gn_g[0], B, S, dk, dv, rows=min(512, S))
    x1, x1b = _proj_ln(y, ret_w_out[0].astype(BF16), xf, ln_g[0, 0], ln_b[0, 0], tm=512)
    x2, x2b = _ffn(x1b, ffn_w_in[0].astype(BF16), ffn_w_out[0].astype(BF16), x1, ln_g[0, 1], ln_b[0, 1],
                   tm=512, tf=1408)

    cos_n, sin_n = _rope_tables(S, dh, LANES)
    ks_aug, vs, kw, cmp_kv = _kv_proj(x2b, nsa_w_kv.astype(BF16), cos_n, sin_n, S, tm=512)
    n_c = S // CMP_STRIDE
    cmp_in = cmp_kv.reshape(2, G * B * n_c, CMP_STRIDE * dh)
    w1 = jnp.stack([cmp_k_w1, cmp_v_w1]).astype(BF16)
    w2 = jnp.stack([cmp_k_w2, cmp_v_w2]).astype(BF16)
    pe = jnp.stack([cmp_k_pe, cmp_v_pe]).reshape(2, 1, CMP_LEN * dh)
    pe = jnp.broadcast_to(pe, (2, 8, CMP_LEN * dh)).astype(BF16)
    kvc = _compress(cmp_in, w1, w2, pe, tm=min(1024, G * B * n_c)).reshape(2, G * B, n_c, dh)
    cs = jnp.arange(n_c) * CMP_STRIDE
    ss = jnp.arange(dh) * SLC_LEN
    overlap_t = jnp.clip(jnp.minimum(cs[None, :] + CMP_LEN, ss[:, None] + SLC_LEN)
                         - jnp.maximum(cs[None, :], ss[:, None]), 0).astype(F32) / CMP_STRIDE
    overlap_t = jnp.where(jnp.arange(dh)[:, None] < n_slc, overlap_t, 0.0).astype(BF16)

    nq_cols = NSA_HEADS * dh
    wq = nsa_w_q[0]
    wq_t = wq[:, :nq_cols].T.astype(BF16)
    wg_t = jnp.pad(wq[:, nq_cols:].T.reshape(G, R * 3, D), ((0, 0), (0, GATE_ROWS - R * 3), (0, 0)))
    wg_t = wg_t.reshape(G * GATE_ROWS, D).astype(BF16)
    ang = _rope_angles(S, dh).T
    q_t, gates_t = _q_proj(x2b, wq_t, wg_t, jnp.cos(ang), jnp.sin(ang), S, tm=512)
    attn_t = _nsa_attention(q_t, gates_t, kvc[0], kvc[1], ks_aug, vs, kw, overlap_t, B, S, tq)
    x3, x3b = _proj_ln(attn_t, nsa_w_out[0].astype(BF16), x2, ln_g[1, 0], ln_b[1, 0], tm=512, y_transposed=True)

    out = _moe_layer(x3, x3b, moe_router[0], moe_w_in[0].astype(BF16), moe_w_out[0].astype(BF16),
                     ln_g[1, 1], ln_b[1, 1], tc=512, tm=1024, tf=896)
    return out.reshape(B, S, D)
```

```python
import functools

import jax
import jax.numpy as jnp
from jax import lax
from jax.experimental import pallas as pl
from jax.experimental.pallas import tpu as pltpu

F32 = jnp.float32
BF16 = jnp.bfloat16

DEPTH = 2
ROPE_THETA = 10000.0
LN_EPS = 1e-5
DN_ALPHA = (2.0 * DEPTH) ** 0.25

RET_HEADS = 4
RET_CHUNK = 128

NSA_HEADS = 16
NSA_GROUPS = 4
NSA_REP = NSA_HEADS // NSA_GROUPS
NSA_HD = 64
CMP_LEN = 32
CMP_STRIDE = 16
SLC_LEN = 64
SLC_TOPK = 8
WIN = 512
FORCE_SCORE = 1e9

N_EXPERTS = 8

LANES = 128
GATE_ROWS = 16
MASK_NEG = -1e30
LOG2_E = 1.4426950408889634
VMEM_LIMIT = 56 * 1024 * 1024


def _params(*sem):
    return pltpu.CompilerParams(dimension_semantics=sem, vmem_limit_bytes=VMEM_LIMIT)


def _dot(a, b):
    return jnp.dot(a, b, preferred_element_type=F32)


def _dot_nt(a, b):
    return lax.dot_general(a, b, (((1,), (1,)), ((), ())), preferred_element_type=F32)


def _dot_tn(a, b):
    return lax.dot_general(a, b, (((0,), (0,)), ((), ())), preferred_element_type=F32)


def _layer_norm(z, g, b):
    mu = jnp.mean(z, -1, keepdims=True)
    zc = z - mu
    var = jnp.mean(zc * zc, -1, keepdims=True)
    return zc * lax.rsqrt(var + LN_EPS) * g + b


def _silu(a):
    return a * jax.nn.sigmoid(a)


def _rope_angles(seq, dim):
    inv = 1.0 / (ROPE_THETA ** (jnp.arange(0, dim, 2, dtype=F32) / dim))
    return jnp.arange(seq, dtype=F32)[:, None] * inv[None, :]


def _rope_tables(seq, dim, width):
    ang = _rope_angles(seq, dim)
    cos, sin = jnp.cos(ang), jnp.sin(ang)
    cos_h = jnp.concatenate([cos, cos], -1)
    sin_h = jnp.concatenate([-sin, sin], -1)
    rep = width // dim
    return jnp.tile(cos_h, (1, rep)), jnp.tile(sin_h, (1, rep))


def _ret_proj_kernel(x_ref, w_ref, cos_ref, sin_ref, o_ref, *, dk, n_rot):
    j = pl.program_id(1)
    acc = _dot(x_ref[...], w_ref[...])
    tn = acc.shape[1]
    half = dk // 2

    @pl.when(j < n_rot)
    def _():
        scale = jnp.where(j == n_rot - 1, dk ** -0.5, 1.0).astype(F32)
        cos = cos_ref[...] * scale
        sin = sin_ref[...] * scale
        for h in range(tn // dk):
            a = acc[:, h * dk:h * dk + half]
            b = acc[:, h * dk + half:(h + 1) * dk]
            o_ref[:, h * dk:h * dk + half] = (a * cos - b * sin).astype(o_ref.dtype)
            o_ref[:, h * dk + half:(h + 1) * dk] = (a * sin + b * cos).astype(o_ref.dtype)

    @pl.when(j >= n_rot)
    def _():
        o_ref[...] = acc.astype(o_ref.dtype)


def _ret_proj(xb, w, cos, sin, seq, dk, tm):
    T, D = xb.shape
    N = w.shape[1]
    tn = RET_HEADS * dk
    half = dk // 2
    n_pos = seq // tm
    return pl.pallas_call(
        functools.partial(_ret_proj_kernel, dk=dk, n_rot=2),
        grid=(T // tm, N // tn),
        in_specs=[
            pl.BlockSpec((tm, D), lambda i, j: (i, 0)),
            pl.BlockSpec((D, tn), lambda i, j: (0, j)),
            pl.BlockSpec((tm, half), lambda i, j: (i % n_pos, 0)),
            pl.BlockSpec((tm, half), lambda i, j: (i % n_pos, 0)),
        ],
        out_specs=pl.BlockSpec((tm, tn), lambda i, j: (i, j)),
        out_shape=jax.ShapeDtypeStruct((T, N), BF16),
        compiler_params=_params("parallel", "arbitrary"),
    )(xb, w, cos, sin)


def _retention_kernel(q_ref, k_ref, v_ref, g_ref, dec_ref, qd_ref, kd_ref, gn_ref, o_ref, state_ref,
                      *, chunk, n_chunks):
    @pl.when(pl.program_id(2) == 0)
    def _():
        state_ref[...] = jnp.zeros_like(state_ref)

    decay = dec_ref[0]
    qd = qd_ref[0]
    kd = kd_ref[0]
    cd = qd[chunk - 1:chunk, :]
    gn = gn_ref[0]
    for c in range(n_chunks):
        rows = slice(c * chunk, (c + 1) * chunk)
        q = q_ref[rows, :]
        k = k_ref[rows, :]
        v = v_ref[rows, :]
        scores = _dot_nt(q, k) * decay
        inner = _dot(scores.astype(BF16), v)
        state = state_ref[...]
        cross = _dot(q, state.astype(BF16)) * qd
        out = inner + cross
        k_dec = (k.astype(F32) * kd).astype(BF16)
        state_ref[...] = state * cd + _dot_tn(k_dec, v)
        mu = jnp.mean(out, -1, keepdims=True)
        oc = out - mu
        var = jnp.mean(oc * oc, -1, keepdims=True)
        normed = oc * lax.rsqrt(var + LN_EPS) * gn
        o_ref[rows, :] = (_silu(g_ref[rows, :].astype(F32)) * normed).astype(o_ref.dtype)


def _retention(qkvg, gn_g, batch, seq, dk, dv, rows):
    H, C = RET_HEADS, RET_CHUNK
    T = batch * seq
    n_steps = seq // rows
    log_gamma = jnp.log1p(-(2.0 ** (-5.0 - jnp.arange(H, dtype=F32))))
    i = jnp.arange(C, dtype=F32)
    rel = i[:, None] - i[None, :]
    intra = jnp.where(rel >= 0, jnp.exp(log_gamma[:, None, None] * jnp.maximum(rel, 0.0)), 0.0)
    q_decay = jnp.exp(log_gamma[:, None] * (i + 1.0))[:, :, None]
    k_decay = jnp.exp(log_gamma[:, None] * (C - 1.0 - i))[:, :, None]
    v_off = (2 * H * dk) // dv
    g_off = (2 * H * dk + H * dv) // dv
    row = lambda b, h, n: b * n_steps + n
    return pl.pallas_call(
        functools.partial(_retention_kernel, chunk=C, n_chunks=rows // C),
        grid=(batch, H, n_steps),
        in_specs=[
            pl.BlockSpec((rows, dk), lambda b, h, n: (row(b, h, n), h)),
            pl.BlockSpec((rows, dk), lambda b, h, n: (row(b, h, n), H + h)),
            pl.BlockSpec((rows, dv), lambda b, h, n: (row(b, h, n), v_off + h)),
            pl.BlockSpec((rows, dv), lambda b, h, n: (row(b, h, n), g_off + h)),
            pl.BlockSpec((1, C, C), lambda b, h, n: (h, 0, 0)),
            pl.BlockSpec((1, C, 1), lambda b, h, n: (h, 0, 0)),
            pl.BlockSpec((1, C, 1), lambda b, h, n: (h, 0, 0)),
            pl.BlockSpec((1, 1, dv), lambda b, h, n: (h, 0, 0)),
        ],
        out_specs=pl.BlockSpec((rows, dv), lambda b, h, n: (row(b, h, n), h)),
        out_shape=jax.ShapeDtypeStruct((T, H * dv), BF16),
        scratch_shapes=[pltpu.VMEM((dk, dv), F32)],
        compiler_params=_params("parallel", "parallel", "arbitrary"),
    )(qkvg, qkvg, qkvg, qkvg, intra, q_decay, k_decay, gn_g.reshape(H, 1, dv))


def _proj_ln_kernel(y_ref, w_ref, x_ref, g_ref, b_ref, o_ref, ob_ref, *, y_transposed):
    mix = _dot_tn(y_ref[...], w_ref[...]) if y_transposed else _dot(y_ref[...], w_ref[...])
    o = _layer_norm(DN_ALPHA * x_ref[...] + mix, g_ref[...], b_ref[...])
    o_ref[...] = o
    ob_ref[...] = o.astype(BF16)


def _proj_ln(y, w, x, g, b, tm, y_transposed=False):
    K, D = w.shape
    T = x.shape[0]
    y_spec = pl.BlockSpec((K, tm), lambda i: (0, i)) if y_transposed else pl.BlockSpec((tm, K), lambda i: (i, 0))
    return pl.pallas_call(
        functools.partial(_proj_ln_kernel, y_transposed=y_transposed),
        grid=(T // tm,),
        in_specs=[
            y_spec,
            pl.BlockSpec((K, D), lambda i: (0, 0)),
            pl.BlockSpec((tm, D), lambda i: (i, 0)),
            pl.BlockSpec((1, D), lambda i: (0, 0)),
            pl.BlockSpec((1, D), lambda i: (0, 0)),
        ],
        out_specs=[pl.BlockSpec((tm, D), lambda i: (i, 0)), pl.BlockSpec((tm, D), lambda i: (i, 0))],
        out_shape=[jax.ShapeDtypeStruct((T, D), F32), jax.ShapeDtypeStruct((T, D), BF16)],
        compiler_params=_params("parallel"),
    )(y, w, x, g.reshape(1, D), b.reshape(1, D))


def _ffn_kernel(xb_ref, wi_ref, wo_ref, x_ref, g_ref, b_ref, o_ref, ob_ref, *, tf):
    xb = xb_ref[...]
    F = wo_ref.shape[0]
    acc = None
    for f in range(F // tf):
        cols = slice(f * tf, (f + 1) * tf)
        gate_cols = slice(F + f * tf, F + (f + 1) * tf)
        h = (_silu(_dot(xb, wi_ref[:, cols])) * _dot(xb, wi_ref[:, gate_cols])).astype(BF16)
        contrib = _dot(h, wo_ref[cols, :])
        acc = contrib if acc is None else acc + contrib
    o = _layer_norm(DN_ALPHA * x_ref[...] + acc, g_ref[...], b_ref[...])
    o_ref[...] = o
    ob_ref[...] = o.astype(BF16)


def _ffn(xb, w_in, w_out, x, g, b, tm, tf):
    T, D = xb.shape
    F = w_out.shape[0]
    assert F % tf == 0
    return pl.pallas_call(
        functools.partial(_ffn_kernel, tf=tf),
        grid=(T // tm,),
        in_specs=[
            pl.BlockSpec((tm, D), lambda i: (i, 0)),
            pl.BlockSpec((D, 2 * F), lambda i: (0, 0)),
            pl.BlockSpec((F, D), lambda i: (0, 0)),
            pl.BlockSpec((tm, D), lambda i: (i, 0)),
            pl.BlockSpec((1, D), lambda i: (0, 0)),
            pl.BlockSpec((1, D), lambda i: (0, 0)),
        ],
        out_specs=[pl.BlockSpec((tm, D), lambda i: (i, 0)), pl.BlockSpec((tm, D), lambda i: (i, 0))],
        out_shape=[jax.ShapeDtypeStruct((T, D), F32), jax.ShapeDtypeStruct((T, D), BF16)],
        compiler_params=_params("parallel"),
    )(xb, w_in, w_out, x, g.reshape(1, D), b.reshape(1, D))


def _rope64(acc, cos, sin_signed):
    half = NSA_HD // 2
    lane = lax.broadcasted_iota(jnp.int32, acc.shape, 1)
    first = (lane % NSA_HD) < half
    rot = jnp.where(first, pltpu.roll(acc, LANES - half, 1), pltpu.roll(acc, half, 1))
    return acc * cos + rot * sin_signed


def _kv_proj_kernel(x_ref, w_ref, cos_ref, sin_ref, ks_ref, vs_ref, kw_ref, vw_ref, cmp_ref, *, n_pos):
    G, dh = NSA_GROUPS, NSA_HD
    acc = _dot(x_ref[...], w_ref[...])
    tm = acc.shape[0]
    cos, sin = cos_ref[...], sin_ref[...]
    lane = lax.broadcasted_iota(jnp.int32, (tm, LANES), 1)
    low = lane < dh
    pos = (pl.program_id(0) % n_pos) * tm + lax.broadcasted_iota(jnp.int32, (tm, LANES), 0)
    block_onehot = jnp.where(lane - dh == pos // SLC_LEN, 1.0, 0.0)
    ones_lane = jnp.where(lane == dh, 1.0, 0.0)
    swap = lambda a: pltpu.roll(a, dh, 1)

    def slab(branch, s):
        off = branch * G * dh + s * LANES
        return acc[:, off:off + LANES]

    for s in range(G * dh // LANES):
        k_cmp, v_cmp = _rope64(slab(0, s), cos, sin), slab(1, s)
        k_slc, v_slc = _rope64(slab(2, s), cos, sin), slab(3, s)
        k_win, v_win = _rope64(slab(4, s), cos, sin), slab(5, s)
        for h in range(2):
            g = 2 * s + h
            to_low = (lambda a: a) if h == 0 else swap
            ks_ref[g] = jnp.where(low, to_low(k_slc), block_onehot).astype(BF16)
            vs_ref[g] = jnp.where(low, to_low(v_slc), ones_lane).astype(BF16)
            kw_ref[g] = jnp.where(low, to_low(k_win), 0.0).astype(BF16)
            vw_ref[g] = jnp.where(low, to_low(v_win), ones_lane).astype(BF16)
            cmp_ref[0, g] = to_low(k_cmp)[:, :dh].astype(BF16)
            cmp_ref[1, g] = to_low(v_cmp)[:, :dh].astype(BF16)


def _kv_proj(xb, w, cos, sin, seq, tm):
    T, D = xb.shape
    N = w.shape[1]
    G, dh = NSA_GROUPS, NSA_HD
    n_pos = seq // tm
    wide = jax.ShapeDtypeStruct((G, T, LANES), BF16)
    wide_spec = pl.BlockSpec((G, tm, LANES), lambda i: (0, i, 0))
    return pl.pallas_call(
        functools.partial(_kv_proj_kernel, n_pos=n_pos),
        grid=(T // tm,),
        in_specs=[
            pl.BlockSpec((tm, D), lambda i: (i, 0)),
            pl.BlockSpec((D, N), lambda i: (0, 0)),
            pl.BlockSpec((tm, LANES), lambda i: (i % n_pos, 0)),
            pl.BlockSpec((tm, LANES), lambda i: (i % n_pos, 0)),
        ],
        out_specs=[wide_spec, wide_spec, wide_spec, wide_spec, pl.BlockSpec((2, G, tm, dh), lambda i: (0, 0, i, 0))],
        out_shape=[wide, wide, wide, wide, jax.ShapeDtypeStruct((2, G, T, dh), BF16)],
        compiler_params=_params("parallel"),
    )(xb, w, cos, sin)


def _q_proj_kernel(x_ref, wq_ref, wg_ref, cos_ref, sin_ref, q_ref, gate_ref):
    dh = NSA_HD
    half = dh // 2
    x = x_ref[...]
    acc = _dot_nt(wq_ref[...], x)
    cos = cos_ref[...] * (dh ** -0.5 * LOG2_E)
    sin = sin_ref[...] * (dh ** -0.5 * LOG2_E)
    for h in range(acc.shape[0] // dh):
        a = acc[h * dh:h * dh + half]
        b = acc[h * dh + half:(h + 1) * dh]
        q_ref[h * dh:h * dh + half, :] = (a * cos - b * sin).astype(q_ref.dtype)
        q_ref[h * dh + half:(h + 1) * dh, :] = (a * sin + b * cos).astype(q_ref.dtype)
    gate_ref[...] = jax.nn.sigmoid(_dot_nt(wg_ref[...], x))


def _q_proj(xb, wq_t, wg_t, cos_t, sin_t, seq, tm):
    T, D = xb.shape
    nq, ng = wq_t.shape[0], wg_t.shape[0]
    n_pos = seq // tm
    half = NSA_HD // 2
    return pl.pallas_call(
        _q_proj_kernel,
        grid=(T // tm,),
        in_specs=[
            pl.BlockSpec((tm, D), lambda i: (i, 0)),
            pl.BlockSpec((nq, D), lambda i: (0, 0)),
            pl.BlockSpec((ng, D), lambda i: (0, 0)),
            pl.BlockSpec((half, tm), lambda i: (0, i % n_pos)),
            pl.BlockSpec((half, tm), lambda i: (0, i % n_pos)),
        ],
        out_specs=[pl.BlockSpec((nq, tm), lambda i: (0, i)), pl.BlockSpec((ng, tm), lambda i: (0, i))],
        out_shape=[jax.ShapeDtypeStruct((nq, T), BF16), jax.ShapeDtypeStruct((ng, T), F32)],
        compiler_params=_params("parallel"),
    )(xb, wq_t, wg_t, cos_t, sin_t)


def _compress_kernel(a_ref, w1_ref, w2_ref, pe_ref, o_ref):
    a = a_ref[0]
    w1 = w1_ref[0]
    half = w1.shape[0] // 2
    first = _dot(a, w1[:half])
    second = _dot(a, w1[half:])
    bias = _dot(pe_ref[0], w1)[0:1]
    m = a.shape[0]
    hid = first + pltpu.roll(second, m - 1, 0) + bias
    o_ref[0] = _dot(jax.nn.gelu(hid).astype(BF16), w2_ref[0]).astype(o_ref.dtype)


def _compress(a, w1, w2, pe, tm):
    _, rows, width = a.shape
    hid = w1.shape[2]
    dh = w2.shape[2]
    return pl.pallas_call(
        _compress_kernel,
        grid=(2, rows // tm),
        in_specs=[
            pl.BlockSpec((1, tm, width), lambda s, i: (s, i, 0)),
            pl.BlockSpec((1, 2 * width, hid), lambda s, i: (s, 0, 0)),
            pl.BlockSpec((1, hid, dh), lambda s, i: (s, 0, 0)),
            pl.BlockSpec((1, 8, 2 * width), lambda s, i: (s, 0, 0)),
        ],
        out_specs=pl.BlockSpec((1, tm, dh), lambda s, i: (s, i, 0)),
        out_shape=jax.ShapeDtypeStruct((2, rows, dh), BF16),
        compiler_params=_params("parallel", "parallel"),
    )(a, w1, w2, pe)


def _nsa_kernel(q_ref, gt_ref, kc_ref, vc_ref, ks_ref, vs_ref, kw_ref, vw_ref, ov_ref, bc_ref, bl_ref, o_ref,
                *, tq, n_slc, n_top):
    R, dh = NSA_REP, NSA_HD
    i = pl.program_id(2)
    s0 = pl.multiple_of(i * tq, tq)
    q_all = q_ref[...]
    q_heads = [q_all[r * dh:(r + 1) * dh, :] for r in range(R)]
    t_row = s0 + lax.broadcasted_iota(jnp.int32, (1, tq), 1)

    q4 = jnp.concatenate(q_heads, axis=1)
    n_c = kc_ref.shape[1]
    sc = _dot(kc_ref[0], q4)
    t4 = s0 + (lax.broadcasted_iota(jnp.int32, sc.shape, 1) & (tq - 1))
    valid_c = lax.broadcasted_iota(jnp.int32, sc.shape, 0) * CMP_STRIDE + (CMP_LEN - 1) <= t4
    sc = jnp.where(valid_c, sc, -jnp.inf)
    mc = jnp.max(sc, 0, keepdims=True)
    mc = jnp.where(mc == -jnp.inf, 0.0, mc)
    ec = jnp.where(valid_c, jnp.exp2(sc - mc), 0.0)
    p_cmp = ec / jnp.maximum(jnp.sum(ec, 0, keepdims=True), 1e-30)
    o_cmp = _dot_tn(vc_ref[0], p_cmp.astype(BF16))

    p_sum = p_cmp[:, 0:tq]
    for r in range(1, R):
        p_sum = p_sum + p_cmp[:, r * tq:(r + 1) * tq]
    p_hi = p_sum.astype(BF16)
    p_lo = (p_sum - p_hi.astype(F32)).astype(BF16)
    imp = _dot(ov_ref[...], p_hi) + _dot(ov_ref[...], p_lo)
    blk = lax.broadcasted_iota(jnp.int32, imp.shape, 0)
    cur = t_row // SLC_LEN
    forced = (blk == 0) | (blk == cur) | (blk == cur - 1)
    imp = jnp.where(forced, FORCE_SCORE, imp)
    imp = jnp.where(blk <= cur, imp, -jnp.inf)
    rank = jnp.zeros(imp.shape, jnp.int32)
    for b in range(n_slc):
        row = imp[b:b + 1, :]
        wins_tie = jnp.where(blk > b, 1, 0)
        rank = rank + jnp.where(row > imp, 1, jnp.where(row == imp, wins_tie, 0))
    selected = (rank < n_top) & (blk <= cur)
    sel_bias = jnp.where(selected, 0.0, MASK_NEG).astype(BF16)
    zero_half = jnp.zeros((dh, tq), BF16)
    q_sel = jnp.concatenate([jnp.concatenate([qh, sel_bias], axis=0) for qh in q_heads], axis=1)
    q_win = jnp.concatenate([jnp.concatenate([qh, zero_half], axis=0) for qh in q_heads], axis=1)

    def key_block(ref, blk_idx):
        return ref[0, pl.ds(pl.multiple_of(blk_idx * tq, tq), tq), :]

    def attend(state, k_blk, v_blk, queries, add_bias):
        m, acc = state
        s = _dot(k_blk, queries)
        if add_bias is not None:
            s = add_bias(s)
        m_new = jnp.maximum(m, jnp.max(s, 0, keepdims=True))
        p = jnp.exp2(s - m_new).astype(BF16)
        acc = jnp.exp2(m - m_new) * acc + _dot_tn(v_blk, p)
        return m_new, acc

    init = (jnp.full((1, R * tq), MASK_NEG, F32), jnp.zeros((2 * dh, R * tq), F32))
    causal_bias = bc_ref[...]

    def sel_pair(kb, state):
        k0 = pl.multiple_of(kb * 2 * tq, 2 * tq)
        return attend(state, ks_ref[0, pl.ds(k0, 2 * tq), :], vs_ref[0, pl.ds(k0, 2 * tq), :], q_sel, None)

    st_sel = lax.fori_loop(0, i // 2, sel_pair, init)
    prev_open = jnp.where(i % 2 == 1, 0.0, MASK_NEG)
    prev = jnp.maximum(i - 1, 0)
    k_fin = jnp.concatenate([key_block(ks_ref, prev), key_block(ks_ref, i)], axis=0)
    v_fin = jnp.concatenate([key_block(vs_ref, prev), key_block(vs_ref, i)], axis=0)
    _, acc_sel = attend(
        st_sel, k_fin, v_fin, q_sel,
        lambda s: jnp.concatenate([s[:tq] + prev_open, s[tq:] + causal_bias], axis=0))

    n_back = WIN // tq
    win_blocks = [jnp.maximum(i - back, 0) for back in range(n_back, -1, -1)]
    k_win = jnp.concatenate([key_block(kw_ref, blk_idx) for blk_idx in win_blocks], axis=0)
    v_win = jnp.concatenate([key_block(vw_ref, blk_idx) for blk_idx in win_blocks], axis=0)

    def window_bias(s):
        parts = []
        for n, back in enumerate(range(n_back, -1, -1)):
            blk = s[n * tq:(n + 1) * tq]
            if back == 0:
                parts.append(blk + causal_bias)
            else:
                exists = jnp.where(i >= back, 0.0, MASK_NEG)
                parts.append(blk + (bl_ref[...] + exists) if back == n_back else blk + exists)
        return jnp.concatenate(parts, axis=0)

    _, acc_win = attend(init, k_win, v_win, q_win, window_bias)

    gates = gt_ref[...]
    gate = [jnp.concatenate([gates[3 * r + c:3 * r + c + 1] for r in range(R)], axis=1) for c in range(3)]
    o = (gate[0] * o_cmp + (gate[1] * (1.0 / acc_sel[dh:dh + 1])) * acc_sel[:dh]
         + (gate[2] * (1.0 / acc_win[dh:dh + 1])) * acc_win[:dh])
    for r in range(R):
        o_ref[r * dh:(r + 1) * dh, :] = o[:, r * tq:(r + 1) * tq].astype(o_ref.dtype)


def _nsa_attention(q_t, gates_t, kc, vc, ks_aug, vs, kw, vw, overlap_t, batch, seq, tq):
    G, R, dh = NSA_GROUPS, NSA_REP, NSA_HD
    T = batch * seq
    nq = seq // tq
    n_slc = seq // SLC_LEN
    n_c = kc.shape[1]
    key = lax.broadcasted_iota(jnp.int32, (tq, R * tq), 0)
    qry = lax.broadcasted_iota(jnp.int32, (tq, R * tq), 1) % tq
    causal_bias = jnp.where(key <= qry, 0.0, MASK_NEG).astype(F32)
    lower_bias = jnp.where(key > qry, 0.0, MASK_NEG).astype(F32)
    per_bg = lambda b, g, i: (g * batch + b, 0, 0)
    seq_spec = pl.BlockSpec((1, seq, LANES), lambda b, g, i: (g, b, 0))
    const = lambda b, g, i: (0, 0)
    return pl.pallas_call(
        functools.partial(_nsa_kernel, tq=tq, n_slc=n_slc, n_top=min(SLC_TOPK, n_slc)),
        grid=(batch, G, nq),
        in_specs=[
            pl.BlockSpec((R * dh, tq), lambda b, g, i: (g, b * nq + i)),
            pl.BlockSpec((GATE_ROWS, tq), lambda b, g, i: (g, b * nq + i)),
            pl.BlockSpec((1, n_c, dh), per_bg),
            pl.BlockSpec((1, n_c, dh), per_bg),
            seq_spec, seq_spec, seq_spec, seq_spec,
            pl.BlockSpec((dh, n_c), const),
            pl.BlockSpec((tq, R * tq), const),
            pl.BlockSpec((tq, R * tq), const),
        ],
        out_specs=pl.BlockSpec((R * dh, tq), lambda b, g, i: (g, b * nq + i)),
        out_shape=jax.ShapeDtypeStruct((G * R * dh, T), BF16),
        compiler_params=_params("parallel", "parallel", "arbitrary"),
    )(q_t, gates_t, kc, vc, ks_aug, vs, kw, vw, overlap_t, causal_bias, lower_bias)


R_E1, R_E2, R_W1, R_W2, R_RANK1, R_RANK2 = range(6)
P_E1 = 6


def _router_kernel(x_ref, whi_ref, wlo_ref, route_ref, pieces_ref, before_ref, total_ref, cnt_ref):
    @pl.when(pl.program_id(0) == 0)
    def _():
        cnt_ref[...] = jnp.zeros_like(cnt_ref)

    x = x_ref[...]
    x_hi = x.astype(BF16)
    x_lo = (x - x_hi.astype(F32)).astype(BF16)
    logits = _dot(x_hi, whi_ref[...]) + _dot(x_hi, wlo_ref[...]) + _dot(x_lo, whi_ref[...])
    tc = logits.shape[0]
    lane = lax.broadcasted_iota(jnp.int32, logits.shape, 1).astype(F32)
    logits = jnp.where(lane < N_EXPERTS, logits, -jnp.inf)
    m1 = jnp.max(logits, -1, keepdims=True)
    i1 = jnp.min(jnp.where(logits == m1, lane, float(LANES)), -1, keepdims=True)
    rest = jnp.where(lane == i1, -jnp.inf, logits)
    m2 = jnp.max(rest, -1, keepdims=True)
    i2 = jnp.min(jnp.where(rest == m2, lane, float(LANES)), -1, keepdims=True)
    e2 = jnp.exp(m2 - m1)
    den = 1.0 + e2
    chosen = jnp.where(lane == i1, 1.0, jnp.where(lane == i2, 1.0, 0.0))
    earlier = (lax.broadcasted_iota(jnp.int32, (tc, tc), 1) < lax.broadcasted_iota(jnp.int32, (tc, tc), 0))
    before = cnt_ref[0:1, :]
    excl = _dot(jnp.where(earlier, 1.0, 0.0).astype(BF16), chosen.astype(BF16)) + before
    rank1 = jnp.sum(jnp.where(lane == i1, excl, 0.0), -1, keepdims=True)
    rank2 = jnp.sum(jnp.where(lane == i2, excl, 0.0), -1, keepdims=True)
    cols = {R_E1: i1, R_E2: i2, R_W1: 1.0 / den, R_W2: e2 / den, R_RANK1: rank1, R_RANK2: rank2}
    route = jnp.zeros_like(logits)
    for k, v in cols.items():
        route = jnp.where(lane == float(k), v, route)
    route_ref[...] = route
    pieces = jnp.where(lane == float(P_E1), i1, 0.0)
    for slot, weight in enumerate((1.0 / den, e2 / den)):
        rem = weight
        for k in range(3):
            piece = rem.astype(BF16).astype(F32)
            pieces = jnp.where(lane == float(3 * slot + k), piece, pieces)
            rem = rem - piece
    pieces_ref[...] = pieces.astype(BF16)
    before_ref[0] = jnp.broadcast_to(before, before_ref.shape[1:])
    after = before + jnp.sum(chosen, axis=0, keepdims=True)
    cnt_ref[...] = jnp.broadcast_to(after, cnt_ref.shape)
    total_ref[...] = jnp.broadcast_to(after, total_ref.shape)


def _router(x, w_hi, w_lo, tc):
    T, D = x.shape
    return pl.pallas_call(
        _router_kernel,
        grid=(T // tc,),
        in_specs=[
            pl.BlockSpec((tc, D), lambda i: (i, 0)),
            pl.BlockSpec((D, LANES), lambda i: (0, 0)),
            pl.BlockSpec((D, LANES), lambda i: (0, 0)),
        ],
        out_specs=[pl.BlockSpec((tc, LANES), lambda i: (i, 0)),
                   pl.BlockSpec((tc, LANES), lambda i: (i, 0)),
                   pl.BlockSpec((1, 8, LANES), lambda i: (i, 0, 0)),
                   pl.BlockSpec((8, LANES), lambda i: (0, 0))],
        out_shape=[jax.ShapeDtypeStruct((T, LANES), F32),
                   jax.ShapeDtypeStruct((T, LANES), BF16),
                   jax.ShapeDtypeStruct((T // tc, 8, LANES), F32),
                   jax.ShapeDtypeStruct((8, LANES), F32)],
        scratch_shapes=[pltpu.VMEM((8, LANES), F32)],
        compiler_params=_params("arbitrary"),
    )(x, w_hi, w_lo)


def _moe_plan(before, total, n_tok, tc, ts, tm):
    E = N_EXPERTS
    n_c = n_tok // tc
    n_rows = 2 * n_tok + E * tm
    n_sub, n_tiles = n_rows // ts, n_rows // tm
    n_work = E * n_c + 2 * n_sub
    i32 = jnp.int32
    tot = total[0, :E].astype(i32)
    cum = jnp.concatenate([before[:, 0, :E].astype(i32).T, tot[:, None]], axis=1)
    tiles_e = (tot + tm - 1) // tm
    tile_end = jnp.cumsum(tiles_e)
    row_off = (tile_end - tiles_e) * tm
    tile_id = jnp.arange(n_tiles, dtype=i32)
    tile_expert = jnp.minimum(jnp.sum(tile_id[:, None] >= tile_end[None, :], axis=1), E - 1).astype(i32)
    tile_valid = (tile_id < tile_end[-1]).astype(i32)
    sub = jnp.arange(n_sub, dtype=i32)
    sub_e = tile_expert[(sub * ts) // tm]
    rel0 = sub * ts - row_off[sub_e]
    rel1 = rel0 + ts
    cum_s = cum[sub_e]
    c_lo = jnp.sum(cum_s[:, 1:] <= rel0[:, None], axis=1).astype(i32)
    c_hi = jnp.sum(cum_s[:, :-1] < rel1[:, None], axis=1).astype(i32) - 1
    c_lo = jnp.clip(c_lo, 0, n_c - 1)
    c_hi = jnp.clip(c_hi, c_lo, n_c - 1)
    n_j = c_hi - c_lo + 1
    ends = jnp.cumsum(n_j)
    starts = ends - n_j
    n_valid = ends[-1]
    w = jnp.arange(n_work, dtype=i32)
    valid = w < n_valid
    jw = jnp.minimum(jnp.sum(w[:, None] >= ends[None, :], axis=1), n_sub - 1).astype(i32)
    cw = jnp.where(valid, c_lo[jw] + (w - starts[jw]), c_hi[n_sub - 1]).astype(i32)
    first = valid & (w == starts[jw])
    last = valid & (w == ends[jw] - 1)
    disp = (jw, cw, valid.astype(i32), first.astype(i32), last.astype(i32), sub_e[jw].astype(i32))
    order = jnp.argsort(jnp.where(valid, cw * n_sub + jw, n_c * n_sub + w))
    cj, cc, cv = jw[order], cw[order], valid[order]
    last_j, last_c = cj[n_valid - 1], cc[n_valid - 1]
    cj = jnp.where(cv, cj, last_j)
    cc = jnp.where(cv, cc, last_c)
    prev_c = jnp.concatenate([jnp.full((1,), -1, i32), cc[:-1]])
    next_c = jnp.concatenate([cc[1:], jnp.full((1,), -1, i32)])
    next_v = jnp.concatenate([cv[1:], jnp.zeros((1,), bool)])
    cfirst = cv & (cc != prev_c)
    clast = cv & ((cc != next_c) | ~next_v)
    comb = (cj.astype(i32), cc.astype(i32), cv.astype(i32), cfirst.astype(i32), clast.astype(i32))
    return row_off, tile_expert, tile_valid, disp, comb, n_rows, n_work


def _dispatch_kernel(jw, cw, vw, fw, lw, ew, x_ref, pc_ref, pos_ref, xs_ref, gs_ref, acc_ref, gacc_ref):
    w = pl.program_id(0)

    @pl.when(vw[w] == 1)
    def _():
        ts = acc_ref.shape[0]
        row_id = jw[w] * ts + lax.broadcasted_iota(jnp.int32, (ts, 1), 0)
        pos = pos_ref[0]
        hit = jnp.where(pos[0:1, :] == row_id, 1.0,
                        jnp.where(pos[1:2, :] == row_id, 1.0, 0.0)).astype(BF16)
        rows = _dot(hit, x_ref[...])
        side = _dot(hit, pc_ref[...])

        @pl.when(fw[w] == 1)
        def _():
            acc_ref[...] = rows
            gacc_ref[...] = side

        @pl.when(fw[w] == 0)
        def _():
            acc_ref[...] += rows
            gacc_ref[...] += side

        @pl.when(lw[w] == 1)
        def _():
            xs_ref[...] = acc_ref[...].astype(xs_ref.dtype)
            side_all = gacc_ref[...]
            lane = lax.broadcasted_iota(jnp.int32, side_all.shape, 1)
            lane_sum = lambda keep: jnp.sum(jnp.where(keep, side_all, 0.0), -1, keepdims=True)
            weight1 = lane_sum(lane < 3)
            weight2 = lane_sum((lane >= 3) & (lane < 6))
            expert1 = lane_sum(lane == P_E1)
            gate = jnp.where(expert1 == ew[w].astype(F32), weight1, weight2)
            gs_ref[...] = jnp.broadcast_to(gate, gs_ref.shape)


def _dispatch(xb, pieces, pos_rows, disp, n_rows, n_work, tc, ts):
    T, D = xb.shape
    by_chunk = lambda w, jw, cw, vw, fw, lw, ew: (cw[w], 0)
    by_sub = lambda w, jw, cw, vw, fw, lw, ew: (jw[w], 0)
    grid_spec = pltpu.PrefetchScalarGridSpec(
        num_scalar_prefetch=6,
        grid=(n_work,),
        in_specs=[
            pl.BlockSpec((tc, D), by_chunk),
            pl.BlockSpec((tc, LANES), by_chunk),
            pl.BlockSpec((1, 8, tc), lambda w, jw, cw, vw, fw, lw, ew: (cw[w], 0, 0)),
        ],
        out_specs=[pl.BlockSpec((ts, D), by_sub), pl.BlockSpec((ts, LANES), by_sub)],
        scratch_shapes=[pltpu.VMEM((ts, D), F32), pltpu.VMEM((ts, LANES), F32)],
    )
    return pl.pallas_call(
        _dispatch_kernel,
        grid_spec=grid_spec,
        out_shape=[jax.ShapeDtypeStruct((n_rows, D), BF16), jax.ShapeDtypeStruct((n_rows, LANES), F32)],
        compiler_params=_params("arbitrary"),
    )(*disp, xb, pieces, pos_rows)


def _moe_kernel(te, tv, xs_ref, gs_ref, wa_ref, wb_ref, wo_ref, y_ref, acc_ref):
    i = pl.program_id(0)
    f = pl.program_id(1)
    last_f = pl.num_programs(1) - 1

    @pl.when(tv[i] == 1)
    def _():
        xs = xs_ref[...]
        h = (_silu(_dot(xs, wa_ref[0])) * _dot(xs, wb_ref[0]) * gs_ref[:, 0:1]).astype(BF16)
        contrib = _dot(h, wo_ref[0])

        @pl.when(f == 0)
        def _():
            acc_ref[...] = contrib

        @pl.when(f > 0)
        def _():
            acc_ref[...] += contrib

        @pl.when(f == last_f)
        def _():
            y_ref[...] = acc_ref[...].astype(y_ref.dtype)

    @pl.when((tv[i] == 0) & (f == last_f))
    def _():
        y_ref[...] = jnp.zeros_like(y_ref)


def _moe(xs, gs, tile_expert, tile_valid, w_in, w_out, tm, tf):
    n_rows, D = xs.shape
    E, F, _ = w_out.shape
    nf = F // tf
    f_of = lambda i, f, te, tv: jnp.where(tv[i] == 1, f, nf - 1)
    grid_spec = pltpu.PrefetchScalarGridSpec(
        num_scalar_prefetch=2,
        grid=(n_rows // tm, nf),
        in_specs=[
            pl.BlockSpec((tm, D), lambda i, f, te, tv: (i, 0)),
            pl.BlockSpec((tm, LANES), lambda i, f, te, tv: (i, 0)),
            pl.BlockSpec((1, D, tf), lambda i, f, te, tv: (te[i], 0, f_of(i, f, te, tv))),
            pl.BlockSpec((1, D, tf), lambda i, f, te, tv: (te[i], 0, nf + f_of(i, f, te, tv))),
            pl.BlockSpec((1, tf, D), lambda i, f, te, tv: (te[i], f_of(i, f, te, tv), 0)),
        ],
        out_specs=pl.BlockSpec((tm, D), lambda i, f, te, tv: (i, 0)),
        scratch_shapes=[pltpu.VMEM((tm, D), F32)],
    )
    return pl.pallas_call(
        _moe_kernel,
        grid_spec=grid_spec,
        out_shape=jax.ShapeDtypeStruct((n_rows, D), BF16),
        compiler_params=_params("parallel", "arbitrary"),
    )(tile_expert, tile_valid, xs, gs, w_in, w_in, w_out)


def _combine_kernel(cj, cc, cv, cf, cl, y_ref, pos_ref, x_ref, g_ref, b_ref, o_ref, acc_ref):
    w = pl.program_id(0)

    @pl.when(cv[w] == 1)
    def _():
        ts = y_ref.shape[0]
        col_id = cj[w] * ts + lax.broadcasted_iota(jnp.int32, (1, ts), 1)
        pos = pos_ref[...]
        hit = (jnp.where(pos[:, 0:1] == col_id, 1.0, 0.0).astype(BF16)
               + jnp.where(pos[:, 1:2] == col_id, 1.0, 0.0).astype(BF16))
        contrib = _dot(hit, y_ref[...])

        @pl.when(cf[w] == 1)
        def _():
            acc_ref[...] = contrib

        @pl.when(cf[w] == 0)
        def _():
            acc_ref[...] += contrib

        @pl.when(cl[w] == 1)
        def _():
            o_ref[...] = _layer_norm(DN_ALPHA * x_ref[...] + acc_ref[...], g_ref[...], b_ref[...])


def _combine(y, pos_cols, x, g, b, comb, n_work, tc, ts):
    T, D = x.shape
    grid_spec = pltpu.PrefetchScalarGridSpec(
        num_scalar_prefetch=5,
        grid=(n_work,),
        in_specs=[
            pl.BlockSpec((ts, D), lambda w, cj, cc, cv, cf, cl: (cj[w], 0)),
            pl.BlockSpec((tc, LANES), lambda w, cj, cc, cv, cf, cl: (cc[w], 0)),
            pl.BlockSpec((tc, D), lambda w, cj, cc, cv, cf, cl: (cc[w], 0)),
            pl.BlockSpec((1, D), lambda w, cj, cc, cv, cf, cl: (0, 0)),
            pl.BlockSpec((1, D), lambda w, cj, cc, cv, cf, cl: (0, 0)),
        ],
        out_specs=pl.BlockSpec((tc, D), lambda w, cj, cc, cv, cf, cl: (cc[w], 0)),
        scratch_shapes=[pltpu.VMEM((tc, D), F32)],
    )
    return pl.pallas_call(
        _combine_kernel,
        grid_spec=grid_spec,
        out_shape=jax.ShapeDtypeStruct((T, D), F32),
        compiler_params=_params("arbitrary"),
    )(*comb, y, pos_cols, x, g.reshape(1, D), b.reshape(1, D))


def _moe_layer(x, xb, w_router, w_in, w_out, g, b, tc, tm, tf):
    T, D = x.shape
    ts = tc
    wr = jnp.pad(w_router, ((0, 0), (0, LANES - N_EXPERTS)))
    wr_hi = wr.astype(BF16)
    wr_lo = (wr - wr_hi.astype(F32)).astype(BF16)
    route, pieces, before, total = _router(x, wr_hi, wr_lo, tc)
    row_off, tile_expert, tile_valid, disp, comb, n_rows, n_work = _moe_plan(before, total, T, tc, ts, tm)
    e1 = route[:, R_E1].astype(jnp.int32)
    e2 = route[:, R_E2].astype(jnp.int32)
    pos1 = row_off[e1] + route[:, R_RANK1].astype(jnp.int32)
    pos2 = row_off[e2] + route[:, R_RANK2].astype(jnp.int32)
    pos = jnp.stack([pos1, pos2])
    pos_rows = jnp.pad(pos.reshape(2, T // tc, tc).transpose(1, 0, 2), ((0, 0), (0, 6), (0, 0)),
                       constant_values=-1)
    pos_cols = jnp.pad(pos.T, ((0, 0), (0, LANES - 2)), constant_values=-1)
    xs, gs = _dispatch(xb, pieces, pos_rows, disp, n_rows, n_work, tc, ts)
    y = _moe(xs, gs, tile_expert, tile_valid, w_in, w_out, tm, tf)
    return _combine(y, pos_cols, x, g, b, comb, n_work, tc, ts)


def kernel(x, ret_w_in, ret_gn_g, ret_w_out, nsa_w_kv, cmp_k_pe, cmp_k_w1, cmp_k_w2, cmp_v_pe, cmp_v_w1, cmp_v_w2,
           nsa_w_q, nsa_w_out, ffn_w_in, ffn_w_out, moe_router, moe_w_in, moe_w_out, ln_g, ln_b):
    B, S, D = x.shape
    T = B * S
    G, R, dh = NSA_GROUPS, NSA_REP, NSA_HD
    assert ret_w_in.shape[0] == 1 and nsa_w_q.shape[0] == 1 and ln_g.shape[0] == DEPTH
    dk = D // RET_HEADS
    dv = 2 * dk
    n_slc = S // SLC_LEN
    tq = 256
    assert n_slc <= dh and S % tq == 0 and WIN % tq == 0 and R * 3 <= GATE_ROWS
    tm = min(1024, S)

    xf = x.reshape(T, D)
    xb = xf.astype(BF16)

    cos_r, sin_r = _rope_tables(S, dk, dk)
    qkvg = _ret_proj(xb, ret_w_in[0].astype(BF16), cos_r[:, :dk // 2], -sin_r[:, :dk // 2], S, dk, tm)
    y = _retention(qkvg, ret_gn_g[0], B, S, dk, dv, rows=min(512, S))
    x1, x1b = _proj_ln(y, ret_w_out[0].astype(BF16), xf, ln_g[0, 0], ln_b[0, 0], tm=512)
    x2, x2b = _ffn(x1b, ffn_w_in[0].astype(BF16), ffn_w_out[0].astype(BF16), x1, ln_g[0, 1], ln_b[0, 1],
                   tm=512, tf=1408)

    cos_n, sin_n = _rope_tables(S, dh, LANES)
    ks_aug, vs, kw, vw, cmp_kv = _kv_proj(x2b, nsa_w_kv.astype(BF16), cos_n, sin_n, S, tm=512)
    n_c = S // CMP_STRIDE
    cmp_in = cmp_kv.reshape(2, G * B * n_c, CMP_STRIDE * dh)
    w1 = jnp.stack([cmp_k_w1, cmp_v_w1]).astype(BF16)
    w2 = jnp.stack([cmp_k_w2, cmp_v_w2]).astype(BF16)
    pe = jnp.stack([cmp_k_pe, cmp_v_pe]).reshape(2, 1, CMP_LEN * dh)
    pe = jnp.broadcast_to(pe, (2, 8, CMP_LEN * dh)).astype(BF16)
    kvc = _compress(cmp_in, w1, w2, pe, tm=min(1024, G * B * n_c)).reshape(2, G * B, n_c, dh)
    cs = jnp.arange(n_c) * CMP_STRIDE
    ss = jnp.arange(dh) * SLC_LEN
    overlap_t = jnp.clip(jnp.minimum(cs[None, :] + CMP_LEN, ss[:, None] + SLC_LEN)
                         - jnp.maximum(cs[None, :], ss[:, None]), 0).astype(F32) / CMP_STRIDE
    overlap_t = jnp.where(jnp.arange(dh)[:, None] < n_slc, overlap_t, 0.0).astype(BF16)

    nq_cols = NSA_HEADS * dh
    wq = nsa_w_q[0]
    wq_t = wq[:, :nq_cols].T.astype(BF16)
    wg_t = jnp.pad(wq[:, nq_cols:].T.reshape(G, R * 3, D), ((0, 0), (0, GATE_ROWS - R * 3), (0, 0)))
    wg_t = wg_t.reshape(G * GATE_ROWS, D).astype(BF16)
    ang = _rope_angles(S, dh).T
    q_t, gates_t = _q_proj(x2b, wq_t, wg_t, jnp.cos(ang), jnp.sin(ang), S, tm=512)
    attn_t = _nsa_attention(q_t, gates_t, kvc[0], kvc[1], ks_aug, vs, kw, vw, overlap_t, B, S, tq)
    x3, x3b = _proj_ln(attn_t, nsa_w_out[0].astype(BF16), x2, ln_g[1, 0], ln_b[1, 0], tm=512, y_transposed=True)

    out = _moe_layer(x3, x3b, moe_router[0], moe_w_in[0].astype(BF16), moe_w_out[0].astype(BF16),
                     ln_g[1, 1], ln_b[1, 1], tc=512, tm=1024, tf=896)
    return out.reshape(B, S, D)
```

```python
import functools

import jax
import jax.numpy as jnp
from jax import lax
from jax.experimental import pallas as pl
from jax.experimental.pallas import tpu as pltpu

F32 = jnp.float32
BF16 = jnp.bfloat16

DEPTH = 2
ROPE_THETA = 10000.0
LN_EPS = 1e-5
DN_ALPHA = (2.0 * DEPTH) ** 0.25

RET_HEADS = 4
RET_CHUNK = 128

NSA_HEADS = 16
NSA_GROUPS = 4
NSA_REP = NSA_HEADS // NSA_GROUPS
NSA_HD = 64
CMP_LEN = 32
CMP_STRIDE = 16
SLC_LEN = 64
SLC_TOPK = 8
WIN = 512
FORCE_SCORE = 1e9

N_EXPERTS = 8

LANES = 128
GATE_ROWS = 16
MASK_NEG = -1e30
LOG2_E = 1.4426950408889634
VMEM_LIMIT = 56 * 1024 * 1024


def _params(*sem):
    return pltpu.CompilerParams(dimension_semantics=sem, vmem_limit_bytes=VMEM_LIMIT)


def _dot(a, b):
    return jnp.dot(a, b, preferred_element_type=F32)


def _dot_nt(a, b):
    return lax.dot_general(a, b, (((1,), (1,)), ((), ())), preferred_element_type=F32)


def _dot_tn(a, b):
    return lax.dot_general(a, b, (((0,), (0,)), ((), ())), preferred_element_type=F32)


def _layer_norm(z, g, b):
    mu = jnp.mean(z, -1, keepdims=True)
    zc = z - mu
    var = jnp.mean(zc * zc, -1, keepdims=True)
    return zc * lax.rsqrt(var + LN_EPS) * g + b


def _silu(a):
    return a * jax.nn.sigmoid(a)


def _rope_angles(seq, dim):
    inv = 1.0 / (ROPE_THETA ** (jnp.arange(0, dim, 2, dtype=F32) / dim))
    return jnp.arange(seq, dtype=F32)[:, None] * inv[None, :]


def _rope_tables(seq, dim, width):
    ang = _rope_angles(seq, dim)
    cos, sin = jnp.cos(ang), jnp.sin(ang)
    cos_h = jnp.concatenate([cos, cos], -1)
    sin_h = jnp.concatenate([-sin, sin], -1)
    rep = width // dim
    return jnp.tile(cos_h, (1, rep)), jnp.tile(sin_h, (1, rep))


def _ret_proj_kernel(x_ref, w_ref, cos_ref, sin_ref, o_ref, *, dk, tn):
    x = x_ref[...]
    half = dk // 2
    for j in range(w_ref.shape[1] // tn):
        acc = _dot(x, w_ref[:, j * tn:(j + 1) * tn])
        if j < 2:
            scale = dk ** -0.5 if j == 1 else 1.0
            cos = cos_ref[...] * scale
            sin = sin_ref[...] * scale
            for h in range(tn // dk):
                a = acc[:, h * dk:h * dk + half]
                b = acc[:, h * dk + half:(h + 1) * dk]
                lo = j * tn + h * dk
                o_ref[:, lo:lo + half] = (a * cos - b * sin).astype(o_ref.dtype)
                o_ref[:, lo + half:lo + dk] = (a * sin + b * cos).astype(o_ref.dtype)
        else:
            o_ref[:, j * tn:(j + 1) * tn] = acc.astype(o_ref.dtype)


def _ret_proj(xb, w, cos, sin, seq, dk, tm):
    T, D = xb.shape
    N = w.shape[1]
    tn = RET_HEADS * dk
    half = dk // 2
    n_pos = seq // tm
    return pl.pallas_call(
        functools.partial(_ret_proj_kernel, dk=dk, tn=tn),
        grid=(T // tm,),
        in_specs=[
            pl.BlockSpec((tm, D), lambda i: (i, 0)),
            pl.BlockSpec((D, N), lambda i: (0, 0)),
            pl.BlockSpec((tm, half), lambda i: (i % n_pos, 0)),
            pl.BlockSpec((tm, half), lambda i: (i % n_pos, 0)),
        ],
        out_specs=pl.BlockSpec((tm, N), lambda i: (i, 0)),
        out_shape=jax.ShapeDtypeStruct((T, N), BF16),
        compiler_params=_params("parallel"),
    )(xb, w, cos, sin)


def _retention_kernel(q_ref, k_ref, v_ref, g_ref, dec_ref, qd_ref, kd_ref, gn_ref, o_ref, state_ref,
                      *, chunk, n_chunks):
    @pl.when(pl.program_id(2) == 0)
    def _():
        state_ref[...] = jnp.zeros_like(state_ref)

    decay = dec_ref[0]
    qd = qd_ref[0]
    kd = kd_ref[0]
    cd = qd[chunk - 1:chunk, :]
    gn = gn_ref[0]
    for c in range(n_chunks):
        rows = slice(c * chunk, (c + 1) * chunk)
        for u in range(q_ref.shape[0]):
            q = q_ref[u, rows, :]
            k = k_ref[u, rows, :]
            v = v_ref[u, rows, :]
            scores = _dot_nt(q, k) * decay
            inner = _dot(scores.astype(BF16), v)
            state = state_ref[u]
            cross = _dot(q, state.astype(BF16)) * qd
            out = inner + cross
            k_dec = (k.astype(F32) * kd).astype(BF16)
            state_ref[u] = state * cd + _dot_tn(k_dec, v)
            mu = jnp.mean(out, -1, keepdims=True)
            oc = out - mu
            var = jnp.mean(oc * oc, -1, keepdims=True)
            normed = oc * lax.rsqrt(var + LN_EPS) * gn
            o_ref[u, rows, :] = (_silu(g_ref[u, rows, :].astype(F32)) * normed).astype(o_ref.dtype)


def _retention(qkvg, gn_g, batch, seq, dk, dv, rows):
    H, C = RET_HEADS, RET_CHUNK
    T = batch * seq
    n_steps = seq // rows
    nb = 2 if batch % 2 == 0 else 1
    qkvg = qkvg.reshape(batch, seq, qkvg.shape[1])
    log_gamma = jnp.log1p(-(2.0 ** (-5.0 - jnp.arange(H, dtype=F32))))
    i = jnp.arange(C, dtype=F32)
    rel = i[:, None] - i[None, :]
    intra = jnp.where(rel >= 0, jnp.exp(log_gamma[:, None, None] * jnp.maximum(rel, 0.0)), 0.0)
    q_decay = jnp.exp(log_gamma[:, None] * (i + 1.0))[:, :, None]
    k_decay = jnp.exp(log_gamma[:, None] * (C - 1.0 - i))[:, :, None]
    v_off = (2 * H * dk) // dv
    g_off = (2 * H * dk + H * dv) // dv
    y = pl.pallas_call(
        functools.partial(_retention_kernel, chunk=C, n_chunks=rows // C),
        grid=(batch // nb, H, n_steps),
        in_specs=[
            pl.BlockSpec((nb, rows, dk), lambda b, h, n: (b, n, h)),
            pl.BlockSpec((nb, rows, dk), lambda b, h, n: (b, n, H + h)),
            pl.BlockSpec((nb, rows, dv), lambda b, h, n: (b, n, v_off + h)),
            pl.BlockSpec((nb, rows, dv), lambda b, h, n: (b, n, g_off + h)),
            pl.BlockSpec((1, C, C), lambda b, h, n: (h, 0, 0)),
            pl.BlockSpec((1, C, 1), lambda b, h, n: (h, 0, 0)),
            pl.BlockSpec((1, C, 1), lambda b, h, n: (h, 0, 0)),
            pl.BlockSpec((1, 1, dv), lambda b, h, n: (h, 0, 0)),
        ],
        out_specs=pl.BlockSpec((nb, rows, dv), lambda b, h, n: (b, n, h)),
        out_shape=jax.ShapeDtypeStruct((batch, seq, H * dv), BF16),
        scratch_shapes=[pltpu.VMEM((nb, dk, dv), F32)],
        compiler_params=_params("parallel", "parallel", "arbitrary"),
    )(qkvg, qkvg, qkvg, qkvg, intra, q_decay, k_decay, gn_g.reshape(H, 1, dv))
    return y.reshape(T, H * dv)


def _proj_ln_kernel(y_ref, w_ref, x_ref, g_ref, b_ref, o_ref, ob_ref, *, y_transposed):
    mix = _dot_tn(y_ref[...], w_ref[...]) if y_transposed else _dot(y_ref[...], w_ref[...])
    o = _layer_norm(DN_ALPHA * x_ref[...] + mix, g_ref[...], b_ref[...])
    o_ref[...] = o
    ob_ref[...] = o.astype(BF16)


def _proj_ln(y, w, x, g, b, tm, y_transposed=False):
    K, D = w.shape
    T = x.shape[0]
    y_spec = pl.BlockSpec((K, tm), lambda i: (0, i)) if y_transposed else pl.BlockSpec((tm, K), lambda i: (i, 0))
    return pl.pallas_call(
        functools.partial(_proj_ln_kernel, y_transposed=y_transposed),
        grid=(T // tm,),
        in_specs=[
            y_spec,
            pl.BlockSpec((K, D), lambda i: (0, 0)),
            pl.BlockSpec((tm, D), lambda i: (i, 0)),
            pl.BlockSpec((1, D), lambda i: (0, 0)),
            pl.BlockSpec((1, D), lambda i: (0, 0)),
        ],
        out_specs=[pl.BlockSpec((tm, D), lambda i: (i, 0)), pl.BlockSpec((tm, D), lambda i: (i, 0))],
        out_shape=[jax.ShapeDtypeStruct((T, D), F32), jax.ShapeDtypeStruct((T, D), BF16)],
        compiler_params=_params("parallel"),
    )(y, w, x, g.reshape(1, D), b.reshape(1, D))


def _ffn_kernel(xb_ref, wi_ref, wo_ref, x_ref, g_ref, b_ref, o_ref, ob_ref, *, tf):
    xb = xb_ref[...]
    F = wo_ref.shape[0]
    acc = None
    for f in range(F // tf):
        cols = slice(f * tf, (f + 1) * tf)
        gate_cols = slice(F + f * tf, F + (f + 1) * tf)
        h = (_silu(_dot(xb, wi_ref[:, cols])) * _dot(xb, wi_ref[:, gate_cols])).astype(BF16)
        contrib = _dot(h, wo_ref[cols, :])
        acc = contrib if acc is None else acc + contrib
    o = _layer_norm(DN_ALPHA * x_ref[...] + acc, g_ref[...], b_ref[...])
    o_ref[...] = o
    ob_ref[...] = o.astype(BF16)


def _ffn(xb, w_in, w_out, x, g, b, tm, tf):
    T, D = xb.shape
    F = w_out.shape[0]
    assert F % tf == 0
    return pl.pallas_call(
        functools.partial(_ffn_kernel, tf=tf),
        grid=(T // tm,),
        in_specs=[
            pl.BlockSpec((tm, D), lambda i: (i, 0)),
            pl.BlockSpec((D, 2 * F), lambda i: (0, 0)),
            pl.BlockSpec((F, D), lambda i: (0, 0)),
            pl.BlockSpec((tm, D), lambda i: (i, 0)),
            pl.BlockSpec((1, D), lambda i: (0, 0)),
            pl.BlockSpec((1, D), lambda i: (0, 0)),
        ],
        out_specs=[pl.BlockSpec((tm, D), lambda i: (i, 0)), pl.BlockSpec((tm, D), lambda i: (i, 0))],
        out_shape=[jax.ShapeDtypeStruct((T, D), F32), jax.ShapeDtypeStruct((T, D), BF16)],
        compiler_params=_params("parallel"),
    )(xb, w_in, w_out, x, g.reshape(1, D), b.reshape(1, D))


def _rope64(acc, cos, sin_signed):
    half = NSA_HD // 2
    lane = lax.broadcasted_iota(jnp.int32, acc.shape, 1)
    first = (lane % NSA_HD) < half
    rot = jnp.where(first, pltpu.roll(acc, LANES - half, 1), pltpu.roll(acc, half, 1))
    return acc * cos + rot * sin_signed


def _kv_proj_kernel(x_ref, w_ref, cos_ref, sin_ref, ks_ref, vs_ref, kw_ref, vw_ref, cmp_ref, *, n_pos):
    G, dh = NSA_GROUPS, NSA_HD
    acc = _dot(x_ref[...], w_ref[...])
    tm = acc.shape[0]
    cos, sin = cos_ref[...], sin_ref[...]
    lane = lax.broadcasted_iota(jnp.int32, (tm, LANES), 1)
    low = lane < dh
    pos = (pl.program_id(0) % n_pos) * tm + lax.broadcasted_iota(jnp.int32, (tm, LANES), 0)
    block_onehot = jnp.where(lane - dh == pos // SLC_LEN, 1.0, 0.0)
    ones_lane = jnp.where(lane == dh, 1.0, 0.0)
    swap = lambda a: pltpu.roll(a, dh, 1)

    def slab(branch, s):
        off = branch * G * dh + s * LANES
        return acc[:, off:off + LANES]

    for s in range(G * dh // LANES):
        k_cmp, v_cmp = _rope64(slab(0, s), cos, sin), slab(1, s)
        k_slc, v_slc = _rope64(slab(2, s), cos, sin), slab(3, s)
        k_win, v_win = _rope64(slab(4, s), cos, sin), slab(5, s)
        for h in range(2):
            g = 2 * s + h
            to_low = (lambda a: a) if h == 0 else swap
            ks_ref[g] = jnp.where(low, to_low(k_slc), block_onehot).astype(BF16)
            vs_ref[g] = jnp.where(low, to_low(v_slc), ones_lane).astype(BF16)
            kw_ref[g] = jnp.where(low, to_low(k_win), 0.0).astype(BF16)
            vw_ref[g] = jnp.where(low, to_low(v_win), ones_lane).astype(BF16)
            cmp_ref[0, g] = to_low(k_cmp)[:, :dh].astype(BF16)
            cmp_ref[1, g] = to_low(v_cmp)[:, :dh].astype(BF16)


def _kv_proj(xb, w, cos, sin, seq, tm):
    T, D = xb.shape
    N = w.shape[1]
    G, dh = NSA_GROUPS, NSA_HD
    n_pos = seq // tm
    wide = jax.ShapeDtypeStruct((G, T, LANES), BF16)
    wide_spec = pl.BlockSpec((G, tm, LANES), lambda i: (0, i, 0))
    return pl.pallas_call(
        functools.partial(_kv_proj_kernel, n_pos=n_pos),
        grid=(T // tm,),
        in_specs=[
            pl.BlockSpec((tm, D), lambda i: (i, 0)),
            pl.BlockSpec((D, N), lambda i: (0, 0)),
            pl.BlockSpec((tm, LANES), lambda i: (i % n_pos, 0)),
            pl.BlockSpec((tm, LANES), lambda i: (i % n_pos, 0)),
        ],
        out_specs=[wide_spec, wide_spec, wide_spec, wide_spec, pl.BlockSpec((2, G, tm, dh), lambda i: (0, 0, i, 0))],
        out_shape=[wide, wide, wide, wide, jax.ShapeDtypeStruct((2, G, T, dh), BF16)],
        compiler_params=_params("parallel"),
    )(xb, w, cos, sin)


def _q_proj_kernel(x_ref, wq_ref, wg_ref, cos_ref, sin_ref, q_ref, gate_ref):
    dh = NSA_HD
    half = dh // 2
    x = x_ref[...]
    acc = _dot_nt(wq_ref[...], x)
    cos = cos_ref[...] * (dh ** -0.5 * LOG2_E)
    sin = sin_ref[...] * (dh ** -0.5 * LOG2_E)
    for h in range(acc.shape[0] // dh):
        a = acc[h * dh:h * dh + half]
        b = acc[h * dh + half:(h + 1) * dh]
        q_ref[h * dh:h * dh + half, :] = (a * cos - b * sin).astype(q_ref.dtype)
        q_ref[h * dh + half:(h + 1) * dh, :] = (a * sin + b * cos).astype(q_ref.dtype)
    gate_ref[...] = jax.nn.sigmoid(_dot_nt(wg_ref[...], x))


def _q_proj(xb, wq_t, wg_t, cos_t, sin_t, seq, tm):
    T, D = xb.shape
    nq, ng = wq_t.shape[0], wg_t.shape[0]
    n_pos = seq // tm
    half = NSA_HD // 2
    return pl.pallas_call(
        _q_proj_kernel,
        grid=(T // tm,),
        in_specs=[
            pl.BlockSpec((tm, D), lambda i: (i, 0)),
            pl.BlockSpec((nq, D), lambda i: (0, 0)),
            pl.BlockSpec((ng, D), lambda i: (0, 0)),
            pl.BlockSpec((half, tm), lambda i: (0, i % n_pos)),
            pl.BlockSpec((half, tm), lambda i: (0, i % n_pos)),
        ],
        out_specs=[pl.BlockSpec((nq, tm), lambda i: (0, i)), pl.BlockSpec((ng, tm), lambda i: (0, i))],
        out_shape=[jax.ShapeDtypeStruct((nq, T), BF16), jax.ShapeDtypeStruct((ng, T), F32)],
        compiler_params=_params("parallel"),
    )(xb, wq_t, wg_t, cos_t, sin_t)


def _compress_kernel(a_ref, w1_ref, w2_ref, pe_ref, o_ref):
    a = a_ref[0]
    w1 = w1_ref[0]
    half = w1.shape[0] // 2
    first = _dot(a, w1[:half])
    second = _dot(a, w1[half:])
    bias = _dot(pe_ref[0], w1)[0:1]
    m = a.shape[0]
    hid = first + pltpu.roll(second, m - 1, 0) + bias
    o_ref[0] = _dot(jax.nn.gelu(hid).astype(BF16), w2_ref[0]).astype(o_ref.dtype)


def _compress(a, w1, w2, pe, tm):
    _, rows, width = a.shape
    hid = w1.shape[2]
    dh = w2.shape[2]
    return pl.pallas_call(
        _compress_kernel,
        grid=(2, rows // tm),
        in_specs=[
            pl.BlockSpec((1, tm, width), lambda s, i: (s, i, 0)),
            pl.BlockSpec((1, 2 * width, hid), lambda s, i: (s, 0, 0)),
            pl.BlockSpec((1, hid, dh), lambda s, i: (s, 0, 0)),
            pl.BlockSpec((1, 8, 2 * width), lambda s, i: (s, 0, 0)),
        ],
        out_specs=pl.BlockSpec((1, tm, dh), lambda s, i: (s, i, 0)),
        out_shape=jax.ShapeDtypeStruct((2, rows, dh), BF16),
        compiler_params=_params("parallel", "parallel"),
    )(a, w1, w2, pe)


def _nsa_kernel(q_ref, gt_ref, kc_ref, vc_ref, ks_ref, vs_ref, kw_ref, vw_ref, ov_ref, bc_ref, bl_ref, o_ref,
                *, tq, n_slc, n_top):
    R, dh = NSA_REP, NSA_HD
    i = pl.program_id(2)
    s0 = pl.multiple_of(i * tq, tq)
    q_all = q_ref[...]
    q_heads = [q_all[r * dh:(r + 1) * dh, :] for r in range(R)]
    t_row = s0 + lax.broadcasted_iota(jnp.int32, (1, tq), 1)

    q4 = jnp.concatenate(q_heads, axis=1)
    n_c = kc_ref.shape[1]
    sc = _dot(kc_ref[0], q4)
    t4 = s0 + (lax.broadcasted_iota(jnp.int32, sc.shape, 1) & (tq - 1))
    valid_c = lax.broadcasted_iota(jnp.int32, sc.shape, 0) * CMP_STRIDE + (CMP_LEN - 1) <= t4
    sc = jnp.where(valid_c, sc, -jnp.inf)
    mc = jnp.max(sc, 0, keepdims=True)
    mc = jnp.where(mc == -jnp.inf, 0.0, mc)
    ec = jnp.where(valid_c, jnp.exp2(sc - mc), 0.0)
    p_cmp = ec / jnp.maximum(jnp.sum(ec, 0, keepdims=True), 1e-30)
    o_cmp = _dot_tn(vc_ref[0], p_cmp.astype(BF16))

    p_sum = p_cmp[:, 0:tq]
    for r in range(1, R):
        p_sum = p_sum + p_cmp[:, r * tq:(r + 1) * tq]
    p_hi = p_sum.astype(BF16)
    p_lo = (p_sum - p_hi.astype(F32)).astype(BF16)
    imp = _dot(ov_ref[...], p_hi) + _dot(ov_ref[...], p_lo)
    blk = lax.broadcasted_iota(jnp.int32, imp.shape, 0)
    cur = t_row // SLC_LEN
    forced = (blk == 0) | (blk == cur) | (blk == cur - 1)
    imp = jnp.where(forced, FORCE_SCORE, imp)
    imp = jnp.where(blk <= cur, imp, -jnp.inf)
    rank = jnp.zeros(imp.shape, jnp.int32)
    for b in range(n_slc):
        row = imp[b:b + 1, :]
        wins_tie = jnp.where(blk > b, 1, 0)
        rank = rank + jnp.where(row > imp, 1, jnp.where(row == imp, wins_tie, 0))
    selected = (rank < n_top) & (blk <= cur)
    sel_bias = jnp.where(selected, 0.0, MASK_NEG).astype(BF16)
    zero_half = jnp.zeros((dh, tq), BF16)
    q_sel = jnp.concatenate([jnp.concatenate([qh, sel_bias], axis=0) for qh in q_heads], axis=1)
    q_win = jnp.concatenate([jnp.concatenate([qh, zero_half], axis=0) for qh in q_heads], axis=1)

    def key_block(ref, blk_idx):
        return ref[0, pl.ds(pl.multiple_of(blk_idx * tq, tq), tq), :]

    def attend(state, k_blk, v_blk, queries, add_bias):
        m, acc = state
        s = _dot(k_blk, queries)
        if add_bias is not None:
            s = add_bias(s)
        m_new = jnp.maximum(m, jnp.max(s, 0, keepdims=True))
        p = jnp.exp2(s - m_new).astype(BF16)
        acc = jnp.exp2(m - m_new) * acc + _dot_tn(v_blk, p)
        return m_new, acc

    init = (jnp.full((1, R * tq), MASK_NEG, F32), jnp.zeros((2 * dh, R * tq), F32))
    causal_bias = bc_ref[...]

    def sel_pair(kb, state):
        k0 = pl.multiple_of(kb * 2 * tq, 2 * tq)
        return attend(state, ks_ref[0, pl.ds(k0, 2 * tq), :], vs_ref[0, pl.ds(k0, 2 * tq), :], q_sel, None)

    st_sel = lax.fori_loop(0, i // 2, sel_pair, init)
    prev_open = jnp.where(i % 2 == 1, 0.0, MASK_NEG)
    prev = jnp.maximum(i - 1, 0)
    k_fin = jnp.concatenate([key_block(ks_ref, prev), key_block(ks_ref, i)], axis=0)
    v_fin = jnp.concatenate([key_block(vs_ref, prev), key_block(vs_ref, i)], axis=0)
    _, acc_sel = attend(
        st_sel, k_fin, v_fin, q_sel,
        lambda s: jnp.concatenate([s[:tq] + prev_open, s[tq:] + causal_bias], axis=0))

    n_back = WIN // tq
    win_blocks = [jnp.maximum(i - back, 0) for back in range(n_back, -1, -1)]
    k_win = jnp.concatenate([key_block(kw_ref, blk_idx) for blk_idx in win_blocks], axis=0)
    v_win = jnp.concatenate([key_block(vw_ref, blk_idx) for blk_idx in win_blocks], axis=0)

    def window_bias(s):
        parts = []
        for n, back in enumerate(range(n_back, -1, -1)):
            blk = s[n * tq:(n + 1) * tq]
            if back == 0:
                parts.append(blk + causal_bias)
            else:
                exists = jnp.where(i >= back, 0.0, MASK_NEG)
                parts.append(blk + (bl_ref[...] + exists) if back == n_back else blk + exists)
        return jnp.concatenate(parts, axis=0)

    _, acc_win = attend(init, k_win, v_win, q_win, window_bias)

    gates = gt_ref[...]
    gate = [jnp.concatenate([gates[3 * r + c:3 * r + c + 1] for r in range(R)], axis=1) for c in range(3)]
    o = (gate[0] * o_cmp + (gate[1] * (1.0 / acc_sel[dh:dh + 1])) * acc_sel[:dh]
         + (gate[2] * (1.0 / acc_win[dh:dh + 1])) * acc_win[:dh])
    for r in range(R):
        o_ref[r * dh:(r + 1) * dh, :] = o[:, r * tq:(r + 1) * tq].astype(o_ref.dtype)


def _nsa_attention(q_t, gates_t, kc, vc, ks_aug, vs, kw, vw, overlap_t, batch, seq, tq):
    G, R, dh = NSA_GROUPS, NSA_REP, NSA_HD
    T = batch * seq
    nq = seq // tq
    n_slc = seq // SLC_LEN
    n_c = kc.shape[1]
    key = lax.broadcasted_iota(jnp.int32, (tq, R * tq), 0)
    qry = lax.broadcasted_iota(jnp.int32, (tq, R * tq), 1) % tq
    causal_bias = jnp.where(key <= qry, 0.0, MASK_NEG).astype(F32)
    lower_bias = jnp.where(key > qry, 0.0, MASK_NEG).astype(F32)
    per_bg = lambda b, g, i: (g * batch + b, 0, 0)
    seq_spec = pl.BlockSpec((1, seq, LANES), lambda b, g, i: (g, b, 0))
    const = lambda b, g, i: (0, 0)
    return pl.pallas_call(
        functools.partial(_nsa_kernel, tq=tq, n_slc=n_slc, n_top=min(SLC_TOPK, n_slc)),
        grid=(batch, G, nq),
        in_specs=[
            pl.BlockSpec((R * dh, tq), lambda b, g, i: (g, b * nq + i)),
            pl.BlockSpec((GATE_ROWS, tq), lambda b, g, i: (g, b * nq + i)),
            pl.BlockSpec((1, n_c, dh), per_bg),
            pl.BlockSpec((1, n_c, dh), per_bg),
            seq_spec, seq_spec, seq_spec, seq_spec,
            pl.BlockSpec((dh, n_c), const),
            pl.BlockSpec((tq, R * tq), const),
            pl.BlockSpec((tq, R * tq), const),
        ],
        out_specs=pl.BlockSpec((R * dh, tq), lambda b, g, i: (g, b * nq + i)),
        out_shape=jax.ShapeDtypeStruct((G * R * dh, T), BF16),
        compiler_params=_params("parallel", "parallel", "arbitrary"),
    )(q_t, gates_t, kc, vc, ks_aug, vs, kw, vw, overlap_t, causal_bias, lower_bias)


R_E1, R_E2, R_W1, R_W2, R_RANK1, R_RANK2 = range(6)
P_E1 = 6


def _router_kernel(x_ref, whi_ref, wlo_ref, route_ref, pieces_ref, before_ref, total_ref, cnt_ref):
    @pl.when(pl.program_id(0) == 0)
    def _():
        cnt_ref[...] = jnp.zeros_like(cnt_ref)

    x = x_ref[...]
    x_hi = x.astype(BF16)
    x_lo = (x - x_hi.astype(F32)).astype(BF16)
    logits = _dot(x_hi, whi_ref[...]) + _dot(x_hi, wlo_ref[...]) + _dot(x_lo, whi_ref[...])
    tc = logits.shape[0]
    lane = lax.broadcasted_iota(jnp.int32, logits.shape, 1).astype(F32)
    logits = jnp.where(lane < N_EXPERTS, logits, -jnp.inf)
    m1 = jnp.max(logits, -1, keepdims=True)
    i1 = jnp.min(jnp.where(logits == m1, lane, float(LANES)), -1, keepdims=True)
    rest = jnp.where(lane == i1, -jnp.inf, logits)
    m2 = jnp.max(rest, -1, keepdims=True)
    i2 = jnp.min(jnp.where(rest == m2, lane, float(LANES)), -1, keepdims=True)
    e2 = jnp.exp(m2 - m1)
    den = 1.0 + e2
    chosen = jnp.where(lane == i1, 1.0, jnp.where(lane == i2, 1.0, 0.0))
    earlier = (lax.broadcasted_iota(jnp.int32, (tc, tc), 1) < lax.broadcasted_iota(jnp.int32, (tc, tc), 0))
    before = cnt_ref[0:1, :]
    excl = _dot(jnp.where(earlier, 1.0, 0.0).astype(BF16), chosen.astype(BF16)) + before
    rank1 = jnp.sum(jnp.where(lane == i1, excl, 0.0), -1, keepdims=True)
    rank2 = jnp.sum(jnp.where(lane == i2, excl, 0.0), -1, keepdims=True)
    cols = {R_E1: i1, R_E2: i2, R_W1: 1.0 / den, R_W2: e2 / den, R_RANK1: rank1, R_RANK2: rank2}
    route = jnp.zeros_like(logits)
    for k, v in cols.items():
        route = jnp.where(lane == float(k), v, route)
    route_ref[...] = route
    pieces = jnp.where(lane == float(P_E1), i1, 0.0)
    for slot, weight in enumerate((1.0 / den, e2 / den)):
        rem = weight
        for k in range(3):
            piece = rem.astype(BF16).astype(F32)
            pieces = jnp.where(lane == float(3 * slot + k), piece, pieces)
            rem = rem - piece
    pieces_ref[...] = pieces.astype(BF16)
    before_ref[0] = jnp.broadcast_to(before, before_ref.shape[1:])
    after = before + jnp.sum(chosen, axis=0, keepdims=True)
    cnt_ref[...] = jnp.broadcast_to(after, cnt_ref.shape)
    total_ref[...] = jnp.broadcast_to(after, total_ref.shape)


def _router(x, w_hi, w_lo, tc):
    T, D = x.shape
    return pl.pallas_call(
        _router_kernel,
        grid=(T // tc,),
        in_specs=[
            pl.BlockSpec((tc, D), lambda i: (i, 0)),
            pl.BlockSpec((D, LANES), lambda i: (0, 0)),
            pl.BlockSpec((D, LANES), lambda i: (0, 0)),
        ],
        out_specs=[pl.BlockSpec((tc, LANES), lambda i: (i, 0)),
                   pl.BlockSpec((tc, LANES), lambda i: (i, 0)),
                   pl.BlockSpec((1, 8, LANES), lambda i: (i, 0, 0)),
                   pl.BlockSpec((8, LANES), lambda i: (0, 0))],
        out_shape=[jax.ShapeDtypeStruct((T, LANES), F32),
                   jax.ShapeDtypeStruct((T, LANES), BF16),
                   jax.ShapeDtypeStruct((T // tc, 8, LANES), F32),
                   jax.ShapeDtypeStruct((8, LANES), F32)],
        scratch_shapes=[pltpu.VMEM((8, LANES), F32)],
        compiler_params=_params("arbitrary"),
    )(x, w_hi, w_lo)


def _moe_plan(before, total, n_tok, tc, ts, tm):
    E = N_EXPERTS
    n_c = n_tok // tc
    n_rows = 2 * n_tok + E * tm
    n_sub, n_tiles = n_rows // ts, n_rows // tm
    n_work = E * n_c + 2 * n_sub
    i32 = jnp.int32
    tot = total[0, :E].astype(i32)
    cum = jnp.concatenate([before[:, 0, :E].astype(i32).T, tot[:, None]], axis=1)
    tiles_e = (tot + tm - 1) // tm
    tile_end = jnp.cumsum(tiles_e)
    row_off = (tile_end - tiles_e) * tm
    tile_id = jnp.arange(n_tiles, dtype=i32)
    tile_expert = jnp.minimum(jnp.sum(tile_id[:, None] >= tile_end[None, :], axis=1), E - 1).astype(i32)
    tile_valid = (tile_id < tile_end[-1]).astype(i32)
    sub = jnp.arange(n_sub, dtype=i32)
    sub_e = tile_expert[(sub * ts) // tm]
    rel0 = sub * ts - row_off[sub_e]
    rel1 = rel0 + ts
    cum_s = cum[sub_e]
    c_lo = jnp.sum(cum_s[:, 1:] <= rel0[:, None], axis=1).astype(i32)
    c_hi = jnp.sum(cum_s[:, :-1] < rel1[:, None], axis=1).astype(i32) - 1
    c_lo = jnp.clip(c_lo, 0, n_c - 1)
    c_hi = jnp.clip(c_hi, c_lo, n_c - 1)
    n_j = c_hi - c_lo + 1
    ends = jnp.cumsum(n_j)
    starts = ends - n_j
    n_valid = ends[-1]
    w = jnp.arange(n_work, dtype=i32)
    valid = w < n_valid
    jw = jnp.minimum(jnp.sum(w[:, None] >= ends[None, :], axis=1), n_sub - 1).astype(i32)
    cw = jnp.where(valid, c_lo[jw] + (w - starts[jw]), c_hi[n_sub - 1]).astype(i32)
    first = valid & (w == starts[jw])
    last = valid & (w == ends[jw] - 1)
    disp = (jw, cw, valid.astype(i32), first.astype(i32), last.astype(i32), sub_e[jw].astype(i32))
    order = jnp.argsort(jnp.where(valid, cw * n_sub + jw, n_c * n_sub + w))
    cj, cc, cv = jw[order], cw[order], valid[order]
    last_j, last_c = cj[n_valid - 1], cc[n_valid - 1]
    cj = jnp.where(cv, cj, last_j)
    cc = jnp.where(cv, cc, last_c)
    prev_c = jnp.concatenate([jnp.full((1,), -1, i32), cc[:-1]])
    next_c = jnp.concatenate([cc[1:], jnp.full((1,), -1, i32)])
    next_v = jnp.concatenate([cv[1:], jnp.zeros((1,), bool)])
    cfirst = cv & (cc != prev_c)
    clast = cv & ((cc != next_c) | ~next_v)
    comb = (cj.astype(i32), cc.astype(i32), cv.astype(i32), cfirst.astype(i32), clast.astype(i32))
    return row_off, tile_expert, tile_valid, disp, comb, n_rows, n_work


def _dispatch_kernel(jw, cw, vw, fw, lw, ew, x_ref, pc_ref, pos_ref, xs_ref, gs_ref, acc_ref, gacc_ref):
    w = pl.program_id(0)

    @pl.when(vw[w] == 1)
    def _():
        ts = acc_ref.shape[0]
        row_id = jw[w] * ts + lax.broadcasted_iota(jnp.int32, (ts, 1), 0)
        pos = pos_ref[0]
        hit = jnp.where(pos[0:1, :] == row_id, 1.0,
                        jnp.where(pos[1:2, :] == row_id, 1.0, 0.0)).astype(BF16)
        rows = _dot(hit, x_ref[...])
        side = _dot(hit, pc_ref[...])

        @pl.when(fw[w] == 1)
        def _():
            acc_ref[...] = rows
            gacc_ref[...] = side

        @pl.when(fw[w] == 0)
        def _():
            acc_ref[...] += rows
            gacc_ref[...] += side

        @pl.when(lw[w] == 1)
        def _():
            xs_ref[...] = acc_ref[...].astype(xs_ref.dtype)
            side_all = gacc_ref[...]
            lane = lax.broadcasted_iota(jnp.int32, side_all.shape, 1)
            lane_sum = lambda keep: jnp.sum(jnp.where(keep, side_all, 0.0), -1, keepdims=True)
            weight1 = lane_sum(lane < 3)
            weight2 = lane_sum((lane >= 3) & (lane < 6))
            expert1 = lane_sum(lane == P_E1)
            gate = jnp.where(expert1 == ew[w].astype(F32), weight1, weight2)
            gs_ref[...] = jnp.broadcast_to(gate, gs_ref.shape)


def _dispatch(xb, pieces, pos_rows, disp, n_rows, n_work, tc, ts):
    T, D = xb.shape
    by_chunk = lambda w, jw, cw, vw, fw, lw, ew: (cw[w], 0)
    by_sub = lambda w, jw, cw, vw, fw, lw, ew: (jw[w], 0)
    grid_spec = pltpu.PrefetchScalarGridSpec(
        num_scalar_prefetch=6,
        grid=(n_work,),
        in_specs=[
            pl.BlockSpec((tc, D), by_chunk),
            pl.BlockSpec((tc, LANES), by_chunk),
            pl.BlockSpec((1, 8, tc), lambda w, jw, cw, vw, fw, lw, ew: (cw[w], 0, 0)),
        ],
        out_specs=[pl.BlockSpec((ts, D), by_sub), pl.BlockSpec((ts, LANES), by_sub)],
        scratch_shapes=[pltpu.VMEM((ts, D), F32), pltpu.VMEM((ts, LANES), F32)],
    )
    return pl.pallas_call(
        _dispatch_kernel,
        grid_spec=grid_spec,
        out_shape=[jax.ShapeDtypeStruct((n_rows, D), BF16), jax.ShapeDtypeStruct((n_rows, LANES), F32)],
        compiler_params=_params("arbitrary"),
    )(*disp, xb, pieces, pos_rows)


def _moe_kernel(te, tv, xs_ref, gs_ref, wa_ref, wb_ref, wo_ref, y_ref, acc_ref):
    i = pl.program_id(0)
    f = pl.program_id(1)
    last_f = pl.num_programs(1) - 1

    @pl.when(tv[i] == 1)
    def _():
        xs = xs_ref[...]
        h = (_silu(_dot(xs, wa_ref[0])) * _dot(xs, wb_ref[0]) * gs_ref[:, 0:1]).astype(BF16)
        contrib = _dot(h, wo_ref[0])

        @pl.when(f == 0)
        def _():
            acc_ref[...] = contrib

        @pl.when(f > 0)
        def _():
            acc_ref[...] += contrib

        @pl.when(f == last_f)
        def _():
            y_ref[...] = acc_ref[...].astype(y_ref.dtype)

    @pl.when((tv[i] == 0) & (f == last_f))
    def _():
        y_ref[...] = jnp.zeros_like(y_ref)


def _moe(xs, gs, tile_expert, tile_valid, w_in, w_out, tm, tf):
    n_rows, D = xs.shape
    E, F, _ = w_out.shape
    nf = F // tf
    f_of = lambda i, f, te, tv: jnp.where(tv[i] == 1, f, nf - 1)
    grid_spec = pltpu.PrefetchScalarGridSpec(
        num_scalar_prefetch=2,
        grid=(n_rows // tm, nf),
        in_specs=[
            pl.BlockSpec((tm, D), lambda i, f, te, tv: (i, 0)),
            pl.BlockSpec((tm, LANES), lambda i, f, te, tv: (i, 0)),
            pl.BlockSpec((1, D, tf), lambda i, f, te, tv: (te[i], 0, f_of(i, f, te, tv))),
            pl.BlockSpec((1, D, tf), lambda i, f, te, tv: (te[i], 0, nf + f_of(i, f, te, tv))),
            pl.BlockSpec((1, tf, D), lambda i, f, te, tv: (te[i], f_of(i, f, te, tv), 0)),
        ],
        out_specs=pl.BlockSpec((tm, D), lambda i, f, te, tv: (i, 0)),
        scratch_shapes=[pltpu.VMEM((tm, D), F32)],
    )
    return pl.pallas_call(
        _moe_kernel,
        grid_spec=grid_spec,
        out_shape=jax.ShapeDtypeStruct((n_rows, D), BF16),
        compiler_params=_params("parallel", "arbitrary"),
    )(tile_expert, tile_valid, xs, gs, w_in, w_in, w_out)


def _combine_kernel(cj, cc, cv, cf, cl, y_ref, pos_ref, x_ref, g_ref, b_ref, o_ref, acc_ref):
    w = pl.program_id(0)

    @pl.when(cv[w] == 1)
    def _():
        ts = y_ref.shape[0]
        col_id = cj[w] * ts + lax.broadcasted_iota(jnp.int32, (1, ts), 1)
        pos = pos_ref[...]
        hit = (jnp.where(pos[:, 0:1] == col_id, 1.0, 0.0).astype(BF16)
               + jnp.where(pos[:, 1:2] == col_id, 1.0, 0.0).astype(BF16))
        contrib = _dot(hit, y_ref[...])

        @pl.when(cf[w] == 1)
        def _():
            acc_ref[...] = contrib

        @pl.when(cf[w] == 0)
        def _():
            acc_ref[...] += contrib

        @pl.when(cl[w] == 1)
        def _():
            o_ref[...] = _layer_norm(DN_ALPHA * x_ref[...] + acc_ref[...], g_ref[...], b_ref[...])


def _combine(y, pos_cols, x, g, b, comb, n_work, tc, ts):
    T, D = x.shape
    grid_spec = pltpu.PrefetchScalarGridSpec(
        num_scalar_prefetch=5,
        grid=(n_work,),
        in_specs=[
            pl.BlockSpec((ts, D), lambda w, cj, cc, cv, cf, cl: (cj[w], 0)),
            pl.BlockSpec((tc, LANES), lambda w, cj, cc, cv, cf, cl: (cc[w], 0)),
            pl.BlockSpec((tc, D), lambda w, cj, cc, cv, cf, cl: (cc[w], 0)),
            pl.BlockSpec((1, D), lambda w, cj, cc, cv, cf, cl: (0, 0)),
            pl.BlockSpec((1, D), lambda w, cj, cc, cv, cf, cl: (0, 0)),
        ],
        out_specs=pl.BlockSpec((tc, D), lambda w, cj, cc, cv, cf, cl: (cc[w], 0)),
        scratch_shapes=[pltpu.VMEM((tc, D), F32)],
    )
    return pl.pallas_call(
        _combine_kernel,
        grid_spec=grid_spec,
        out_shape=jax.ShapeDtypeStruct((T, D), F32),
        compiler_params=_params("arbitrary"),
    )(*comb, y, pos_cols, x, g.reshape(1, D), b.reshape(1, D))


def _moe_layer(x, xb, w_router, w_in, w_out, g, b, tc, ts, tm, tf):
    T, D = x.shape
    wr = jnp.pad(w_router, ((0, 0), (0, LANES - N_EXPERTS)))
    wr_hi = wr.astype(BF16)
    wr_lo = (wr - wr_hi.astype(F32)).astype(BF16)
    route, pieces, before, total = _router(x, wr_hi, wr_lo, tc)
    row_off, tile_expert, tile_valid, disp, comb, n_rows, n_work = _moe_plan(before, total, T, tc, ts, tm)
    e1 = route[:, R_E1].astype(jnp.int32)
    e2 = route[:, R_E2].astype(jnp.int32)
    pos1 = row_off[e1] + route[:, R_RANK1].astype(jnp.int32)
    pos2 = row_off[e2] + route[:, R_RANK2].astype(jnp.int32)
    pos = jnp.stack([pos1, pos2])
    pos_rows = jnp.pad(pos.reshape(2, T // tc, tc).transpose(1, 0, 2), ((0, 0), (0, 6), (0, 0)),
                       constant_values=-1)
    pos_cols = jnp.pad(pos.T, ((0, 0), (0, LANES - 2)), constant_values=-1)
    xs, gs = _dispatch(xb, pieces, pos_rows, disp, n_rows, n_work, tc, ts)
    y = _moe(xs, gs, tile_expert, tile_valid, w_in, w_out, tm, tf)
    return _combine(y, pos_cols, x, g, b, comb, n_work, tc, ts)


def kernel(x, ret_w_in, ret_gn_g, ret_w_out, nsa_w_kv, cmp_k_pe, cmp_k_w1, cmp_k_w2, cmp_v_pe, cmp_v_w1, cmp_v_w2,
           nsa_w_q, nsa_w_out, ffn_w_in, ffn_w_out, moe_router, moe_w_in, moe_w_out, ln_g, ln_b):
    B, S, D = x.shape
    T = B * S
    G, R, dh = NSA_GROUPS, NSA_REP, NSA_HD
    assert ret_w_in.shape[0] == 1 and nsa_w_q.shape[0] == 1 and ln_g.shape[0] == DEPTH
    dk = D // RET_HEADS
    dv = 2 * dk
    n_slc = S // SLC_LEN
    tq = 256
    assert n_slc <= dh and S % tq == 0 and WIN % tq == 0 and R * 3 <= GATE_ROWS
    tm = min(1024, S)

    xf = x.reshape(T, D)
    xb = xf.astype(BF16)

    cos_r, sin_r = _rope_tables(S, dk, dk)
    qkvg = _ret_proj(xb, ret_w_in[0].astype(BF16), cos_r[:, :dk // 2], -sin_r[:, :dk // 2], S, dk, tm=512)
    y = _retention(qkvg, ret_gn_g[0], B, S, dk, dv, rows=min(512, S))
    x1, x1b = _proj_ln(y, ret_w_out[0].astype(BF16), xf, ln_g[0, 0], ln_b[0, 0], tm=512)
    x2, x2b = _ffn(x1b, ffn_w_in[0].astype(BF16), ffn_w_out[0].astype(BF16), x1, ln_g[0, 1], ln_b[0, 1],
                   tm=512, tf=1408)

    cos_n, sin_n = _rope_tables(S, dh, LANES)
    ks_aug, vs, kw, vw, cmp_kv = _kv_proj(x2b, nsa_w_kv.astype(BF16), cos_n, sin_n, S, tm=512)
    n_c = S // CMP_STRIDE
    cmp_in = cmp_kv.reshape(2, G * B * n_c, CMP_STRIDE * dh)
    w1 = jnp.stack([cmp_k_w1, cmp_v_w1]).astype(BF16)
    w2 = jnp.stack([cmp_k_w2, cmp_v_w2]).astype(BF16)
    pe = jnp.stack([cmp_k_pe, cmp_v_pe]).reshape(2, 1, CMP_LEN * dh)
    pe = jnp.broadcast_to(pe, (2, 8, CMP_LEN * dh)).astype(BF16)
    kvc = _compress(cmp_in, w1, w2, pe, tm=min(1024, G * B * n_c)).reshape(2, G * B, n_c, dh)
    cs = jnp.arange(n_c) * CMP_STRIDE
    ss = jnp.arange(dh) * SLC_LEN
    overlap_t = jnp.clip(jnp.minimum(cs[None, :] + CMP_LEN, ss[:, None] + SLC_LEN)
                         - jnp.maximum(cs[None, :], ss[:, None]), 0).astype(F32) / CMP_STRIDE
    overlap_t = jnp.where(jnp.arange(dh)[:, None] < n_slc, overlap_t, 0.0).astype(BF16)

    nq_cols = NSA_HEADS * dh
    wq = nsa_w_q[0]
    wq_t = wq[:, :nq_cols].T.astype(BF16)
    wg_t = jnp.pad(wq[:, nq_cols:].T.reshape(G, R * 3, D), ((0, 0), (0, GATE_ROWS - R * 3), (0, 0)))
    wg_t = wg_t.reshape(G * GATE_ROWS, D).astype(BF16)
    ang = _rope_angles(S, dh).T
    q_t, gates_t = _q_proj(x2b, wq_t, wg_t, jnp.cos(ang), jnp.sin(ang), S, tm=512)
    attn_t = _nsa_attention(q_t, gates_t, kvc[0], kvc[1], ks_aug, vs, kw, vw, overlap_t, B, S, tq)
    x3, x3b = _proj_ln(attn_t, nsa_w_out[0].astype(BF16), x2, ln_g[1, 0], ln_b[1, 0], tm=512, y_transposed=True)

    out = _moe_layer(x3, x3b, moe_router[0], moe_w_in[0].astype(BF16), moe_w_out[0].astype(BF16),
                     ln_g[1, 1], ln_b[1, 1], tc=512, ts=256, tm=1024, tf=896)
    return out.reshape(B, S, D)
```

```python
import functools

import jax
import jax.numpy as jnp
from jax import lax
from jax.experimental import pallas as pl
from jax.experimental.pallas import tpu as pltpu

F32 = jnp.float32
BF16 = jnp.bfloat16

DEPTH = 2
ROPE_THETA = 10000.0
LN_EPS = 1e-5
DN_ALPHA = (2.0 * DEPTH) ** 0.25

RET_HEADS = 4
RET_CHUNK = 128

NSA_HEADS = 16
NSA_GROUPS = 4
NSA_REP = NSA_HEADS // NSA_GROUPS
NSA_HD = 64
CMP_LEN = 32
CMP_STRIDE = 16
SLC_LEN = 64
SLC_TOPK = 8
WIN = 512
FORCE_SCORE = 1e9

N_EXPERTS = 8

LANES = 128
GATE_ROWS = 16
MASK_NEG = -1e30
LOG2_E = 1.4426950408889634
VMEM_LIMIT = 56 * 1024 * 1024


def _params(*sem):
    return pltpu.CompilerParams(dimension_semantics=sem, vmem_limit_bytes=VMEM_LIMIT)


def _dot(a, b):
    return jnp.dot(a, b, preferred_element_type=F32)


def _dot_nt(a, b):
    return lax.dot_general(a, b, (((1,), (1,)), ((), ())), preferred_element_type=F32)


def _dot_tn(a, b):
    return lax.dot_general(a, b, (((0,), (0,)), ((), ())), preferred_element_type=F32)


def _layer_norm(z, g, b):
    mu = jnp.mean(z, -1, keepdims=True)
    zc = z - mu
    var = jnp.mean(zc * zc, -1, keepdims=True)
    return zc * lax.rsqrt(var + LN_EPS) * g + b


def _silu(a):
    return a * jax.nn.sigmoid(a)


def _rope_angles(seq, dim):
    inv = 1.0 / (ROPE_THETA ** (jnp.arange(0, dim, 2, dtype=F32) / dim))
    return jnp.arange(seq, dtype=F32)[:, None] * inv[None, :]


def _rope_tables(seq, dim, width):
    ang = _rope_angles(seq, dim)
    cos, sin = jnp.cos(ang), jnp.sin(ang)
    cos_h = jnp.concatenate([cos, cos], -1)
    sin_h = jnp.concatenate([-sin, sin], -1)
    rep = width // dim
    return jnp.tile(cos_h, (1, rep)), jnp.tile(sin_h, (1, rep))


def _ret_proj_kernel(x_ref, w_ref, cos_ref, sin_ref, o_ref, *, dk, tn):
    x = x_ref[...].astype(BF16)
    half = dk // 2
    for j in range(w_ref.shape[1] // tn):
        acc = _dot(x, w_ref[:, j * tn:(j + 1) * tn])
        if j < 2:
            scale = dk ** -0.5 if j == 1 else 1.0
            cos = cos_ref[...] * scale
            sin = sin_ref[...] * scale
            for h in range(tn // dk):
                a = acc[:, h * dk:h * dk + half]
                b = acc[:, h * dk + half:(h + 1) * dk]
                lo = j * tn + h * dk
                o_ref[:, lo:lo + half] = (a * cos - b * sin).astype(o_ref.dtype)
                o_ref[:, lo + half:lo + dk] = (a * sin + b * cos).astype(o_ref.dtype)
        else:
            o_ref[:, j * tn:(j + 1) * tn] = acc.astype(o_ref.dtype)


def _ret_proj(xb, w, cos, sin, seq, dk, tm):
    T, D = xb.shape
    N = w.shape[1]
    tn = RET_HEADS * dk
    half = dk // 2
    n_pos = seq // tm
    return pl.pallas_call(
        functools.partial(_ret_proj_kernel, dk=dk, tn=tn),
        grid=(T // tm,),
        in_specs=[
            pl.BlockSpec((tm, D), lambda i: (i, 0)),
            pl.BlockSpec((D, N), lambda i: (0, 0)),
            pl.BlockSpec((tm, half), lambda i: (i % n_pos, 0)),
            pl.BlockSpec((tm, half), lambda i: (i % n_pos, 0)),
        ],
        out_specs=pl.BlockSpec((tm, N), lambda i: (i, 0)),
        out_shape=jax.ShapeDtypeStruct((T, N), BF16),
        compiler_params=_params("parallel"),
    )(xb, w, cos, sin)


def _retention_kernel(q_ref, k_ref, v_ref, g_ref, dec_ref, qd_ref, kd_ref, gn_ref, o_ref, state_ref,
                      *, chunk, n_chunks):
    @pl.when(pl.program_id(2) == 0)
    def _():
        state_ref[...] = jnp.zeros_like(state_ref)

    decay = dec_ref[0]
    qd = qd_ref[0]
    kd = kd_ref[0]
    cd = qd[chunk - 1:chunk, :]
    gn = gn_ref[0]
    for c in range(n_chunks):
        rows = slice(c * chunk, (c + 1) * chunk)
        for u in range(q_ref.shape[0]):
            q = q_ref[u, rows, :]
            k = k_ref[u, rows, :]
            v = v_ref[u, rows, :]
            scores = _dot_nt(q, k) * decay
            inner = _dot(scores.astype(BF16), v)
            state = state_ref[u]
            cross = _dot(q, state.astype(BF16)) * qd
            out = inner + cross
            k_dec = (k.astype(F32) * kd).astype(BF16)
            state_ref[u] = state * cd + _dot_tn(k_dec, v)
            mu = jnp.mean(out, -1, keepdims=True)
            oc = out - mu
            var = jnp.mean(oc * oc, -1, keepdims=True)
            normed = oc * lax.rsqrt(var + LN_EPS) * gn
            o_ref[u, rows, :] = (_silu(g_ref[u, rows, :].astype(F32)) * normed).astype(o_ref.dtype)


def _retention(qkvg, gn_g, batch, seq, dk, dv, rows):
    H, C = RET_HEADS, RET_CHUNK
    T = batch * seq
    n_steps = seq // rows
    nb = 2 if batch % 2 == 0 else 1
    qkvg = qkvg.reshape(batch, seq, qkvg.shape[1])
    log_gamma = jnp.log1p(-(2.0 ** (-5.0 - jnp.arange(H, dtype=F32))))
    i = jnp.arange(C, dtype=F32)
    rel = i[:, None] - i[None, :]
    intra = jnp.where(rel >= 0, jnp.exp(log_gamma[:, None, None] * jnp.maximum(rel, 0.0)), 0.0)
    q_decay = jnp.exp(log_gamma[:, None] * (i + 1.0))[:, :, None]
    k_decay = jnp.exp(log_gamma[:, None] * (C - 1.0 - i))[:, :, None]
    v_off = (2 * H * dk) // dv
    g_off = (2 * H * dk + H * dv) // dv
    y = pl.pallas_call(
        functools.partial(_retention_kernel, chunk=C, n_chunks=rows // C),
        grid=(batch // nb, H, n_steps),
        in_specs=[
            pl.BlockSpec((nb, rows, dk), lambda b, h, n: (b, n, h)),
            pl.BlockSpec((nb, rows, dk), lambda b, h, n: (b, n, H + h)),
            pl.BlockSpec((nb, rows, dv), lambda b, h, n: (b, n, v_off + h)),
            pl.BlockSpec((nb, rows, dv), lambda b, h, n: (b, n, g_off + h)),
            pl.BlockSpec((1, C, C), lambda b, h, n: (h, 0, 0)),
            pl.BlockSpec((1, C, 1), lambda b, h, n: (h, 0, 0)),
            pl.BlockSpec((1, C, 1), lambda b, h, n: (h, 0, 0)),
            pl.BlockSpec((1, 1, dv), lambda b, h, n: (h, 0, 0)),
        ],
        out_specs=pl.BlockSpec((nb, rows, dv), lambda b, h, n: (b, n, h)),
        out_shape=jax.ShapeDtypeStruct((batch, seq, H * dv), BF16),
        scratch_shapes=[pltpu.VMEM((nb, dk, dv), F32)],
        compiler_params=_params("parallel", "parallel", "arbitrary"),
    )(qkvg, qkvg, qkvg, qkvg, intra, q_decay, k_decay, gn_g.reshape(H, 1, dv))
    return y.reshape(T, H * dv)


def _proj_ln_kernel(y_ref, w_ref, x_ref, g_ref, b_ref, o_ref, ob_ref, *, y_transposed):
    mix = _dot_tn(y_ref[...], w_ref[...]) if y_transposed else _dot(y_ref[...], w_ref[...])
    o = _layer_norm(DN_ALPHA * x_ref[...] + mix, g_ref[...], b_ref[...])
    o_ref[...] = o
    ob_ref[...] = o.astype(BF16)


def _proj_ln(y, w, x, g, b, tm, y_transposed=False):
    K, D = w.shape
    T = x.shape[0]
    y_spec = pl.BlockSpec((K, tm), lambda i: (0, i)) if y_transposed else pl.BlockSpec((tm, K), lambda i: (i, 0))
    return pl.pallas_call(
        functools.partial(_proj_ln_kernel, y_transposed=y_transposed),
        grid=(T // tm,),
        in_specs=[
            y_spec,
            pl.BlockSpec((K, D), lambda i: (0, 0)),
            pl.BlockSpec((tm, D), lambda i: (i, 0)),
            pl.BlockSpec((1, D), lambda i: (0, 0)),
            pl.BlockSpec((1, D), lambda i: (0, 0)),
        ],
        out_specs=[pl.BlockSpec((tm, D), lambda i: (i, 0)), pl.BlockSpec((tm, D), lambda i: (i, 0))],
        out_shape=[jax.ShapeDtypeStruct((T, D), F32), jax.ShapeDtypeStruct((T, D), BF16)],
        compiler_params=_params("parallel"),
    )(y, w, x, g.reshape(1, D), b.reshape(1, D))


def _ffn_kernel(xb_ref, wi_ref, wo_ref, x_ref, g_ref, b_ref, o_ref, ob_ref, *, tf):
    xb = xb_ref[...]
    F = wo_ref.shape[0]
    acc = None
    for f in range(F // tf):
        cols = slice(f * tf, (f + 1) * tf)
        gate_cols = slice(F + f * tf, F + (f + 1) * tf)
        h = (_silu(_dot(xb, wi_ref[:, cols])) * _dot(xb, wi_ref[:, gate_cols])).astype(BF16)
        contrib = _dot(h, wo_ref[cols, :])
        acc = contrib if acc is None else acc + contrib
    o = _layer_norm(DN_ALPHA * x_ref[...] + acc, g_ref[...], b_ref[...])
    o_ref[...] = o
    ob_ref[...] = o.astype(BF16)


def _ffn(xb, w_in, w_out, x, g, b, tm, tf):
    T, D = xb.shape
    F = w_out.shape[0]
    assert F % tf == 0
    return pl.pallas_call(
        functools.partial(_ffn_kernel, tf=tf),
        grid=(T // tm,),
        in_specs=[
            pl.BlockSpec((tm, D), lambda i: (i, 0)),
            pl.BlockSpec((D, 2 * F), lambda i: (0, 0)),
            pl.BlockSpec((F, D), lambda i: (0, 0)),
            pl.BlockSpec((tm, D), lambda i: (i, 0)),
            pl.BlockSpec((1, D), lambda i: (0, 0)),
            pl.BlockSpec((1, D), lambda i: (0, 0)),
        ],
        out_specs=[pl.BlockSpec((tm, D), lambda i: (i, 0)), pl.BlockSpec((tm, D), lambda i: (i, 0))],
        out_shape=[jax.ShapeDtypeStruct((T, D), F32), jax.ShapeDtypeStruct((T, D), BF16)],
        compiler_params=_params("parallel"),
    )(xb, w_in, w_out, x, g.reshape(1, D), b.reshape(1, D))


def _rope64(acc, cos, sin_signed):
    half = NSA_HD // 2
    lane = lax.broadcasted_iota(jnp.int32, acc.shape, 1)
    first = (lane % NSA_HD) < half
    rot = jnp.where(first, pltpu.roll(acc, LANES - half, 1), pltpu.roll(acc, half, 1))
    return acc * cos + rot * sin_signed


def _kv_proj_kernel(x_ref, w_ref, cos_ref, sin_ref, ks_ref, vs_ref, kw_ref, vw_ref, cmp_ref, *, n_pos):
    G, dh = NSA_GROUPS, NSA_HD
    acc = _dot(x_ref[...], w_ref[...])
    tm = acc.shape[0]
    cos, sin = cos_ref[...], sin_ref[...]
    lane = lax.broadcasted_iota(jnp.int32, (tm, LANES), 1)
    low = lane < dh
    pos = (pl.program_id(0) % n_pos) * tm + lax.broadcasted_iota(jnp.int32, (tm, LANES), 0)
    block_onehot = jnp.where(lane - dh == pos // SLC_LEN, 1.0, 0.0)
    ones_lane = jnp.where(lane == dh, 1.0, 0.0)
    swap = lambda a: pltpu.roll(a, dh, 1)

    def slab(branch, s):
        off = branch * G * dh + s * LANES
        return acc[:, off:off + LANES]

    for s in range(G * dh // LANES):
        k_cmp, v_cmp = _rope64(slab(0, s), cos, sin), slab(1, s)
        k_slc, v_slc = _rope64(slab(2, s), cos, sin), slab(3, s)
        k_win, v_win = _rope64(slab(4, s), cos, sin), slab(5, s)
        for h in range(2):
            g = 2 * s + h
            to_low = (lambda a: a) if h == 0 else swap
            ks_ref[g] = jnp.where(low, to_low(k_slc), block_onehot).astype(BF16)
            vs_ref[g] = jnp.where(low, to_low(v_slc), ones_lane).astype(BF16)
            kw_ref[g] = jnp.where(low, to_low(k_win), 0.0).astype(BF16)
            vw_ref[g] = jnp.where(low, to_low(v_win), ones_lane).astype(BF16)
            cmp_ref[0, g] = to_low(k_cmp)[:, :dh].astype(BF16)
            cmp_ref[1, g] = to_low(v_cmp)[:, :dh].astype(BF16)


def _kv_proj(xb, w, cos, sin, seq, tm):
    T, D = xb.shape
    N = w.shape[1]
    G, dh = NSA_GROUPS, NSA_HD
    n_pos = seq // tm
    wide = jax.ShapeDtypeStruct((G, T, LANES), BF16)
    wide_spec = pl.BlockSpec((G, tm, LANES), lambda i: (0, i, 0))
    return pl.pallas_call(
        functools.partial(_kv_proj_kernel, n_pos=n_pos),
        grid=(T // tm,),
        in_specs=[
            pl.BlockSpec((tm, D), lambda i: (i, 0)),
            pl.BlockSpec((D, N), lambda i: (0, 0)),
            pl.BlockSpec((tm, LANES), lambda i: (i % n_pos, 0)),
            pl.BlockSpec((tm, LANES), lambda i: (i % n_pos, 0)),
        ],
        out_specs=[wide_spec, wide_spec, wide_spec, wide_spec, pl.BlockSpec((2, G, tm, dh), lambda i: (0, 0, i, 0))],
        out_shape=[wide, wide, wide, wide, jax.ShapeDtypeStruct((2, G, T, dh), BF16)],
        compiler_params=_params("parallel"),
    )(xb, w, cos, sin)


def _q_proj_kernel(x_ref, wq_ref, wg_ref, cos_ref, sin_ref, q_ref, gate_ref):
    dh = NSA_HD
    half = dh // 2
    x = x_ref[...]
    acc = _dot_nt(wq_ref[...], x)
    cos = cos_ref[...] * (dh ** -0.5 * LOG2_E)
    sin = sin_ref[...] * (dh ** -0.5 * LOG2_E)
    for h in range(acc.shape[0] // dh):
        a = acc[h * dh:h * dh + half]
        b = acc[h * dh + half:(h + 1) * dh]
        q_ref[h * dh:h * dh + half, :] = (a * cos - b * sin).astype(q_ref.dtype)
        q_ref[h * dh + half:(h + 1) * dh, :] = (a * sin + b * cos).astype(q_ref.dtype)
    gate_ref[...] = jax.nn.sigmoid(_dot_nt(wg_ref[...], x))


def _q_proj(xb, wq_t, wg_t, cos_t, sin_t, seq, tm):
    T, D = xb.shape
    nq, ng = wq_t.shape[0], wg_t.shape[0]
    n_pos = seq // tm
    half = NSA_HD // 2
    return pl.pallas_call(
        _q_proj_kernel,
        grid=(T // tm,),
        in_specs=[
            pl.BlockSpec((tm, D), lambda i: (i, 0)),
            pl.BlockSpec((nq, D), lambda i: (0, 0)),
            pl.BlockSpec((ng, D), lambda i: (0, 0)),
            pl.BlockSpec((half, tm), lambda i: (0, i % n_pos)),
            pl.BlockSpec((half, tm), lambda i: (0, i % n_pos)),
        ],
        out_specs=[pl.BlockSpec((nq, tm), lambda i: (0, i)), pl.BlockSpec((ng, tm), lambda i: (0, i))],
        out_shape=[jax.ShapeDtypeStruct((nq, T), BF16), jax.ShapeDtypeStruct((ng, T), F32)],
        compiler_params=_params("parallel"),
    )(xb, wq_t, wg_t, cos_t, sin_t)


def _compress_kernel(a_ref, w1_ref, w2_ref, pe_ref, o_ref):
    a = a_ref[0]
    w1 = w1_ref[0]
    half = w1.shape[0] // 2
    first = _dot(a, w1[:half])
    second = _dot(a, w1[half:])
    bias = _dot(pe_ref[0], w1)[0:1]
    m = a.shape[0]
    hid = first + pltpu.roll(second, m - 1, 0) + bias
    o_ref[0] = _dot(jax.nn.gelu(hid).astype(BF16), w2_ref[0]).astype(o_ref.dtype)


def _compress(a, w1, w2, pe, tm):
    _, rows, width = a.shape
    hid = w1.shape[2]
    dh = w2.shape[2]
    return pl.pallas_call(
        _compress_kernel,
        grid=(2, rows // tm),
        in_specs=[
            pl.BlockSpec((1, tm, width), lambda s, i: (s, i, 0)),
            pl.BlockSpec((1, 2 * width, hid), lambda s, i: (s, 0, 0)),
            pl.BlockSpec((1, hid, dh), lambda s, i: (s, 0, 0)),
            pl.BlockSpec((1, 8, 2 * width), lambda s, i: (s, 0, 0)),
        ],
        out_specs=pl.BlockSpec((1, tm, dh), lambda s, i: (s, i, 0)),
        out_shape=jax.ShapeDtypeStruct((2, rows, dh), BF16),
        compiler_params=_params("parallel", "parallel"),
    )(a, w1, w2, pe)


def _nsa_kernel(q_ref, gt_ref, kc_ref, vc_ref, ks_ref, vs_ref, kw_ref, vw_ref, ov_ref, bc_ref, bl_ref, o_ref,
                *, tq, n_slc, n_top):
    R, dh = NSA_REP, NSA_HD
    i = pl.program_id(2)
    s0 = pl.multiple_of(i * tq, tq)
    q_all = q_ref[...]
    q_heads = [q_all[r * dh:(r + 1) * dh, :] for r in range(R)]
    t_row = s0 + lax.broadcasted_iota(jnp.int32, (1, tq), 1)

    q4 = jnp.concatenate(q_heads, axis=1)
    n_c = kc_ref.shape[1]
    sc = _dot(kc_ref[0], q4)
    t4 = s0 + (lax.broadcasted_iota(jnp.int32, sc.shape, 1) & (tq - 1))
    valid_c = lax.broadcasted_iota(jnp.int32, sc.shape, 0) * CMP_STRIDE + (CMP_LEN - 1) <= t4
    sc = jnp.where(valid_c, sc, -jnp.inf)
    mc = jnp.max(sc, 0, keepdims=True)
    mc = jnp.where(mc == -jnp.inf, 0.0, mc)
    ec = jnp.where(valid_c, jnp.exp2(sc - mc), 0.0)
    p_cmp = ec / jnp.maximum(jnp.sum(ec, 0, keepdims=True), 1e-30)
    o_cmp = _dot_tn(vc_ref[0], p_cmp.astype(BF16))

    p_sum = p_cmp[:, 0:tq]
    for r in range(1, R):
        p_sum = p_sum + p_cmp[:, r * tq:(r + 1) * tq]
    p_hi = p_sum.astype(BF16)
    p_lo = (p_sum - p_hi.astype(F32)).astype(BF16)
    imp = _dot(ov_ref[...], p_hi) + _dot(ov_ref[...], p_lo)
    blk = lax.broadcasted_iota(jnp.int32, imp.shape, 0)
    cur = t_row // SLC_LEN
    forced = (blk == 0) | (blk == cur) | (blk == cur - 1)
    imp = jnp.where(forced, FORCE_SCORE, imp)
    imp = jnp.where(blk <= cur, imp, -jnp.inf)
    rank = jnp.zeros(imp.shape, jnp.int32)
    for b in range(n_slc):
        row = imp[b:b + 1, :]
        wins_tie = jnp.where(blk > b, 1, 0)
        rank = rank + jnp.where(row > imp, 1, jnp.where(row == imp, wins_tie, 0))
    selected = (rank < n_top) & (blk <= cur)
    sel_bias = jnp.where(selected, 0.0, MASK_NEG).astype(BF16)
    zero_half = jnp.zeros((dh, tq), BF16)
    q_sel = jnp.concatenate([jnp.concatenate([qh, sel_bias], axis=0) for qh in q_heads], axis=1)
    q_win = jnp.concatenate([jnp.concatenate([qh, zero_half], axis=0) for qh in q_heads], axis=1)

    def key_block(ref, blk_idx):
        return ref[0, pl.ds(pl.multiple_of(blk_idx * tq, tq), tq), :]

    def attend(state, k_blk, v_blk, queries, add_bias):
        m, acc = state
        s = _dot(k_blk, queries)
        if add_bias is not None:
            s = add_bias(s)
        m_new = jnp.maximum(m, jnp.max(s, 0, keepdims=True))
        p = jnp.exp2(s - m_new).astype(BF16)
        acc = jnp.exp2(m - m_new) * acc + _dot_tn(v_blk, p)
        return m_new, acc

    init = (jnp.full((1, R * tq), MASK_NEG, F32), jnp.zeros((2 * dh, R * tq), F32))
    causal_bias = bc_ref[...]

    def sel_pair(kb, state):
        k0 = pl.multiple_of(kb * 2 * tq, 2 * tq)
        return attend(state, ks_ref[0, pl.ds(k0, 2 * tq), :], vs_ref[0, pl.ds(k0, 2 * tq), :], q_sel, None)

    st_sel = lax.fori_loop(0, i // 2, sel_pair, init)
    prev_open = jnp.where(i % 2 == 1, 0.0, MASK_NEG)
    prev = jnp.maximum(i - 1, 0)
    k_fin = jnp.concatenate([key_block(ks_ref, prev), key_block(ks_ref, i)], axis=0)
    v_fin = jnp.concatenate([key_block(vs_ref, prev), key_block(vs_ref, i)], axis=0)
    _, acc_sel = attend(
        st_sel, k_fin, v_fin, q_sel,
        lambda s: jnp.concatenate([s[:tq] + prev_open, s[tq:] + causal_bias], axis=0))

    n_back = WIN // tq
    win_blocks = [jnp.maximum(i - back, 0) for back in range(n_back, -1, -1)]
    k_win = jnp.concatenate([key_block(kw_ref, blk_idx) for blk_idx in win_blocks], axis=0)
    v_win = jnp.concatenate([key_block(vw_ref, blk_idx) for blk_idx in win_blocks], axis=0)

    def window_bias(s):
        parts = []
        for n, back in enumerate(range(n_back, -1, -1)):
            blk = s[n * tq:(n + 1) * tq]
            if back == 0:
                parts.append(blk + causal_bias)
            else:
                exists = jnp.where(i >= back, 0.0, MASK_NEG)
                parts.append(blk + (bl_ref[...] + exists) if back == n_back else blk + exists)
        return jnp.concatenate(parts, axis=0)

    _, acc_win = attend(init, k_win, v_win, q_win, window_bias)

    gates = gt_ref[...]
    gate = [jnp.concatenate([gates[3 * r + c:3 * r + c + 1] for r in range(R)], axis=1) for c in range(3)]
    o = (gate[0] * o_cmp + (gate[1] * (1.0 / acc_sel[dh:dh + 1])) * acc_sel[:dh]
         + (gate[2] * (1.0 / acc_win[dh:dh + 1])) * acc_win[:dh])
    for r in range(R):
        o_ref[r * dh:(r + 1) * dh, :] = o[:, r * tq:(r + 1) * tq].astype(o_ref.dtype)


def _nsa_attention(q_t, gates_t, kc, vc, ks_aug, vs, kw, vw, overlap_t, batch, seq, tq):
    G, R, dh = NSA_GROUPS, NSA_REP, NSA_HD
    T = batch * seq
    nq = seq // tq
    n_slc = seq // SLC_LEN
    n_c = kc.shape[1]
    key = lax.broadcasted_iota(jnp.int32, (tq, R * tq), 0)
    qry = lax.broadcasted_iota(jnp.int32, (tq, R * tq), 1) % tq
    causal_bias = jnp.where(key <= qry, 0.0, MASK_NEG).astype(F32)
    lower_bias = jnp.where(key > qry, 0.0, MASK_NEG).astype(F32)
    per_bg = lambda b, g, i: (g * batch + b, 0, 0)
    seq_spec = pl.BlockSpec((1, seq, LANES), lambda b, g, i: (g, b, 0))
    const = lambda b, g, i: (0, 0)
    return pl.pallas_call(
        functools.partial(_nsa_kernel, tq=tq, n_slc=n_slc, n_top=min(SLC_TOPK, n_slc)),
        grid=(batch, G, nq),
        in_specs=[
            pl.BlockSpec((R * dh, tq), lambda b, g, i: (g, b * nq + i)),
            pl.BlockSpec((GATE_ROWS, tq), lambda b, g, i: (g, b * nq + i)),
            pl.BlockSpec((1, n_c, dh), per_bg),
            pl.BlockSpec((1, n_c, dh), per_bg),
            seq_spec, seq_spec, seq_spec, seq_spec,
            pl.BlockSpec((dh, n_c), const),
            pl.BlockSpec((tq, R * tq), const),
            pl.BlockSpec((tq, R * tq), const),
        ],
        out_specs=pl.BlockSpec((R * dh, tq), lambda b, g, i: (g, b * nq + i)),
        out_shape=jax.ShapeDtypeStruct((G * R * dh, T), BF16),
        compiler_params=_params("parallel", "parallel", "arbitrary"),
    )(q_t, gates_t, kc, vc, ks_aug, vs, kw, vw, overlap_t, causal_bias, lower_bias)


R_E1, R_E2, R_W1, R_W2, R_RANK1, R_RANK2 = range(6)
P_E1 = 6


def _router_kernel(x_ref, whi_ref, wlo_ref, route_ref, pieces_ref, before_ref, total_ref, cnt_ref):
    @pl.when(pl.program_id(0) == 0)
    def _():
        cnt_ref[...] = jnp.zeros_like(cnt_ref)

    x = x_ref[...]
    x_hi = x.astype(BF16)
    x_lo = (x - x_hi.astype(F32)).astype(BF16)
    logits = _dot(x_hi, whi_ref[...]) + _dot(x_hi, wlo_ref[...]) + _dot(x_lo, whi_ref[...])
    tc = logits.shape[0]
    lane = lax.broadcasted_iota(jnp.int32, logits.shape, 1).astype(F32)
    logits = jnp.where(lane < N_EXPERTS, logits, -jnp.inf)
    m1 = jnp.max(logits, -1, keepdims=True)
    i1 = jnp.min(jnp.where(logits == m1, lane, float(LANES)), -1, keepdims=True)
    rest = jnp.where(lane == i1, -jnp.inf, logits)
    m2 = jnp.max(rest, -1, keepdims=True)
    i2 = jnp.min(jnp.where(rest == m2, lane, float(LANES)), -1, keepdims=True)
    e2 = jnp.exp(m2 - m1)
    den = 1.0 + e2
    chosen = jnp.where(lane == i1, 1.0, jnp.where(lane == i2, 1.0, 0.0))
    earlier = (lax.broadcasted_iota(jnp.int32, (tc, tc), 1) < lax.broadcasted_iota(jnp.int32, (tc, tc), 0))
    before = cnt_ref[0:1, :]
    excl = _dot(jnp.where(earlier, 1.0, 0.0).astype(BF16), chosen.astype(BF16)) + before
    rank1 = jnp.sum(jnp.where(lane == i1, excl, 0.0), -1, keepdims=True)
    rank2 = jnp.sum(jnp.where(lane == i2, excl, 0.0), -1, keepdims=True)
    cols = {R_E1: i1, R_E2: i2, R_W1: 1.0 / den, R_W2: e2 / den, R_RANK1: rank1, R_RANK2: rank2}
    route = jnp.zeros_like(logits)
    for k, v in cols.items():
        route = jnp.where(lane == float(k), v, route)
    route_ref[...] = route
    pieces = jnp.where(lane == float(P_E1), i1, 0.0)
    for slot, weight in enumerate((1.0 / den, e2 / den)):
        rem = weight
        for k in range(3):
            piece = rem.astype(BF16).astype(F32)
            pieces = jnp.where(lane == float(3 * slot + k), piece, pieces)
            rem = rem - piece
    pieces_ref[...] = pieces.astype(BF16)
    before_ref[0] = jnp.broadcast_to(before, before_ref.shape[1:])
    after = before + jnp.sum(chosen, axis=0, keepdims=True)
    cnt_ref[...] = jnp.broadcast_to(after, cnt_ref.shape)
    total_ref[...] = jnp.broadcast_to(after, total_ref.shape)


def _router(x, w_hi, w_lo, tc):
    T, D = x.shape
    return pl.pallas_call(
        _router_kernel,
        grid=(T // tc,),
        in_specs=[
            pl.BlockSpec((tc, D), lambda i: (i, 0)),
            pl.BlockSpec((D, LANES), lambda i: (0, 0)),
            pl.BlockSpec((D, LANES), lambda i: (0, 0)),
        ],
        out_specs=[pl.BlockSpec((tc, LANES), lambda i: (i, 0)),
                   pl.BlockSpec((tc, LANES), lambda i: (i, 0)),
                   pl.BlockSpec((1, 8, LANES), lambda i: (i, 0, 0)),
                   pl.BlockSpec((8, LANES), lambda i: (0, 0))],
        out_shape=[jax.ShapeDtypeStruct((T, LANES), F32),
                   jax.ShapeDtypeStruct((T, LANES), BF16),
                   jax.ShapeDtypeStruct((T // tc, 8, LANES), F32),
                   jax.ShapeDtypeStruct((8, LANES), F32)],
        scratch_shapes=[pltpu.VMEM((8, LANES), F32)],
        compiler_params=_params("arbitrary"),
    )(x, w_hi, w_lo)


def _moe_plan(before, total, n_tok, tc, ts, tm):
    E = N_EXPERTS
    n_c = n_tok // tc
    n_rows = 2 * n_tok + E * tm
    n_sub, n_tiles = n_rows // ts, n_rows // tm
    n_work = E * n_c + n_sub
    i32 = jnp.int32
    tot = total[0, :E].astype(i32)
    cum = jnp.concatenate([before[:, 0, :E].astype(i32).T, tot[:, None]], axis=1)
    tiles_e = (tot + tm - 1) // tm
    tile_end = jnp.cumsum(tiles_e)
    row_off = (tile_end - tiles_e) * tm
    tile_id = jnp.arange(n_tiles, dtype=i32)
    tile_expert = jnp.minimum(jnp.sum(tile_id[:, None] >= tile_end[None, :], axis=1), E - 1).astype(i32)
    tile_valid = (tile_id < tile_end[-1]).astype(i32)
    sub = jnp.arange(n_sub, dtype=i32)
    sub_e = tile_expert[(sub * ts) // tm]
    rel0 = sub * ts - row_off[sub_e]
    rel1 = rel0 + ts
    cum_s = cum[sub_e]
    c_lo = jnp.sum(cum_s[:, 1:] <= rel0[:, None], axis=1).astype(i32)
    c_hi = jnp.sum(cum_s[:, :-1] < rel1[:, None], axis=1).astype(i32) - 1
    c_lo = jnp.clip(c_lo, 0, n_c - 1)
    c_hi = jnp.clip(c_hi, c_lo, n_c - 1)
    n_j = c_hi - c_lo + 1
    ends = jnp.cumsum(n_j)
    starts = ends - n_j
    n_valid = ends[-1]
    w = jnp.arange(n_work, dtype=i32)
    valid = w < n_valid
    jw = jnp.minimum(jnp.sum(w[:, None] >= ends[None, :], axis=1), n_sub - 1).astype(i32)
    cw = jnp.where(valid, c_lo[jw] + (w - starts[jw]), c_hi[n_sub - 1]).astype(i32)
    first = valid & (w == starts[jw])
    last = valid & (w == ends[jw] - 1)
    disp = (jw, cw, valid.astype(i32), first.astype(i32), last.astype(i32), sub_e[jw].astype(i32))
    order = jnp.argsort(jnp.where(valid, cw * n_sub + jw, n_c * n_sub + w))
    cj, cc, cv = jw[order], cw[order], valid[order]
    last_j, last_c = cj[n_valid - 1], cc[n_valid - 1]
    cj = jnp.where(cv, cj, last_j)
    cc = jnp.where(cv, cc, last_c)
    prev_c = jnp.concatenate([jnp.full((1,), -1, i32), cc[:-1]])
    next_c = jnp.concatenate([cc[1:], jnp.full((1,), -1, i32)])
    next_v = jnp.concatenate([cv[1:], jnp.zeros((1,), bool)])
    cfirst = cv & (cc != prev_c)
    clast = cv & ((cc != next_c) | ~next_v)
    comb = (cj.astype(i32), cc.astype(i32), cv.astype(i32), cfirst.astype(i32), clast.astype(i32))
    return row_off, tile_expert, tile_valid, disp, comb, n_rows, n_work


def _dispatch_kernel(jw, cw, vw, fw, lw, ew, x_ref, pc_ref, pos_ref, xs_ref, gs_ref, acc_ref, gacc_ref):
    w = pl.program_id(0)

    @pl.when(vw[w] == 1)
    def _():
        ts = acc_ref.shape[0]
        row_id = jw[w] * ts + lax.broadcasted_iota(jnp.int32, (ts, 1), 0)
        pos = pos_ref[0]
        hit = jnp.where(pos[0:1, :] == row_id, 1.0,
                        jnp.where(pos[1:2, :] == row_id, 1.0, 0.0)).astype(BF16)
        rows = _dot(hit, x_ref[...])
        side = _dot(hit, pc_ref[...])

        @pl.when(fw[w] == 1)
        def _():
            acc_ref[...] = rows
            gacc_ref[...] = side

        @pl.when(fw[w] == 0)
        def _():
            acc_ref[...] += rows
            gacc_ref[...] += side

        @pl.when(lw[w] == 1)
        def _():
            xs_ref[...] = acc_ref[...].astype(xs_ref.dtype)
            side_all = gacc_ref[...]
            lane = lax.broadcasted_iota(jnp.int32, side_all.shape, 1)
            lane_sum = lambda keep: jnp.sum(jnp.where(keep, side_all, 0.0), -1, keepdims=True)
            weight1 = lane_sum(lane < 3)
            weight2 = lane_sum((lane >= 3) & (lane < 6))
            expert1 = lane_sum(lane == P_E1)
            gate = jnp.where(expert1 == ew[w].astype(F32), weight1, weight2)
            gs_ref[...] = jnp.broadcast_to(gate, gs_ref.shape)


def _dispatch(xb, pieces, pos_rows, disp, n_rows, n_work, tc, ts):
    T, D = xb.shape
    by_chunk = lambda w, jw, cw, vw, fw, lw, ew: (cw[w], 0)
    by_sub = lambda w, jw, cw, vw, fw, lw, ew: (jw[w], 0)
    grid_spec = pltpu.PrefetchScalarGridSpec(
        num_scalar_prefetch=6,
        grid=(n_work,),
        in_specs=[
            pl.BlockSpec((tc, D), by_chunk),
            pl.BlockSpec((tc, LANES), by_chunk),
            pl.BlockSpec((1, 8, tc), lambda w, jw, cw, vw, fw, lw, ew: (cw[w], 0, 0)),
        ],
        out_specs=[pl.BlockSpec((ts, D), by_sub), pl.BlockSpec((ts, LANES), by_sub)],
        scratch_shapes=[pltpu.VMEM((ts, D), F32), pltpu.VMEM((ts, LANES), F32)],
    )
    return pl.pallas_call(
        _dispatch_kernel,
        grid_spec=grid_spec,
        out_shape=[jax.ShapeDtypeStruct((n_rows, D), BF16), jax.ShapeDtypeStruct((n_rows, LANES), F32)],
        compiler_params=_params("arbitrary"),
    )(*disp, xb, pieces, pos_rows)


def _moe_kernel(te, tv, xs_ref, gs_ref, wa_ref, wb_ref, wo_ref, y_ref, acc_ref):
    i = pl.program_id(0)
    f = pl.program_id(1)
    last_f = pl.num_programs(1) - 1

    @pl.when(tv[i] == 1)
    def _():
        xs = xs_ref[...]
        h = (_silu(_dot(xs, wa_ref[0])) * _dot(xs, wb_ref[0]) * gs_ref[:, 0:1]).astype(BF16)
        contrib = _dot(h, wo_ref[0])

        @pl.when(f == 0)
        def _():
            acc_ref[...] = contrib

        @pl.when(f > 0)
        def _():
            acc_ref[...] += contrib

        @pl.when(f == last_f)
        def _():
            y_ref[...] = acc_ref[...].astype(y_ref.dtype)

    @pl.when((tv[i] == 0) & (f == last_f))
    def _():
        y_ref[...] = jnp.zeros_like(y_ref)


def _moe(xs, gs, tile_expert, tile_valid, w_in, w_out, tm, tf):
    n_rows, D = xs.shape
    E, F, _ = w_out.shape
    nf = F // tf
    f_of = lambda i, f, te, tv: jnp.where(tv[i] == 1, f, nf - 1)
    grid_spec = pltpu.PrefetchScalarGridSpec(
        num_scalar_prefetch=2,
        grid=(n_rows // tm, nf),
        in_specs=[
            pl.BlockSpec((tm, D), lambda i, f, te, tv: (i, 0)),
            pl.BlockSpec((tm, LANES), lambda i, f, te, tv: (i, 0)),
            pl.BlockSpec((1, D, tf), lambda i, f, te, tv: (te[i], 0, f_of(i, f, te, tv))),
            pl.BlockSpec((1, D, tf), lambda i, f, te, tv: (te[i], 0, nf + f_of(i, f, te, tv))),
            pl.BlockSpec((1, tf, D), lambda i, f, te, tv: (te[i], f_of(i, f, te, tv), 0)),
        ],
        out_specs=pl.BlockSpec((tm, D), lambda i, f, te, tv: (i, 0)),
        scratch_shapes=[pltpu.VMEM((tm, D), F32)],
    )
    return pl.pallas_call(
        _moe_kernel,
        grid_spec=grid_spec,
        out_shape=jax.ShapeDtypeStruct((n_rows, D), BF16),
        compiler_params=_params("parallel", "arbitrary"),
    )(tile_expert, tile_valid, xs, gs, w_in, w_in, w_out)


def _combine_kernel(cj, cc, cv, cf, cl, y_ref, pos_ref, x_ref, g_ref, b_ref, o_ref, acc_ref):
    w = pl.program_id(0)

    @pl.when(cv[w] == 1)
    def _():
        ts = y_ref.shape[0]
        col_id = cj[w] * ts + lax.broadcasted_iota(jnp.int32, (1, ts), 1)
        pos = pos_ref[...]
        hit = (jnp.where(pos[:, 0:1] == col_id, 1.0, 0.0).astype(BF16)
               + jnp.where(pos[:, 1:2] == col_id, 1.0, 0.0).astype(BF16))
        contrib = _dot(hit, y_ref[...])

        @pl.when(cf[w] == 1)
        def _():
            acc_ref[...] = contrib

        @pl.when(cf[w] == 0)
        def _():
            acc_ref[...] += contrib

        @pl.when(cl[w] == 1)
        def _():
            o_ref[...] = _layer_norm(DN_ALPHA * x_ref[...] + acc_ref[...], g_ref[...], b_ref[...])


def _combine(y, pos_cols, x, g, b, comb, n_work, tc, ts):
    T, D = x.shape
    grid_spec = pltpu.PrefetchScalarGridSpec(
        num_scalar_prefetch=5,
        grid=(n_work,),
        in_specs=[
            pl.BlockSpec((ts, D), lambda w, cj, cc, cv, cf, cl: (cj[w], 0)),
            pl.BlockSpec((tc, LANES), lambda w, cj, cc, cv, cf, cl: (cc[w], 0)),
            pl.BlockSpec((tc, D), lambda w, cj, cc, cv, cf, cl: (cc[w], 0)),
            pl.BlockSpec((1, D), lambda w, cj, cc, cv, cf, cl: (0, 0)),
            pl.BlockSpec((1, D), lambda w, cj, cc, cv, cf, cl: (0, 0)),
        ],
        out_specs=pl.BlockSpec((tc, D), lambda w, cj, cc, cv, cf, cl: (cc[w], 0)),
        scratch_shapes=[pltpu.VMEM((tc, D), F32)],
    )
    return pl.pallas_call(
        _combine_kernel,
        grid_spec=grid_spec,
        out_shape=jax.ShapeDtypeStruct((T, D), F32),
        compiler_params=_params("arbitrary"),
    )(*comb, y, pos_cols, x, g.reshape(1, D), b.reshape(1, D))


def _moe_layer(x, xb, w_router, w_in, w_out, g, b, tc, ts, tm, tf):
    T, D = x.shape
    wr = jnp.pad(w_router, ((0, 0), (0, LANES - N_EXPERTS)))
    wr_hi = wr.astype(BF16)
    wr_lo = (wr - wr_hi.astype(F32)).astype(BF16)
    route, pieces, before, total = _router(x, wr_hi, wr_lo, tc)
    row_off, tile_expert, tile_valid, disp, comb, n_rows, n_work = _moe_plan(before, total, T, tc, ts, tm)
    e1 = route[:, R_E1].astype(jnp.int32)
    e2 = route[:, R_E2].astype(jnp.int32)
    pos1 = row_off[e1] + route[:, R_RANK1].astype(jnp.int32)
    pos2 = row_off[e2] + route[:, R_RANK2].astype(jnp.int32)
    pos = jnp.stack([pos1, pos2])
    pos_rows = jnp.pad(pos.reshape(2, T // tc, tc).transpose(1, 0, 2), ((0, 0), (0, 6), (0, 0)),
                       constant_values=-1)
    pos_cols = jnp.pad(pos.T, ((0, 0), (0, LANES - 2)), constant_values=-1)
    xs, gs = _dispatch(xb, pieces, pos_rows, disp, n_rows, n_work, tc, ts)
    y = _moe(xs, gs, tile_expert, tile_valid, w_in, w_out, tm, tf)
    return _combine(y, pos_cols, x, g, b, comb, n_work, tc, ts)


def kernel(x, ret_w_in, ret_gn_g, ret_w_out, nsa_w_kv, cmp_k_pe, cmp_k_w1, cmp_k_w2, cmp_v_pe, cmp_v_w1, cmp_v_w2,
           nsa_w_q, nsa_w_out, ffn_w_in, ffn_w_out, moe_router, moe_w_in, moe_w_out, ln_g, ln_b):
    B, S, D = x.shape
    T = B * S
    G, R, dh = NSA_GROUPS, NSA_REP, NSA_HD
    assert ret_w_in.shape[0] == 1 and nsa_w_q.shape[0] == 1 and ln_g.shape[0] == DEPTH
    dk = D // RET_HEADS
    dv = 2 * dk
    n_slc = S // SLC_LEN
    tq = 256
    assert n_slc <= dh and S % tq == 0 and WIN % tq == 0 and R * 3 <= GATE_ROWS
    tm = min(1024, S)

    xf = x.reshape(T, D)

    cos_r, sin_r = _rope_tables(S, dk, dk)
    qkvg = _ret_proj(xf, ret_w_in[0].astype(BF16), cos_r[:, :dk // 2], -sin_r[:, :dk // 2], S, dk, tm=512)
    y = _retention(qkvg, ret_gn_g[0], B, S, dk, dv, rows=min(512, S))
    x1, x1b = _proj_ln(y, ret_w_out[0].astype(BF16), xf, ln_g[0, 0], ln_b[0, 0], tm=512)
    x2, x2b = _ffn(x1b, ffn_w_in[0].astype(BF16), ffn_w_out[0].astype(BF16), x1, ln_g[0, 1], ln_b[0, 1],
                   tm=512, tf=1408)

    cos_n, sin_n = _rope_tables(S, dh, LANES)
    ks_aug, vs, kw, vw, cmp_kv = _kv_proj(x2b, nsa_w_kv.astype(BF16), cos_n, sin_n, S, tm=512)
    n_c = S // CMP_STRIDE
    cmp_in = cmp_kv.reshape(2, G * B * n_c, CMP_STRIDE * dh)
    w1 = jnp.stack([cmp_k_w1, cmp_v_w1]).astype(BF16)
    w2 = jnp.stack([cmp_k_w2, cmp_v_w2]).astype(BF16)
    pe = jnp.stack([cmp_k_pe, cmp_v_pe]).reshape(2, 1, CMP_LEN * dh)
    pe = jnp.broadcast_to(pe, (2, 8, CMP_LEN * dh)).astype(BF16)
    kvc = _compress(cmp_in, w1, w2, pe, tm=min(1024, G * B * n_c)).reshape(2, G * B, n_c, dh)
    cs = jnp.arange(n_c) * CMP_STRIDE
    ss = jnp.arange(dh) * SLC_LEN
    overlap_t = jnp.clip(jnp.minimum(cs[None, :] + CMP_LEN, ss[:, None] + SLC_LEN)
                         - jnp.maximum(cs[None, :], ss[:, None]), 0).astype(F32) / CMP_STRIDE
    overlap_t = jnp.where(jnp.arange(dh)[:, None] < n_slc, overlap_t, 0.0).astype(BF16)

    nq_cols = NSA_HEADS * dh
    wq = nsa_w_q[0]
    wq_t = wq[:, :nq_cols].T.astype(BF16)
    wg_t = jnp.pad(wq[:, nq_cols:].T.reshape(G, R * 3, D), ((0, 0), (0, GATE_ROWS - R * 3), (0, 0)))
    wg_t = wg_t.reshape(G * GATE_ROWS, D).astype(BF16)
    ang = _rope_angles(S, dh).T
    q_t, gates_t = _q_proj(x2b, wq_t, wg_t, jnp.cos(ang), jnp.sin(ang), S, tm=512)
    attn_t = _nsa_attention(q_t, gates_t, kvc[0], kvc[1], ks_aug, vs, kw, vw, overlap_t, B, S, tq)
    x3, x3b = _proj_ln(attn_t, nsa_w_out[0].astype(BF16), x2, ln_g[1, 0], ln_b[1, 0], tm=512, y_transposed=True)

    out = _moe_layer(x3, x3b, moe_router[0], moe_w_in[0].astype(BF16), moe_w_out[0].astype(BF16),
                     ln_g[1, 1], ln_b[1, 1], tc=min(1024, S), ts=256, tm=1024, tf=896)
    return out.reshape(B, S, D)
```

```python
import functools

import jax
import jax.numpy as jnp
from jax import lax
from jax.experimental import pallas as pl
from jax.experimental.pallas import tpu as pltpu

F32 = jnp.float32
BF16 = jnp.bfloat16

DEPTH = 2
ROPE_THETA = 10000.0
LN_EPS = 1e-5
DN_ALPHA = (2.0 * DEPTH) ** 0.25

RET_HEADS = 4
RET_CHUNK = 128

NSA_HEADS = 16
NSA_GROUPS = 4
NSA_REP = NSA_HEADS // NSA_GROUPS
NSA_HD = 64
CMP_LEN = 32
CMP_STRIDE = 16
SLC_LEN = 64
SLC_TOPK = 8
WIN = 512
FORCE_SCORE = 1e9

N_EXPERTS = 8

LANES = 128
GATE_ROWS = 16
MASK_NEG = -1e30
LOG2_E = 1.4426950408889634
VMEM_LIMIT = 56 * 1024 * 1024

TILES = {
    "ret_proj": 512, "retention": 512, "proj_ln": 1024, "ffn": 512, "ffn_hidden": 1408,
    "kv_proj": 512, "q_proj": 512, "compress": 1024, "nsa_queries": 256,
    "moe_chunk": 1024, "moe_sorted_rows": 256, "moe_expert_rows": 1024, "moe_hidden": 896,
}


def _params(*sem):
    return pltpu.CompilerParams(dimension_semantics=sem, vmem_limit_bytes=VMEM_LIMIT)


def _dot(a, b):
    return jnp.dot(a, b, preferred_element_type=F32)


def _dot_nt(a, b):
    return lax.dot_general(a, b, (((1,), (1,)), ((), ())), preferred_element_type=F32)


def _dot_tn(a, b):
    return lax.dot_general(a, b, (((0,), (0,)), ((), ())), preferred_element_type=F32)


def _layer_norm(z, g, b):
    mu = jnp.mean(z, -1, keepdims=True)
    zc = z - mu
    var = jnp.mean(zc * zc, -1, keepdims=True)
    return zc * lax.rsqrt(var + LN_EPS) * g + b


def _silu(a):
    return a * jax.nn.sigmoid(a)


def _rope_angles(seq, dim):
    inv = 1.0 / (ROPE_THETA ** (jnp.arange(0, dim, 2, dtype=F32) / dim))
    return jnp.arange(seq, dtype=F32)[:, None] * inv[None, :]


def _rope_tables(seq, dim, width):
    ang = _rope_angles(seq, dim)
    cos, sin = jnp.cos(ang), jnp.sin(ang)
    cos_h = jnp.concatenate([cos, cos], -1)
    sin_h = jnp.concatenate([-sin, sin], -1)
    rep = width // dim
    return jnp.tile(cos_h, (1, rep)), jnp.tile(sin_h, (1, rep))


def _ret_proj_kernel(x_ref, w_ref, cos_ref, sin_ref, o_ref, *, dk, tn):
    x = x_ref[...].astype(BF16)
    half = dk // 2
    for j in range(w_ref.shape[1] // tn):
        acc = _dot(x, w_ref[:, j * tn:(j + 1) * tn])
        if j < 2:
            scale = dk ** -0.5 if j == 1 else 1.0
            cos = cos_ref[...] * scale
            sin = sin_ref[...] * scale
            for h in range(tn // dk):
                a = acc[:, h * dk:h * dk + half]
                b = acc[:, h * dk + half:(h + 1) * dk]
                lo = j * tn + h * dk
                o_ref[:, lo:lo + half] = (a * cos - b * sin).astype(o_ref.dtype)
                o_ref[:, lo + half:lo + dk] = (a * sin + b * cos).astype(o_ref.dtype)
        else:
            o_ref[:, j * tn:(j + 1) * tn] = acc.astype(o_ref.dtype)


def _ret_proj(xb, w, cos, sin, seq, dk, tm):
    T, D = xb.shape
    N = w.shape[1]
    tn = RET_HEADS * dk
    half = dk // 2
    n_pos = seq // tm
    return pl.pallas_call(
        functools.partial(_ret_proj_kernel, dk=dk, tn=tn),
        grid=(T // tm,),
        in_specs=[
            pl.BlockSpec((tm, D), lambda i: (i, 0)),
            pl.BlockSpec((D, N), lambda i: (0, 0)),
            pl.BlockSpec((tm, half), lambda i: (i % n_pos, 0)),
            pl.BlockSpec((tm, half), lambda i: (i % n_pos, 0)),
        ],
        out_specs=pl.BlockSpec((tm, N), lambda i: (i, 0)),
        out_shape=jax.ShapeDtypeStruct((T, N), BF16),
        compiler_params=_params("parallel"),
    )(xb, w, cos, sin)


def _retention_kernel(q_ref, k_ref, v_ref, g_ref, dec_ref, qd_ref, kd_ref, gn_ref, o_ref, state_ref,
                      *, chunk, n_chunks):
    @pl.when(pl.program_id(2) == 0)
    def _():
        state_ref[...] = jnp.zeros_like(state_ref)

    decay = dec_ref[0]
    qd = qd_ref[0]
    kd = kd_ref[0]
    cd = qd[chunk - 1:chunk, :]
    gn = gn_ref[0]
    for c in range(n_chunks):
        rows = slice(c * chunk, (c + 1) * chunk)
        for u in range(q_ref.shape[0]):
            q = q_ref[u, rows, :]
            k = k_ref[u, rows, :]
            v = v_ref[u, rows, :]
            scores = _dot_nt(q, k) * decay
            inner = _dot(scores.astype(BF16), v)
            state = state_ref[u]
            cross = _dot(q, state.astype(BF16)) * qd
            out = inner + cross
            k_dec = (k.astype(F32) * kd).astype(BF16)
            state_ref[u] = state * cd + _dot_tn(k_dec, v)
            mu = jnp.mean(out, -1, keepdims=True)
            oc = out - mu
            var = jnp.mean(oc * oc, -1, keepdims=True)
            normed = oc * lax.rsqrt(var + LN_EPS) * gn
            o_ref[u, rows, :] = (_silu(g_ref[u, rows, :].astype(F32)) * normed).astype(o_ref.dtype)


def _retention(qkvg, gn_g, batch, seq, dk, dv, rows):
    H, C = RET_HEADS, RET_CHUNK
    T = batch * seq
    n_steps = seq // rows
    nb = 2 if batch % 2 == 0 else 1
    qkvg = qkvg.reshape(batch, seq, qkvg.shape[1])
    log_gamma = jnp.log1p(-(2.0 ** (-5.0 - jnp.arange(H, dtype=F32))))
    i = jnp.arange(C, dtype=F32)
    rel = i[:, None] - i[None, :]
    intra = jnp.where(rel >= 0, jnp.exp(log_gamma[:, None, None] * jnp.maximum(rel, 0.0)), 0.0)
    q_decay = jnp.exp(log_gamma[:, None] * (i + 1.0))[:, :, None]
    k_decay = jnp.exp(log_gamma[:, None] * (C - 1.0 - i))[:, :, None]
    v_off = (2 * H * dk) // dv
    g_off = (2 * H * dk + H * dv) // dv
    y = pl.pallas_call(
        functools.partial(_retention_kernel, chunk=C, n_chunks=rows // C),
        grid=(batch // nb, H, n_steps),
        in_specs=[
            pl.BlockSpec((nb, rows, dk), lambda b, h, n: (b, n, h)),
            pl.BlockSpec((nb, rows, dk), lambda b, h, n: (b, n, H + h)),
            pl.BlockSpec((nb, rows, dv), lambda b, h, n: (b, n, v_off + h)),
            pl.BlockSpec((nb, rows, dv), lambda b, h, n: (b, n, g_off + h)),
            pl.BlockSpec((1, C, C), lambda b, h, n: (h, 0, 0)),
            pl.BlockSpec((1, C, 1), lambda b, h, n: (h, 0, 0)),
            pl.BlockSpec((1, C, 1), lambda b, h, n: (h, 0, 0)),
            pl.BlockSpec((1, 1, dv), lambda b, h, n: (h, 0, 0)),
        ],
        out_specs=pl.BlockSpec((nb, rows, dv), lambda b, h, n: (b, n, h)),
        out_shape=jax.ShapeDtypeStruct((batch, seq, H * dv), BF16),
        scratch_shapes=[pltpu.VMEM((nb, dk, dv), F32)],
        compiler_params=_params("parallel", "parallel", "arbitrary"),
    )(qkvg, qkvg, qkvg, qkvg, intra, q_decay, k_decay, gn_g.reshape(H, 1, dv))
    return y.reshape(T, H * dv)


def _proj_ln_kernel(y_ref, w_ref, x_ref, g_ref, b_ref, o_ref, ob_ref, *, y_transposed):
    mix = _dot_tn(y_ref[...], w_ref[...]) if y_transposed else _dot(y_ref[...], w_ref[...])
    o = _layer_norm(DN_ALPHA * x_ref[...] + mix, g_ref[...], b_ref[...])
    o_ref[...] = o
    ob_ref[...] = o.astype(BF16)


def _proj_ln(y, w, x, g, b, tm, y_transposed=False):
    K, D = w.shape
    T = x.shape[0]
    y_spec = pl.BlockSpec((K, tm), lambda i: (0, i)) if y_transposed else pl.BlockSpec((tm, K), lambda i: (i, 0))
    return pl.pallas_call(
        functools.partial(_proj_ln_kernel, y_transposed=y_transposed),
        grid=(T // tm,),
        in_specs=[
            y_spec,
            pl.BlockSpec((K, D), lambda i: (0, 0)),
            pl.BlockSpec((tm, D), lambda i: (i, 0)),
            pl.BlockSpec((1, D), lambda i: (0, 0)),
            pl.BlockSpec((1, D), lambda i: (0, 0)),
        ],
        out_specs=[pl.BlockSpec((tm, D), lambda i: (i, 0)), pl.BlockSpec((tm, D), lambda i: (i, 0))],
        out_shape=[jax.ShapeDtypeStruct((T, D), F32), jax.ShapeDtypeStruct((T, D), BF16)],
        compiler_params=_params("parallel"),
    )(y, w, x, g.reshape(1, D), b.reshape(1, D))


def _ffn_kernel(xb_ref, wi_ref, wo_ref, x_ref, g_ref, b_ref, o_ref, ob_ref, *, tf):
    xb = xb_ref[...]
    F = wo_ref.shape[0]
    acc = None
    for f in range(F // tf):
        cols = slice(f * tf, (f + 1) * tf)
        gate_cols = slice(F + f * tf, F + (f + 1) * tf)
        h = (_silu(_dot(xb, wi_ref[:, cols])) * _dot(xb, wi_ref[:, gate_cols])).astype(BF16)
        contrib = _dot(h, wo_ref[cols, :])
        acc = contrib if acc is None else acc + contrib
    o = _layer_norm(DN_ALPHA * x_ref[...] + acc, g_ref[...], b_ref[...])
    o_ref[...] = o
    ob_ref[...] = o.astype(BF16)


def _ffn(xb, w_in, w_out, x, g, b, tm, tf):
    T, D = xb.shape
    F = w_out.shape[0]
    assert F % tf == 0
    return pl.pallas_call(
        functools.partial(_ffn_kernel, tf=tf),
        grid=(T // tm,),
        in_specs=[
            pl.BlockSpec((tm, D), lambda i: (i, 0)),
            pl.BlockSpec((D, 2 * F), lambda i: (0, 0)),
            pl.BlockSpec((F, D), lambda i: (0, 0)),
            pl.BlockSpec((tm, D), lambda i: (i, 0)),
            pl.BlockSpec((1, D), lambda i: (0, 0)),
            pl.BlockSpec((1, D), lambda i: (0, 0)),
        ],
        out_specs=[pl.BlockSpec((tm, D), lambda i: (i, 0)), pl.BlockSpec((tm, D), lambda i: (i, 0))],
        out_shape=[jax.ShapeDtypeStruct((T, D), F32), jax.ShapeDtypeStruct((T, D), BF16)],
        compiler_params=_params("parallel"),
    )(xb, w_in, w_out, x, g.reshape(1, D), b.reshape(1, D))


def _rope64(acc, cos, sin_signed):
    half = NSA_HD // 2
    lane = lax.broadcasted_iota(jnp.int32, acc.shape, 1)
    first = (lane % NSA_HD) < half
    rot = jnp.where(first, pltpu.roll(acc, LANES - half, 1), pltpu.roll(acc, half, 1))
    return acc * cos + rot * sin_signed


def _kv_proj_kernel(x_ref, w_ref, cos_ref, sin_ref, ks_ref, vs_ref, kw_ref, vw_ref, cmp_ref, *, n_pos):
    G, dh = NSA_GROUPS, NSA_HD
    acc = _dot(x_ref[...], w_ref[...])
    tm = acc.shape[0]
    cos, sin = cos_ref[...], sin_ref[...]
    lane = lax.broadcasted_iota(jnp.int32, (tm, LANES), 1)
    low = lane < dh
    pos = (pl.program_id(0) % n_pos) * tm + lax.broadcasted_iota(jnp.int32, (tm, LANES), 0)
    block_onehot = jnp.where(lane - dh == pos // SLC_LEN, 1.0, 0.0)
    ones_lane = jnp.where(lane == dh, 1.0, 0.0)
    swap = lambda a: pltpu.roll(a, dh, 1)

    def slab(branch, s):
        off = branch * G * dh + s * LANES
        return acc[:, off:off + LANES]

    for s in range(G * dh // LANES):
        k_cmp, v_cmp = _rope64(slab(0, s), cos, sin), slab(1, s)
        k_slc, v_slc = _rope64(slab(2, s), cos, sin), slab(3, s)
        k_win, v_win = _rope64(slab(4, s), cos, sin), slab(5, s)
        for h in range(2):
            g = 2 * s + h
            to_low = (lambda a: a) if h == 0 else swap
            ks_ref[g] = jnp.where(low, to_low(k_slc), block_onehot).astype(BF16)
            vs_ref[g] = jnp.where(low, to_low(v_slc), ones_lane).astype(BF16)
            kw_ref[g] = jnp.where(low, to_low(k_win), 0.0).astype(BF16)
            vw_ref[g] = jnp.where(low, to_low(v_win), ones_lane).astype(BF16)
            cmp_ref[0, g] = to_low(k_cmp)[:, :dh].astype(BF16)
            cmp_ref[1, g] = to_low(v_cmp)[:, :dh].astype(BF16)


def _kv_proj(xb, w, cos, sin, seq, tm):
    T, D = xb.shape
    N = w.shape[1]
    G, dh = NSA_GROUPS, NSA_HD
    n_pos = seq // tm
    wide = jax.ShapeDtypeStruct((G, T, LANES), BF16)
    wide_spec = pl.BlockSpec((G, tm, LANES), lambda i: (0, i, 0))
    return pl.pallas_call(
        functools.partial(_kv_proj_kernel, n_pos=n_pos),
        grid=(T // tm,),
        in_specs=[
            pl.BlockSpec((tm, D), lambda i: (i, 0)),
            pl.BlockSpec((D, N), lambda i: (0, 0)),
            pl.BlockSpec((tm, LANES), lambda i: (i % n_pos, 0)),
            pl.BlockSpec((tm, LANES), lambda i: (i % n_pos, 0)),
        ],
        out_specs=[wide_spec, wide_spec, wide_spec, wide_spec, pl.BlockSpec((2, G, tm, dh), lambda i: (0, 0, i, 0))],
        out_shape=[wide, wide, wide, wide, jax.ShapeDtypeStruct((2, G, T, dh), BF16)],
        compiler_params=_params("parallel"),
    )(xb, w, cos, sin)


def _q_proj_kernel(x_ref, wq_ref, wg_ref, cos_ref, sin_ref, q_ref, gate_ref):
    dh = NSA_HD
    half = dh // 2
    x = x_ref[...]
    acc = _dot_nt(wq_ref[...], x)
    cos = cos_ref[...] * (dh ** -0.5 * LOG2_E)
    sin = sin_ref[...] * (dh ** -0.5 * LOG2_E)
    for h in range(acc.shape[0] // dh):
        a = acc[h * dh:h * dh + half]
        b = acc[h * dh + half:(h + 1) * dh]
        q_ref[h * dh:h * dh + half, :] = (a * cos - b * sin).astype(q_ref.dtype)
        q_ref[h * dh + half:(h + 1) * dh, :] = (a * sin + b * cos).astype(q_ref.dtype)
    gate_ref[...] = jax.nn.sigmoid(_dot_nt(wg_ref[...], x))


def _q_proj(xb, wq_t, wg_t, cos_t, sin_t, seq, tm):
    T, D = xb.shape
    nq, ng = wq_t.shape[0], wg_t.shape[0]
    n_pos = seq // tm
    half = NSA_HD // 2
    return pl.pallas_call(
        _q_proj_kernel,
        grid=(T // tm,),
        in_specs=[
            pl.BlockSpec((tm, D), lambda i: (i, 0)),
            pl.BlockSpec((nq, D), lambda i: (0, 0)),
            pl.BlockSpec((ng, D), lambda i: (0, 0)),
            pl.BlockSpec((half, tm), lambda i: (0, i % n_pos)),
            pl.BlockSpec((half, tm), lambda i: (0, i % n_pos)),
        ],
        out_specs=[pl.BlockSpec((nq, tm), lambda i: (0, i)), pl.BlockSpec((ng, tm), lambda i: (0, i))],
        out_shape=[jax.ShapeDtypeStruct((nq, T), BF16), jax.ShapeDtypeStruct((ng, T), F32)],
        compiler_params=_params("parallel"),
    )(xb, wq_t, wg_t, cos_t, sin_t)


def _compress_kernel(a_ref, w1_ref, w2_ref, pe_ref, o_ref):
    a = a_ref[0]
    w1 = w1_ref[0]
    half = w1.shape[0] // 2
    first = _dot(a, w1[:half])
    second = _dot(a, w1[half:])
    bias = _dot(pe_ref[0], w1)[0:1]
    m = a.shape[0]
    hid = first + pltpu.roll(second, m - 1, 0) + bias
    o_ref[0] = _dot(jax.nn.gelu(hid).astype(BF16), w2_ref[0]).astype(o_ref.dtype)


def _compress(a, w1, w2, pe, tm):
    _, rows, width = a.shape
    hid = w1.shape[2]
    dh = w2.shape[2]
    return pl.pallas_call(
        _compress_kernel,
        grid=(2, rows // tm),
        in_specs=[
            pl.BlockSpec((1, tm, width), lambda s, i: (s, i, 0)),
            pl.BlockSpec((1, 2 * width, hid), lambda s, i: (s, 0, 0)),
            pl.BlockSpec((1, hid, dh), lambda s, i: (s, 0, 0)),
            pl.BlockSpec((1, 8, 2 * width), lambda s, i: (s, 0, 0)),
        ],
        out_specs=pl.BlockSpec((1, tm, dh), lambda s, i: (s, i, 0)),
        out_shape=jax.ShapeDtypeStruct((2, rows, dh), BF16),
        compiler_params=_params("parallel", "parallel"),
    )(a, w1, w2, pe)


def _nsa_kernel(q_ref, gt_ref, kc_ref, vc_ref, ks_ref, vs_ref, kw_ref, vw_ref, ov_ref, bc_ref, bl_ref, o_ref,
                *, tq, n_slc, n_top):
    R, dh = NSA_REP, NSA_HD
    i = pl.program_id(2)
    s0 = pl.multiple_of(i * tq, tq)
    q_all = q_ref[...]
    q_heads = [q_all[r * dh:(r + 1) * dh, :] for r in range(R)]
    t_row = s0 + lax.broadcasted_iota(jnp.int32, (1, tq), 1)

    q4 = jnp.concatenate(q_heads, axis=1)
    n_c = kc_ref.shape[1]
    sc = _dot(kc_ref[0], q4)
    t4 = s0 + (lax.broadcasted_iota(jnp.int32, sc.shape, 1) & (tq - 1))
    valid_c = lax.broadcasted_iota(jnp.int32, sc.shape, 0) * CMP_STRIDE + (CMP_LEN - 1) <= t4
    sc = jnp.where(valid_c, sc, -jnp.inf)
    mc = jnp.max(sc, 0, keepdims=True)
    mc = jnp.where(mc == -jnp.inf, 0.0, mc)
    ec = jnp.where(valid_c, jnp.exp2(sc - mc), 0.0)
    p_cmp = ec / jnp.maximum(jnp.sum(ec, 0, keepdims=True), 1e-30)
    o_cmp = _dot_tn(vc_ref[0], p_cmp.astype(BF16))

    p_sum = p_cmp[:, 0:tq]
    for r in range(1, R):
        p_sum = p_sum + p_cmp[:, r * tq:(r + 1) * tq]
    p_hi = p_sum.astype(BF16)
    p_lo = (p_sum - p_hi.astype(F32)).astype(BF16)
    imp = _dot(ov_ref[...], p_hi) + _dot(ov_ref[...], p_lo)
    blk = lax.broadcasted_iota(jnp.int32, imp.shape, 0)
    cur = t_row // SLC_LEN
    forced = (blk == 0) | (blk == cur) | (blk == cur - 1)
    imp = jnp.where(forced, FORCE_SCORE, imp)
    imp = jnp.where(blk <= cur, imp, -jnp.inf)
    rank = jnp.zeros(imp.shape, jnp.int32)
    for b in range(n_slc):
        row = imp[b:b + 1, :]
        wins_tie = jnp.where(blk > b, 1, 0)
        rank = rank + jnp.where(row > imp, 1, jnp.where(row == imp, wins_tie, 0))
    selected = (rank < n_top) & (blk <= cur)
    sel_bias = jnp.where(selected, 0.0, MASK_NEG).astype(BF16)
    if sel_bias.shape[0] < dh:
        sel_bias = jnp.concatenate([sel_bias, jnp.zeros((dh - sel_bias.shape[0], tq), BF16)], axis=0)
    zero_half = jnp.zeros((dh, tq), BF16)
    q_sel = jnp.concatenate([jnp.concatenate([qh, sel_bias], axis=0) for qh in q_heads], axis=1)
    q_win = jnp.concatenate([jnp.concatenate([qh, zero_half], axis=0) for qh in q_heads], axis=1)

    def key_block(ref, blk_idx):
        return ref[0, pl.ds(pl.multiple_of(blk_idx * tq, tq), tq), :]

    def attend(state, k_blk, v_blk, queries, add_bias):
        m, acc = state
        s = _dot(k_blk, queries)
        if add_bias is not None:
            s = add_bias(s)
        m_new = jnp.maximum(m, jnp.max(s, 0, keepdims=True))
        p = jnp.exp2(s - m_new).astype(BF16)
        acc = jnp.exp2(m - m_new) * acc + _dot_tn(v_blk, p)
        return m_new, acc

    init = (jnp.full((1, R * tq), MASK_NEG, F32), jnp.zeros((2 * dh, R * tq), F32))
    causal_bias = bc_ref[...]

    def sel_pair(kb, state):
        k0 = pl.multiple_of(kb * 2 * tq, 2 * tq)
        return attend(state, ks_ref[0, pl.ds(k0, 2 * tq), :], vs_ref[0, pl.ds(k0, 2 * tq), :], q_sel, None)

    st_sel = lax.fori_loop(0, i // 2, sel_pair, init)
    prev_open = jnp.where(i % 2 == 1, 0.0, MASK_NEG)
    prev = jnp.maximum(i - 1, 0)
    k_fin = jnp.concatenate([key_block(ks_ref, prev), key_block(ks_ref, i)], axis=0)
    v_fin = jnp.concatenate([key_block(vs_ref, prev), key_block(vs_ref, i)], axis=0)
    _, acc_sel = attend(
        st_sel, k_fin, v_fin, q_sel,
        lambda s: jnp.concatenate([s[:tq] + prev_open, s[tq:] + causal_bias], axis=0))

    n_back = WIN // tq
    win_blocks = [jnp.maximum(i - back, 0) for back in range(n_back, -1, -1)]
    k_win = jnp.concatenate([key_block(kw_ref, blk_idx) for blk_idx in win_blocks], axis=0)
    v_win = jnp.concatenate([key_block(vw_ref, blk_idx) for blk_idx in win_blocks], axis=0)

    def window_bias(s):
        parts = []
        for n, back in enumerate(range(n_back, -1, -1)):
            blk = s[n * tq:(n + 1) * tq]
            if back == 0:
                parts.append(blk + causal_bias)
            else:
                exists = jnp.where(i >= back, 0.0, MASK_NEG)
                parts.append(blk + (bl_ref[...] + exists) if back == n_back else blk + exists)
        return jnp.concatenate(parts, axis=0)

    _, acc_win = attend(init, k_win, v_win, q_win, window_bias)

    gates = gt_ref[...]
    gate = [jnp.concatenate([gates[3 * r + c:3 * r + c + 1] for r in range(R)], axis=1) for c in range(3)]
    o = (gate[0] * o_cmp + (gate[1] * (1.0 / acc_sel[dh:dh + 1])) * acc_sel[:dh]
         + (gate[2] * (1.0 / acc_win[dh:dh + 1])) * acc_win[:dh])
    for r in range(R):
        o_ref[r * dh:(r + 1) * dh, :] = o[:, r * tq:(r + 1) * tq].astype(o_ref.dtype)


def _nsa_attention(q_t, gates_t, kc, vc, ks_aug, vs, kw, vw, overlap_t, batch, seq, tq):
    G, R, dh = NSA_GROUPS, NSA_REP, NSA_HD
    T = batch * seq
    nq = seq // tq
    n_slc = seq // SLC_LEN
    n_c = kc.shape[1]
    n_blk = overlap_t.shape[0]
    key = lax.broadcasted_iota(jnp.int32, (tq, R * tq), 0)
    qry = lax.broadcasted_iota(jnp.int32, (tq, R * tq), 1) % tq
    causal_bias = jnp.where(key <= qry, 0.0, MASK_NEG).astype(F32)
    lower_bias = jnp.where(key > qry, 0.0, MASK_NEG).astype(F32)
    per_bg = lambda b, g, i: (g * batch + b, 0, 0)
    seq_spec = pl.BlockSpec((1, seq, LANES), lambda b, g, i: (g, b, 0))
    const = lambda b, g, i: (0, 0)
    return pl.pallas_call(
        functools.partial(_nsa_kernel, tq=tq, n_slc=n_slc, n_top=min(SLC_TOPK, n_slc)),
        grid=(batch, G, nq),
        in_specs=[
            pl.BlockSpec((R * dh, tq), lambda b, g, i: (g, b * nq + i)),
            pl.BlockSpec((GATE_ROWS, tq), lambda b, g, i: (g, b * nq + i)),
            pl.BlockSpec((1, n_c, dh), per_bg),
            pl.BlockSpec((1, n_c, dh), per_bg),
            seq_spec, seq_spec, seq_spec, seq_spec,
            pl.BlockSpec((n_blk, n_c), const),
            pl.BlockSpec((tq, R * tq), const),
            pl.BlockSpec((tq, R * tq), const),
        ],
        out_specs=pl.BlockSpec((R * dh, tq), lambda b, g, i: (g, b * nq + i)),
        out_shape=jax.ShapeDtypeStruct((G * R * dh, T), BF16),
        compiler_params=_params("parallel", "parallel", "arbitrary"),
    )(q_t, gates_t, kc, vc, ks_aug, vs, kw, vw, overlap_t, causal_bias, lower_bias)


R_E1, R_E2, R_W1, R_W2, R_RANK1, R_RANK2 = range(6)
P_E1 = 6


def _router_kernel(x_ref, whi_ref, wlo_ref, route_ref, pieces_ref, before_ref, total_ref, cnt_ref):
    @pl.when(pl.program_id(0) == 0)
    def _():
        cnt_ref[...] = jnp.zeros_like(cnt_ref)

    x = x_ref[...]
    x_hi = x.astype(BF16)
    x_lo = (x - x_hi.astype(F32)).astype(BF16)
    logits = _dot(x_hi, whi_ref[...]) + _dot(x_hi, wlo_ref[...]) + _dot(x_lo, whi_ref[...])
    tc = logits.shape[0]
    lane = lax.broadcasted_iota(jnp.int32, logits.shape, 1).astype(F32)
    logits = jnp.where(lane < N_EXPERTS, logits, -jnp.inf)
    m1 = jnp.max(logits, -1, keepdims=True)
    i1 = jnp.min(jnp.where(logits == m1, lane, float(LANES)), -1, keepdims=True)
    rest = jnp.where(lane == i1, -jnp.inf, logits)
    m2 = jnp.max(rest, -1, keepdims=True)
    i2 = jnp.min(jnp.where(rest == m2, lane, float(LANES)), -1, keepdims=True)
    e2 = jnp.exp(m2 - m1)
    den = 1.0 + e2
    chosen = jnp.where(lane == i1, 1.0, jnp.where(lane == i2, 1.0, 0.0))
    earlier = (lax.broadcasted_iota(jnp.int32, (tc, tc), 1) < lax.broadcasted_iota(jnp.int32, (tc, tc), 0))
    before = cnt_ref[0:1, :]
    excl = _dot(jnp.where(earlier, 1.0, 0.0).astype(BF16), chosen.astype(BF16)) + before
    rank1 = jnp.sum(jnp.where(lane == i1, excl, 0.0), -1, keepdims=True)
    rank2 = jnp.sum(jnp.where(lane == i2, excl, 0.0), -1, keepdims=True)
    cols = {R_E1: i1, R_E2: i2, R_W1: 1.0 / den, R_W2: e2 / den, R_RANK1: rank1, R_RANK2: rank2}
    route = jnp.zeros_like(logits)
    for k, v in cols.items():
        route = jnp.where(lane == float(k), v, route)
    route_ref[...] = route
    pieces = jnp.where(lane == float(P_E1), i1, 0.0)
    for slot, weight in enumerate((1.0 / den, e2 / den)):
        rem = weight
        for k in range(3):
            piece = rem.astype(BF16).astype(F32)
            pieces = jnp.where(lane == float(3 * slot + k), piece, pieces)
            rem = rem - piece
    pieces_ref[...] = pieces.astype(BF16)
    before_ref[0] = jnp.broadcast_to(before, before_ref.shape[1:])
    after = before + jnp.sum(chosen, axis=0, keepdims=True)
    cnt_ref[...] = jnp.broadcast_to(after, cnt_ref.shape)
    total_ref[...] = jnp.broadcast_to(after, total_ref.shape)


def _router(x, w_hi, w_lo, tc):
    T, D = x.shape
    return pl.pallas_call(
        _router_kernel,
        grid=(T // tc,),
        in_specs=[
            pl.BlockSpec((tc, D), lambda i: (i, 0)),
            pl.BlockSpec((D, LANES), lambda i: (0, 0)),
            pl.BlockSpec((D, LANES), lambda i: (0, 0)),
        ],
        out_specs=[pl.BlockSpec((tc, LANES), lambda i: (i, 0)),
                   pl.BlockSpec((tc, LANES), lambda i: (i, 0)),
                   pl.BlockSpec((1, 8, LANES), lambda i: (i, 0, 0)),
                   pl.BlockSpec((8, LANES), lambda i: (0, 0))],
        out_shape=[jax.ShapeDtypeStruct((T, LANES), F32),
                   jax.ShapeDtypeStruct((T, LANES), BF16),
                   jax.ShapeDtypeStruct((T // tc, 8, LANES), F32),
                   jax.ShapeDtypeStruct((8, LANES), F32)],
        scratch_shapes=[pltpu.VMEM((8, LANES), F32)],
        compiler_params=_params("arbitrary"),
    )(x, w_hi, w_lo)


def _moe_plan(before, total, n_tok, tc, ts, tm):
    E = N_EXPERTS
    n_c = n_tok // tc
    n_rows = 2 * n_tok + E * tm
    n_sub, n_tiles = n_rows // ts, n_rows // tm
    n_work = E * n_c + n_sub
    i32 = jnp.int32
    tot = total[0, :E].astype(i32)
    cum = jnp.concatenate([before[:, 0, :E].astype(i32).T, tot[:, None]], axis=1)
    tiles_e = (tot + tm - 1) // tm
    tile_end = jnp.cumsum(tiles_e)
    row_off = (tile_end - tiles_e) * tm
    tile_id = jnp.arange(n_tiles, dtype=i32)
    tile_expert = jnp.minimum(jnp.sum(tile_id[:, None] >= tile_end[None, :], axis=1), E - 1).astype(i32)
    tile_valid = (tile_id < tile_end[-1]).astype(i32)
    sub = jnp.arange(n_sub, dtype=i32)
    sub_e = tile_expert[(sub * ts) // tm]
    rel0 = sub * ts - row_off[sub_e]
    rel1 = rel0 + ts
    cum_s = cum[sub_e]
    c_lo = jnp.sum(cum_s[:, 1:] <= rel0[:, None], axis=1).astype(i32)
    c_hi = jnp.sum(cum_s[:, :-1] < rel1[:, None], axis=1).astype(i32) - 1
    c_lo = jnp.clip(c_lo, 0, n_c - 1)
    c_hi = jnp.clip(c_hi, c_lo, n_c - 1)
    n_j = c_hi - c_lo + 1
    ends = jnp.cumsum(n_j)
    starts = ends - n_j
    n_valid = ends[-1]
    w = jnp.arange(n_work, dtype=i32)
    valid = w < n_valid
    jw = jnp.minimum(jnp.sum(w[:, None] >= ends[None, :], axis=1), n_sub - 1).astype(i32)
    cw = jnp.where(valid, c_lo[jw] + (w - starts[jw]), c_hi[n_sub - 1]).astype(i32)
    first = valid & (w == starts[jw])
    last = valid & (w == ends[jw] - 1)
    disp = (jw, cw, valid.astype(i32), first.astype(i32), last.astype(i32), sub_e[jw].astype(i32))
    order = jnp.argsort(jnp.where(valid, cw * n_sub + jw, n_c * n_sub + w))
    cj, cc, cv = jw[order], cw[order], valid[order]
    last_j, last_c = cj[n_valid - 1], cc[n_valid - 1]
    cj = jnp.where(cv, cj, last_j)
    cc = jnp.where(cv, cc, last_c)
    prev_c = jnp.concatenate([jnp.full((1,), -1, i32), cc[:-1]])
    next_c = jnp.concatenate([cc[1:], jnp.full((1,), -1, i32)])
    next_v = jnp.concatenate([cv[1:], jnp.zeros((1,), bool)])
    cfirst = cv & (cc != prev_c)
    clast = cv & ((cc != next_c) | ~next_v)
    comb = (cj.astype(i32), cc.astype(i32), cv.astype(i32), cfirst.astype(i32), clast.astype(i32))
    return row_off, tile_expert, tile_valid, disp, comb, n_rows, n_work


def _dispatch_kernel(jw, cw, vw, fw, lw, ew, x_ref, pc_ref, pos_ref, xs_ref, gs_ref, acc_ref, gacc_ref):
    w = pl.program_id(0)

    @pl.when(vw[w] == 1)
    def _():
        ts = acc_ref.shape[0]
        row_id = jw[w] * ts + lax.broadcasted_iota(jnp.int32, (ts, 1), 0)
        pos = pos_ref[0]
        hit = jnp.where(pos[0:1, :] == row_id, 1.0,
                        jnp.where(pos[1:2, :] == row_id, 1.0, 0.0)).astype(BF16)
        rows = _dot(hit, x_ref[...])
        side = _dot(hit, pc_ref[...])

        @pl.when(fw[w] == 1)
        def _():
            acc_ref[...] = rows
            gacc_ref[...] = side

        @pl.when(fw[w] == 0)
        def _():
            acc_ref[...] += rows
            gacc_ref[...] += side

        @pl.when(lw[w] == 1)
        def _():
            xs_ref[...] = acc_ref[...].astype(xs_ref.dtype)
            side_all = gacc_ref[...]
            lane = lax.broadcasted_iota(jnp.int32, side_all.shape, 1)
            lane_sum = lambda keep: jnp.sum(jnp.where(keep, side_all, 0.0), -1, keepdims=True)
            weight1 = lane_sum(lane < 3)
            weight2 = lane_sum((lane >= 3) & (lane < 6))
            expert1 = lane_sum(lane == P_E1)
            gate = jnp.where(expert1 == ew[w].astype(F32), weight1, weight2)
            gs_ref[...] = jnp.broadcast_to(gate, gs_ref.shape)


def _dispatch(xb, pieces, pos_rows, disp, n_rows, n_work, tc, ts):
    T, D = xb.shape
    by_chunk = lambda w, jw, cw, vw, fw, lw, ew: (cw[w], 0)
    by_sub = lambda w, jw, cw, vw, fw, lw, ew: (jw[w], 0)
    grid_spec = pltpu.PrefetchScalarGridSpec(
        num_scalar_prefetch=6,
        grid=(n_work,),
        in_specs=[
            pl.BlockSpec((tc, D), by_chunk),
            pl.BlockSpec((tc, LANES), by_chunk),
            pl.BlockSpec((1, 8, tc), lambda w, jw, cw, vw, fw, lw, ew: (cw[w], 0, 0)),
        ],
        out_specs=[pl.BlockSpec((ts, D), by_sub), pl.BlockSpec((ts, LANES), by_sub)],
        scratch_shapes=[pltpu.VMEM((ts, D), F32), pltpu.VMEM((ts, LANES), F32)],
    )
    return pl.pallas_call(
        _dispatch_kernel,
        grid_spec=grid_spec,
        out_shape=[jax.ShapeDtypeStruct((n_rows, D), BF16), jax.ShapeDtypeStruct((n_rows, LANES), F32)],
        compiler_params=_params("arbitrary"),
    )(*disp, xb, pieces, pos_rows)


def _moe_kernel(te, tv, xs_ref, gs_ref, wa_ref, wb_ref, wo_ref, y_ref, acc_ref):
    i = pl.program_id(0)
    f = pl.program_id(1)
    last_f = pl.num_programs(1) - 1

    @pl.when(tv[i] == 1)
    def _():
        xs = xs_ref[...]
        h = (_silu(_dot(xs, wa_ref[0])) * _dot(xs, wb_ref[0]) * gs_ref[:, 0:1]).astype(BF16)
        contrib = _dot(h, wo_ref[0])

        @pl.when(f == 0)
        def _():
            acc_ref[...] = contrib

        @pl.when(f > 0)
        def _():
            acc_ref[...] += contrib

        @pl.when(f == last_f)
        def _():
            y_ref[...] = acc_ref[...].astype(y_ref.dtype)

    @pl.when((tv[i] == 0) & (f == last_f))
    def _():
        y_ref[...] = jnp.zeros_like(y_ref)


def _moe(xs, gs, tile_expert, tile_valid, w_in, w_out, tm, tf):
    n_rows, D = xs.shape
    E, F, _ = w_out.shape
    nf = F // tf
    f_of = lambda i, f, te, tv: jnp.where(tv[i] == 1, f, nf - 1)
    grid_spec = pltpu.PrefetchScalarGridSpec(
        num_scalar_prefetch=2,
        grid=(n_rows // tm, nf),
        in_specs=[
            pl.BlockSpec((tm, D), lambda i, f, te, tv: (i, 0)),
            pl.BlockSpec((tm, LANES), lambda i, f, te, tv: (i, 0)),
            pl.BlockSpec((1, D, tf), lambda i, f, te, tv: (te[i], 0, f_of(i, f, te, tv))),
            pl.BlockSpec((1, D, tf), lambda i, f, te, tv: (te[i], 0, nf + f_of(i, f, te, tv))),
            pl.BlockSpec((1, tf, D), lambda i, f, te, tv: (te[i], f_of(i, f, te, tv), 0)),
        ],
        out_specs=pl.BlockSpec((tm, D), lambda i, f, te, tv: (i, 0)),
        scratch_shapes=[pltpu.VMEM((tm, D), F32)],
    )
    return pl.pallas_call(
        _moe_kernel,
        grid_spec=grid_spec,
        out_shape=jax.ShapeDtypeStruct((n_rows, D), BF16),
        compiler_params=_params("parallel", "arbitrary"),
    )(tile_expert, tile_valid, xs, gs, w_in, w_in, w_out)


def _combine_kernel(cj, cc, cv, cf, cl, y_ref, pos_ref, x_ref, g_ref, b_ref, o_ref, acc_ref):
    w = pl.program_id(0)

    @pl.when(cv[w] == 1)
    def _():
        ts = y_ref.shape[0]
        col_id = cj[w] * ts + lax.broadcasted_iota(jnp.int32, (1, ts), 1)
        pos = pos_ref[...]
        hit = (jnp.where(pos[:, 0:1] == col_id, 1.0, 0.0).astype(BF16)
               + jnp.where(pos[:, 1:2] == col_id, 1.0, 0.0).astype(BF16))
        contrib = _dot(hit, y_ref[...])

        @pl.when(cf[w] == 1)
        def _():
            acc_ref[...] = contrib

        @pl.when(cf[w] == 0)
        def _():
            acc_ref[...] += contrib

        @pl.when(cl[w] == 1)
        def _():
            o_ref[...] = _layer_norm(DN_ALPHA * x_ref[...] + acc_ref[...], g_ref[...], b_ref[...])


def _combine(y, pos_cols, x, g, b, comb, n_work, tc, ts):
    T, D = x.shape
    grid_spec = pltpu.PrefetchScalarGridSpec(
        num_scalar_prefetch=5,
        grid=(n_work,),
        in_specs=[
            pl.BlockSpec((ts, D), lambda w, cj, cc, cv, cf, cl: (cj[w], 0)),
            pl.BlockSpec((tc, LANES), lambda w, cj, cc, cv, cf, cl: (cc[w], 0)),
            pl.BlockSpec((tc, D), lambda w, cj, cc, cv, cf, cl: (cc[w], 0)),
            pl.BlockSpec((1, D), lambda w, cj, cc, cv, cf, cl: (0, 0)),
            pl.BlockSpec((1, D), lambda w, cj, cc, cv, cf, cl: (0, 0)),
        ],
        out_specs=pl.BlockSpec((tc, D), lambda w, cj, cc, cv, cf, cl: (cc[w], 0)),
        scratch_shapes=[pltpu.VMEM((tc, D), F32)],
    )
    return pl.pallas_call(
        _combine_kernel,
        grid_spec=grid_spec,
        out_shape=jax.ShapeDtypeStruct((T, D), F32),
        compiler_params=_params("arbitrary"),
    )(*comb, y, pos_cols, x, g.reshape(1, D), b.reshape(1, D))


def _moe_layer(x, xb, w_router, w_in, w_out, g, b, tc, ts, tm, tf):
    T, D = x.shape
    wr = jnp.pad(w_router, ((0, 0), (0, LANES - N_EXPERTS)))
    wr_hi = wr.astype(BF16)
    wr_lo = (wr - wr_hi.astype(F32)).astype(BF16)
    route, pieces, before, total = _router(x, wr_hi, wr_lo, tc)
    row_off, tile_expert, tile_valid, disp, comb, n_rows, n_work = _moe_plan(before, total, T, tc, ts, tm)
    e1 = route[:, R_E1].astype(jnp.int32)
    e2 = route[:, R_E2].astype(jnp.int32)
    pos1 = row_off[e1] + route[:, R_RANK1].astype(jnp.int32)
    pos2 = row_off[e2] + route[:, R_RANK2].astype(jnp.int32)
    pos = jnp.stack([pos1, pos2])
    pos_rows = jnp.pad(pos.reshape(2, T // tc, tc).transpose(1, 0, 2), ((0, 0), (0, 6), (0, 0)),
                       constant_values=-1)
    pos_cols = jnp.pad(pos.T, ((0, 0), (0, LANES - 2)), constant_values=-1)
    xs, gs = _dispatch(xb, pieces, pos_rows, disp, n_rows, n_work, tc, ts)
    y = _moe(xs, gs, tile_expert, tile_valid, w_in, w_out, tm, tf)
    return _combine(y, pos_cols, x, g, b, comb, n_work, tc, ts)


def kernel(x, ret_w_in, ret_gn_g, ret_w_out, nsa_w_kv, cmp_k_pe, cmp_k_w1, cmp_k_w2, cmp_v_pe, cmp_v_w1, cmp_v_w2,
           nsa_w_q, nsa_w_out, ffn_w_in, ffn_w_out, moe_router, moe_w_in, moe_w_out, ln_g, ln_b):
    B, S, D = x.shape
    T = B * S
    G, R, dh = NSA_GROUPS, NSA_REP, NSA_HD
    assert ret_w_in.shape[0] == 1 and nsa_w_q.shape[0] == 1 and ln_g.shape[0] == DEPTH
    dk = D // RET_HEADS
    dv = 2 * dk
    n_slc = S // SLC_LEN
    tq = TILES["nsa_queries"]
    assert n_slc <= dh and S % tq == 0 and WIN % tq == 0 and R * 3 <= GATE_ROWS
    rows = lambda name: min(TILES[name], S)

    xf = x.reshape(T, D)

    cos_r, sin_r = _rope_tables(S, dk, dk)
    qkvg = _ret_proj(xf, ret_w_in[0].astype(BF16), cos_r[:, :dk // 2], -sin_r[:, :dk // 2], S, dk,
                     tm=rows("ret_proj"))
    y = _retention(qkvg, ret_gn_g[0], B, S, dk, dv, rows=rows("retention"))
    x1, x1b = _proj_ln(y, ret_w_out[0].astype(BF16), xf, ln_g[0, 0], ln_b[0, 0], tm=rows("proj_ln"))
    x2, x2b = _ffn(x1b, ffn_w_in[0].astype(BF16), ffn_w_out[0].astype(BF16), x1, ln_g[0, 1], ln_b[0, 1],
                   tm=rows("ffn"), tf=TILES["ffn_hidden"])

    cos_n, sin_n = _rope_tables(S, dh, LANES)
    ks_aug, vs, kw, vw, cmp_kv = _kv_proj(x2b, nsa_w_kv.astype(BF16), cos_n, sin_n, S, tm=rows("kv_proj"))
    n_c = S // CMP_STRIDE
    cmp_in = cmp_kv.reshape(2, G * B * n_c, CMP_STRIDE * dh)
    w1 = jnp.stack([cmp_k_w1, cmp_v_w1]).astype(BF16)
    w2 = jnp.stack([cmp_k_w2, cmp_v_w2]).astype(BF16)
    pe = jnp.stack([cmp_k_pe, cmp_v_pe]).reshape(2, 1, CMP_LEN * dh)
    pe = jnp.broadcast_to(pe, (2, 8, CMP_LEN * dh)).astype(BF16)
    kvc = _compress(cmp_in, w1, w2, pe, tm=min(TILES["compress"], G * B * n_c)).reshape(2, G * B, n_c, dh)
    n_blk = -(-n_slc // 32) * 32
    cs = jnp.arange(n_c) * CMP_STRIDE
    ss = jnp.arange(n_blk) * SLC_LEN
    overlap_t = jnp.clip(jnp.minimum(cs[None, :] + CMP_LEN, ss[:, None] + SLC_LEN)
                         - jnp.maximum(cs[None, :], ss[:, None]), 0).astype(F32) / CMP_STRIDE
    overlap_t = jnp.where(jnp.arange(n_blk)[:, None] < n_slc, overlap_t, 0.0).astype(BF16)

    nq_cols = NSA_HEADS * dh
    wq = nsa_w_q[0]
    wq_t = wq[:, :nq_cols].T.astype(BF16)
    wg_t = jnp.pad(wq[:, nq_cols:].T.reshape(G, R * 3, D), ((0, 0), (0, GATE_ROWS - R * 3), (0, 0)))
    wg_t = wg_t.reshape(G * GATE_ROWS, D).astype(BF16)
    ang = _rope_angles(S, dh).T
    q_t, gates_t = _q_proj(x2b, wq_t, wg_t, jnp.cos(ang), jnp.sin(ang), S, tm=rows("q_proj"))
    attn_t = _nsa_attention(q_t, gates_t, kvc[0], kvc[1], ks_aug, vs, kw, vw, overlap_t, B, S, tq)
    x3, x3b = _proj_ln(attn_t, nsa_w_out[0].astype(BF16), x2, ln_g[1, 0], ln_b[1, 0], tm=rows("proj_ln"),
                       y_transposed=True)

    out = _moe_layer(x3, x3b, moe_router[0], moe_w_in[0].astype(BF16), moe_w_out[0].astype(BF16),
                     ln_g[1, 1], ln_b[1, 1], tc=rows("moe_chunk"), ts=TILES["moe_sorted_rows"],
                     tm=TILES["moe_expert_rows"], tf=TILES["moe_hidden"])
    return out.reshape(B, S, D)
```

```python
import functools

import jax
import jax.numpy as jnp
from jax import lax
from jax.experimental import pallas as pl
from jax.experimental.pallas import tpu as pltpu

F32 = jnp.float32
BF16 = jnp.bfloat16

DEPTH = 2
ROPE_THETA = 10000.0
LN_EPS = 1e-5
DN_ALPHA = (2.0 * DEPTH) ** 0.25

RET_HEADS = 4
RET_CHUNK = 128

NSA_HEADS = 16
NSA_GROUPS = 4
NSA_REP = NSA_HEADS // NSA_GROUPS
NSA_HD = 64
CMP_LEN = 32
CMP_STRIDE = 16
SLC_LEN = 64
SLC_TOPK = 8
WIN = 512
FORCE_SCORE = 1e9

N_EXPERTS = 8

LANES = 128
GATE_ROWS = 16
MASK_NEG = -1e30
LOG2_E = 1.4426950408889634
VMEM_LIMIT = 56 * 1024 * 1024

TILES = {
    "ret_proj": 512, "retention": 512, "proj_ln": 1024, "ffn": 512, "ffn_hidden": 1408,
    "kv_proj": 512, "q_proj": 512, "compress": 1024, "nsa_queries": 256,
    "moe_chunk": 1024, "moe_sorted_rows": 256, "moe_expert_rows": 1024, "moe_hidden": 896,
}


def _params(*sem):
    return pltpu.CompilerParams(dimension_semantics=sem, vmem_limit_bytes=VMEM_LIMIT)


def _dot(a, b):
    return jnp.dot(a, b, preferred_element_type=F32)


def _dot_nt(a, b):
    return lax.dot_general(a, b, (((1,), (1,)), ((), ())), preferred_element_type=F32)


def _dot_tn(a, b):
    return lax.dot_general(a, b, (((0,), (0,)), ((), ())), preferred_element_type=F32)


def _layer_norm(z, g, b):
    mu = jnp.mean(z, -1, keepdims=True)
    zc = z - mu
    var = jnp.mean(zc * zc, -1, keepdims=True)
    return zc * lax.rsqrt(var + LN_EPS) * g + b


def _silu(a):
    return a * jax.nn.sigmoid(a)


def _rope_angles(seq, dim):
    inv = 1.0 / (ROPE_THETA ** (jnp.arange(0, dim, 2, dtype=F32) / dim))
    return jnp.arange(seq, dtype=F32)[:, None] * inv[None, :]


def _rope_tables(seq, dim, width):
    ang = _rope_angles(seq, dim)
    cos, sin = jnp.cos(ang), jnp.sin(ang)
    cos_h = jnp.concatenate([cos, cos], -1)
    sin_h = jnp.concatenate([-sin, sin], -1)
    rep = width // dim
    return jnp.tile(cos_h, (1, rep)), jnp.tile(sin_h, (1, rep))


def _ret_proj_kernel(x_ref, w_ref, cos_ref, sin_ref, o_ref, *, dk, tn):
    x = x_ref[...].astype(BF16)
    half = dk // 2
    for j in range(w_ref.shape[1] // tn):
        acc = _dot(x, w_ref[:, j * tn:(j + 1) * tn])
        if j < 2:
            scale = dk ** -0.5 if j == 1 else 1.0
            cos = cos_ref[...] * scale
            sin = sin_ref[...] * scale
            for h in range(tn // dk):
                a = acc[:, h * dk:h * dk + half]
                b = acc[:, h * dk + half:(h + 1) * dk]
                lo = j * tn + h * dk
                o_ref[:, lo:lo + half] = (a * cos - b * sin).astype(o_ref.dtype)
                o_ref[:, lo + half:lo + dk] = (a * sin + b * cos).astype(o_ref.dtype)
        else:
            o_ref[:, j * tn:(j + 1) * tn] = acc.astype(o_ref.dtype)


def _ret_proj(xb, w, cos, sin, seq, dk, tm):
    T, D = xb.shape
    N = w.shape[1]
    tn = RET_HEADS * dk
    half = dk // 2
    n_pos = seq // tm
    return pl.pallas_call(
        functools.partial(_ret_proj_kernel, dk=dk, tn=tn),
        grid=(T // tm,),
        in_specs=[
            pl.BlockSpec((tm, D), lambda i: (i, 0)),
            pl.BlockSpec((D, N), lambda i: (0, 0)),
            pl.BlockSpec((tm, half), lambda i: (i % n_pos, 0)),
            pl.BlockSpec((tm, half), lambda i: (i % n_pos, 0)),
        ],
        out_specs=pl.BlockSpec((tm, N), lambda i: (i, 0)),
        out_shape=jax.ShapeDtypeStruct((T, N), BF16),
        compiler_params=_params("parallel"),
    )(xb, w, cos, sin)


def _retention_kernel(q_ref, k_ref, v_ref, g_ref, dec_ref, qd_ref, kd_ref, gn_ref, o_ref, state_ref,
                      *, chunk, n_chunks):
    @pl.when(pl.program_id(2) == 0)
    def _():
        state_ref[...] = jnp.zeros_like(state_ref)

    decay = dec_ref[0]
    qd = qd_ref[0]
    kd = kd_ref[0]
    cd = qd[chunk - 1:chunk, :]
    gn = gn_ref[0]
    for c in range(n_chunks):
        rows = slice(c * chunk, (c + 1) * chunk)
        for u in range(q_ref.shape[0]):
            q = q_ref[u, rows, :]
            k = k_ref[u, rows, :]
            v = v_ref[u, rows, :]
            scores = _dot_nt(q, k) * decay
            inner = _dot(scores.astype(BF16), v)
            state = state_ref[u]
            cross = _dot(q, state.astype(BF16)) * qd
            out = inner + cross
            k_dec = (k.astype(F32) * kd).astype(BF16)
            state_ref[u] = state * cd + _dot_tn(k_dec, v)
            mu = jnp.mean(out, -1, keepdims=True)
            oc = out - mu
            var = jnp.mean(oc * oc, -1, keepdims=True)
            normed = oc * lax.rsqrt(var + LN_EPS) * gn
            o_ref[u, rows, :] = (_silu(g_ref[u, rows, :].astype(F32)) * normed).astype(o_ref.dtype)


def _retention(qkvg, gn_g, batch, seq, dk, dv, rows):
    H, C = RET_HEADS, RET_CHUNK
    T = batch * seq
    n_steps = seq // rows
    nb = 2 if batch % 2 == 0 else 1
    qkvg = qkvg.reshape(batch, seq, qkvg.shape[1])
    log_gamma = jnp.log1p(-(2.0 ** (-5.0 - jnp.arange(H, dtype=F32))))
    i = jnp.arange(C, dtype=F32)
    rel = i[:, None] - i[None, :]
    intra = jnp.where(rel >= 0, jnp.exp(log_gamma[:, None, None] * jnp.maximum(rel, 0.0)), 0.0)
    q_decay = jnp.exp(log_gamma[:, None] * (i + 1.0))[:, :, None]
    k_decay = jnp.exp(log_gamma[:, None] * (C - 1.0 - i))[:, :, None]
    v_off = (2 * H * dk) // dv
    g_off = (2 * H * dk + H * dv) // dv
    y = pl.pallas_call(
        functools.partial(_retention_kernel, chunk=C, n_chunks=rows // C),
        grid=(batch // nb, H, n_steps),
        in_specs=[
            pl.BlockSpec((nb, rows, dk), lambda b, h, n: (b, n, h)),
            pl.BlockSpec((nb, rows, dk), lambda b, h, n: (b, n, H + h)),
            pl.BlockSpec((nb, rows, dv), lambda b, h, n: (b, n, v_off + h)),
            pl.BlockSpec((nb, rows, dv), lambda b, h, n: (b, n, g_off + h)),
            pl.BlockSpec((1, C, C), lambda b, h, n: (h, 0, 0)),
            pl.BlockSpec((1, C, 1), lambda b, h, n: (h, 0, 0)),
            pl.BlockSpec((1, C, 1), lambda b, h, n: (h, 0, 0)),
            pl.BlockSpec((1, 1, dv), lambda b, h, n: (h, 0, 0)),
        ],
        out_specs=pl.BlockSpec((nb, rows, dv), lambda b, h, n: (b, n, h)),
        out_shape=jax.ShapeDtypeStruct((batch, seq, H * dv), BF16),
        scratch_shapes=[pltpu.VMEM((nb, dk, dv), F32)],
        compiler_params=_params("parallel", "parallel", "arbitrary"),
    )(qkvg, qkvg, qkvg, qkvg, intra, q_decay, k_decay, gn_g.reshape(H, 1, dv))
    return y.reshape(T, H * dv)


def _proj_ln_kernel(y_ref, w_ref, x_ref, g_ref, b_ref, o_ref, ob_ref, *, y_transposed, ob_transposed):
    mix = _dot_tn(y_ref[...], w_ref[...]) if y_transposed else _dot(y_ref[...], w_ref[...])
    o = _layer_norm(DN_ALPHA * x_ref[...] + mix, g_ref[...], b_ref[...])
    o_ref[...] = o
    ob_ref[...] = (o.T if ob_transposed else o).astype(BF16)


def _proj_ln(y, w, x, g, b, tm, y_transposed=False, ob_transposed=False):
    K, D = w.shape
    T = x.shape[0]
    y_spec = pl.BlockSpec((K, tm), lambda i: (0, i)) if y_transposed else pl.BlockSpec((tm, K), lambda i: (i, 0))
    ob_spec = pl.BlockSpec((D, tm), lambda i: (0, i)) if ob_transposed else pl.BlockSpec((tm, D), lambda i: (i, 0))
    ob_shape = (D, T) if ob_transposed else (T, D)
    return pl.pallas_call(
        functools.partial(_proj_ln_kernel, y_transposed=y_transposed, ob_transposed=ob_transposed),
        grid=(T // tm,),
        in_specs=[
            y_spec,
            pl.BlockSpec((K, D), lambda i: (0, 0)),
            pl.BlockSpec((tm, D), lambda i: (i, 0)),
            pl.BlockSpec((1, D), lambda i: (0, 0)),
            pl.BlockSpec((1, D), lambda i: (0, 0)),
        ],
        out_specs=[pl.BlockSpec((tm, D), lambda i: (i, 0)), ob_spec],
        out_shape=[jax.ShapeDtypeStruct((T, D), F32), jax.ShapeDtypeStruct(ob_shape, BF16)],
        compiler_params=_params("parallel"),
    )(y, w, x, g.reshape(1, D), b.reshape(1, D))


def _ffn_kernel(xb_ref, wi_ref, wo_ref, x_ref, g_ref, b_ref, o_ref, ob_ref, *, tf):
    xb = xb_ref[...]
    F = wo_ref.shape[0]
    acc = None
    for f in range(F // tf):
        cols = slice(f * tf, (f + 1) * tf)
        gate_cols = slice(F + f * tf, F + (f + 1) * tf)
        h = (_silu(_dot(xb, wi_ref[:, cols])) * _dot(xb, wi_ref[:, gate_cols])).astype(BF16)
        contrib = _dot(h, wo_ref[cols, :])
        acc = contrib if acc is None else acc + contrib
    o = _layer_norm(DN_ALPHA * x_ref[...] + acc, g_ref[...], b_ref[...])
    o_ref[...] = o
    ob_ref[...] = o.astype(BF16)


def _ffn(xb, w_in, w_out, x, g, b, tm, tf):
    T, D = xb.shape
    F = w_out.shape[0]
    assert F % tf == 0
    return pl.pallas_call(
        functools.partial(_ffn_kernel, tf=tf),
        grid=(T // tm,),
        in_specs=[
            pl.BlockSpec((tm, D), lambda i: (i, 0)),
            pl.BlockSpec((D, 2 * F), lambda i: (0, 0)),
            pl.BlockSpec((F, D), lambda i: (0, 0)),
            pl.BlockSpec((tm, D), lambda i: (i, 0)),
            pl.BlockSpec((1, D), lambda i: (0, 0)),
            pl.BlockSpec((1, D), lambda i: (0, 0)),
        ],
        out_specs=[pl.BlockSpec((tm, D), lambda i: (i, 0)), pl.BlockSpec((tm, D), lambda i: (i, 0))],
        out_shape=[jax.ShapeDtypeStruct((T, D), F32), jax.ShapeDtypeStruct((T, D), BF16)],
        compiler_params=_params("parallel"),
    )(xb, w_in, w_out, x, g.reshape(1, D), b.reshape(1, D))


def _rope64(acc, cos, sin_signed):
    half = NSA_HD // 2
    lane = lax.broadcasted_iota(jnp.int32, acc.shape, 1)
    first = (lane % NSA_HD) < half
    rot = jnp.where(first, pltpu.roll(acc, LANES - half, 1), pltpu.roll(acc, half, 1))
    return acc * cos + rot * sin_signed


def _kv_proj_kernel(x_ref, w_ref, cos_ref, sin_ref, ks_ref, vs_ref, kw_ref, vw_ref, cmp_ref, *, n_pos):
    G, dh = NSA_GROUPS, NSA_HD
    acc = _dot(x_ref[...], w_ref[...])
    tm = acc.shape[0]
    cos, sin = cos_ref[...], sin_ref[...]
    lane = lax.broadcasted_iota(jnp.int32, (tm, LANES), 1)
    low = lane < dh
    pos = (pl.program_id(0) % n_pos) * tm + lax.broadcasted_iota(jnp.int32, (tm, LANES), 0)
    block_onehot = jnp.where(lane - dh == pos // SLC_LEN, 1.0, 0.0)
    ones_lane = jnp.where(lane == dh, 1.0, 0.0)
    swap = lambda a: pltpu.roll(a, dh, 1)

    def slab(branch, s):
        off = branch * G * dh + s * LANES
        return acc[:, off:off + LANES]

    for s in range(G * dh // LANES):
        k_cmp, v_cmp = _rope64(slab(0, s), cos, sin), slab(1, s)
        k_slc, v_slc = _rope64(slab(2, s), cos, sin), slab(3, s)
        k_win, v_win = _rope64(slab(4, s), cos, sin), slab(5, s)
        for h in range(2):
            g = 2 * s + h
            to_low = (lambda a: a) if h == 0 else swap
            ks_ref[g] = jnp.where(low, to_low(k_slc), block_onehot).astype(BF16)
            vs_ref[g] = jnp.where(low, to_low(v_slc), ones_lane).astype(BF16)
            kw_ref[g] = jnp.where(low, to_low(k_win), 0.0).astype(BF16)
            vw_ref[g] = jnp.where(low, to_low(v_win), ones_lane).astype(BF16)
            cmp_ref[0, g] = to_low(k_cmp)[:, :dh].astype(BF16)
            cmp_ref[1, g] = to_low(v_cmp)[:, :dh].astype(BF16)


def _kv_proj(xb, w, cos, sin, seq, tm):
    T, D = xb.shape
    N = w.shape[1]
    G, dh = NSA_GROUPS, NSA_HD
    n_pos = seq // tm
    wide = jax.ShapeDtypeStruct((G, T, LANES), BF16)
    wide_spec = pl.BlockSpec((G, tm, LANES), lambda i: (0, i, 0))
    return pl.pallas_call(
        functools.partial(_kv_proj_kernel, n_pos=n_pos),
        grid=(T // tm,),
        in_specs=[
            pl.BlockSpec((tm, D), lambda i: (i, 0)),
            pl.BlockSpec((D, N), lambda i: (0, 0)),
            pl.BlockSpec((tm, LANES), lambda i: (i % n_pos, 0)),
            pl.BlockSpec((tm, LANES), lambda i: (i % n_pos, 0)),
        ],
        out_specs=[wide_spec, wide_spec, wide_spec, wide_spec, pl.BlockSpec((2, G, tm, dh), lambda i: (0, 0, i, 0))],
        out_shape=[wide, wide, wide, wide, jax.ShapeDtypeStruct((2, G, T, dh), BF16)],
        compiler_params=_params("parallel"),
    )(xb, w, cos, sin)


def _q_proj_kernel(x_ref, wq_ref, wg_ref, cos_ref, sin_ref, q_ref, gate_ref):
    dh = NSA_HD
    half = dh // 2
    x = x_ref[...]
    acc = _dot_nt(wq_ref[...], x)
    cos = cos_ref[...] * (dh ** -0.5 * LOG2_E)
    sin = sin_ref[...] * (dh ** -0.5 * LOG2_E)
    for h in range(acc.shape[0] // dh):
        a = acc[h * dh:h * dh + half]
        b = acc[h * dh + half:(h + 1) * dh]
        q_ref[h * dh:h * dh + half, :] = (a * cos - b * sin).astype(q_ref.dtype)
        q_ref[h * dh + half:(h + 1) * dh, :] = (a * sin + b * cos).astype(q_ref.dtype)
    gate_ref[...] = jax.nn.sigmoid(_dot_nt(wg_ref[...], x))


def _q_proj(xb, wq_t, wg_t, cos_t, sin_t, seq, tm):
    T, D = xb.shape
    nq, ng = wq_t.shape[0], wg_t.shape[0]
    n_pos = seq // tm
    half = NSA_HD // 2
    return pl.pallas_call(
        _q_proj_kernel,
        grid=(T // tm,),
        in_specs=[
            pl.BlockSpec((tm, D), lambda i: (i, 0)),
            pl.BlockSpec((nq, D), lambda i: (0, 0)),
            pl.BlockSpec((ng, D), lambda i: (0, 0)),
            pl.BlockSpec((half, tm), lambda i: (0, i % n_pos)),
            pl.BlockSpec((half, tm), lambda i: (0, i % n_pos)),
        ],
        out_specs=[pl.BlockSpec((nq, tm), lambda i: (0, i)), pl.BlockSpec((ng, tm), lambda i: (0, i))],
        out_shape=[jax.ShapeDtypeStruct((nq, T), BF16), jax.ShapeDtypeStruct((ng, T), F32)],
        compiler_params=_params("parallel"),
    )(xb, wq_t, wg_t, cos_t, sin_t)


def _compress_kernel(a_ref, w1_ref, w2_ref, pe_ref, o_ref):
    a = a_ref[0]
    w1 = w1_ref[0]
    half = w1.shape[0] // 2
    first = _dot(a, w1[:half])
    second = _dot(a, w1[half:])
    bias = _dot(pe_ref[0], w1)[0:1]
    m = a.shape[0]
    hid = first + pltpu.roll(second, m - 1, 0) + bias
    o_ref[0] = _dot(jax.nn.gelu(hid).astype(BF16), w2_ref[0]).astype(o_ref.dtype)


def _compress(a, w1, w2, pe, tm):
    _, rows, width = a.shape
    hid = w1.shape[2]
    dh = w2.shape[2]
    return pl.pallas_call(
        _compress_kernel,
        grid=(2, rows // tm),
        in_specs=[
            pl.BlockSpec((1, tm, width), lambda s, i: (s, i, 0)),
            pl.BlockSpec((1, 2 * width, hid), lambda s, i: (s, 0, 0)),
            pl.BlockSpec((1, hid, dh), lambda s, i: (s, 0, 0)),
            pl.BlockSpec((1, 8, 2 * width), lambda s, i: (s, 0, 0)),
        ],
        out_specs=pl.BlockSpec((1, tm, dh), lambda s, i: (s, i, 0)),
        out_shape=jax.ShapeDtypeStruct((2, rows, dh), BF16),
        compiler_params=_params("parallel", "parallel"),
    )(a, w1, w2, pe)


def _nsa_kernel(q_ref, gt_ref, kc_ref, vc_ref, ks_ref, vs_ref, kw_ref, vw_ref, ov_ref, bc_ref, bl_ref, o_ref,
                *, tq, n_slc, n_top):
    R, dh = NSA_REP, NSA_HD
    i = pl.program_id(2)
    s0 = pl.multiple_of(i * tq, tq)
    q_all = q_ref[...]
    q_heads = [q_all[r * dh:(r + 1) * dh, :] for r in range(R)]
    t_row = s0 + lax.broadcasted_iota(jnp.int32, (1, tq), 1)

    q4 = jnp.concatenate(q_heads, axis=1)
    n_c = kc_ref.shape[1]
    sc = _dot(kc_ref[0], q4)
    t4 = s0 + (lax.broadcasted_iota(jnp.int32, sc.shape, 1) & (tq - 1))
    valid_c = lax.broadcasted_iota(jnp.int32, sc.shape, 0) * CMP_STRIDE + (CMP_LEN - 1) <= t4
    sc = jnp.where(valid_c, sc, -jnp.inf)
    mc = jnp.max(sc, 0, keepdims=True)
    mc = jnp.where(mc == -jnp.inf, 0.0, mc)
    ec = jnp.where(valid_c, jnp.exp2(sc - mc), 0.0)
    p_cmp = ec / jnp.maximum(jnp.sum(ec, 0, keepdims=True), 1e-30)
    o_cmp = _dot_tn(vc_ref[0], p_cmp.astype(BF16))

    p_sum = p_cmp[:, 0:tq]
    for r in range(1, R):
        p_sum = p_sum + p_cmp[:, r * tq:(r + 1) * tq]
    p_hi = p_sum.astype(BF16)
    p_lo = (p_sum - p_hi.astype(F32)).astype(BF16)
    imp = _dot(ov_ref[...], p_hi) + _dot(ov_ref[...], p_lo)
    blk = lax.broadcasted_iota(jnp.int32, imp.shape, 0)
    cur = t_row // SLC_LEN
    forced = (blk == 0) | (blk == cur) | (blk == cur - 1)
    imp = jnp.where(forced, FORCE_SCORE, imp)
    imp = jnp.where(blk <= cur, imp, -jnp.inf)
    rank = jnp.zeros(imp.shape, jnp.int32)
    for b in range(n_slc):
        row = imp[b:b + 1, :]
        wins_tie = jnp.where(blk > b, 1, 0)
        rank = rank + jnp.where(row > imp, 1, jnp.where(row == imp, wins_tie, 0))
    selected = (rank < n_top) & (blk <= cur)
    sel_bias = jnp.where(selected, 0.0, MASK_NEG).astype(BF16)
    if sel_bias.shape[0] < dh:
        sel_bias = jnp.concatenate([sel_bias, jnp.zeros((dh - sel_bias.shape[0], tq), BF16)], axis=0)
    zero_half = jnp.zeros((dh, tq), BF16)
    q_sel = jnp.concatenate([jnp.concatenate([qh, sel_bias], axis=0) for qh in q_heads], axis=1)
    q_win = jnp.concatenate([jnp.concatenate([qh, zero_half], axis=0) for qh in q_heads], axis=1)

    def key_block(ref, blk_idx):
        return ref[0, pl.ds(pl.multiple_of(blk_idx * tq, tq), tq), :]

    def attend(state, k_blk, v_blk, queries, add_bias):
        m, acc = state
        s = _dot(k_blk, queries)
        if add_bias is not None:
            s = add_bias(s)
        m_new = jnp.maximum(m, jnp.max(s, 0, keepdims=True))
        p = jnp.exp2(s - m_new).astype(BF16)
        acc = jnp.exp2(m - m_new) * acc + _dot_tn(v_blk, p)
        return m_new, acc

    init = (jnp.full((1, R * tq), MASK_NEG, F32), jnp.zeros((2 * dh, R * tq), F32))
    causal_bias = bc_ref[...]

    def sel_pair(kb, state):
        k0 = pl.multiple_of(kb * 2 * tq, 2 * tq)
        return attend(state, ks_ref[0, pl.ds(k0, 2 * tq), :], vs_ref[0, pl.ds(k0, 2 * tq), :], q_sel, None)

    st_sel = lax.fori_loop(0, i // 2, sel_pair, init)
    prev_open = jnp.where(i % 2 == 1, 0.0, MASK_NEG)
    prev = jnp.maximum(i - 1, 0)
    k_fin = jnp.concatenate([key_block(ks_ref, prev), key_block(ks_ref, i)], axis=0)
    v_fin = jnp.concatenate([key_block(vs_ref, prev), key_block(vs_ref, i)], axis=0)
    _, acc_sel = attend(
        st_sel, k_fin, v_fin, q_sel,
        lambda s: jnp.concatenate([s[:tq] + prev_open, s[tq:] + causal_bias], axis=0))

    n_back = WIN // tq
    win_blocks = [jnp.maximum(i - back, 0) for back in range(n_back, -1, -1)]
    k_win = jnp.concatenate([key_block(kw_ref, blk_idx) for blk_idx in win_blocks], axis=0)
    v_win = jnp.concatenate([key_block(vw_ref, blk_idx) for blk_idx in win_blocks], axis=0)

    def window_bias(s):
        parts = []
        for n, back in enumerate(range(n_back, -1, -1)):
            blk = s[n * tq:(n + 1) * tq]
            if back == 0:
                parts.append(blk + causal_bias)
            else:
                exists = jnp.where(i >= back, 0.0, MASK_NEG)
                parts.append(blk + (bl_ref[...] + exists) if back == n_back else blk + exists)
        return jnp.concatenate(parts, axis=0)

    _, acc_win = attend(init, k_win, v_win, q_win, window_bias)

    gates = gt_ref[...]
    gate = [jnp.concatenate([gates[3 * r + c:3 * r + c + 1] for r in range(R)], axis=1) for c in range(3)]
    o = (gate[0] * o_cmp + (gate[1] * (1.0 / acc_sel[dh:dh + 1])) * acc_sel[:dh]
         + (gate[2] * (1.0 / acc_win[dh:dh + 1])) * acc_win[:dh])
    for r in range(R):
        o_ref[r * dh:(r + 1) * dh, :] = o[:, r * tq:(r + 1) * tq].astype(o_ref.dtype)


def _nsa_attention(q_t, gates_t, kc, vc, ks_aug, vs, kw, vw, overlap_t, batch, seq, tq):
    G, R, dh = NSA_GROUPS, NSA_REP, NSA_HD
    T = batch * seq
    nq = seq // tq
    n_slc = seq // SLC_LEN
    n_c = kc.shape[1]
    n_blk = overlap_t.shape[0]
    key = lax.broadcasted_iota(jnp.int32, (tq, R * tq), 0)
    qry = lax.broadcasted_iota(jnp.int32, (tq, R * tq), 1) % tq
    causal_bias = jnp.where(key <= qry, 0.0, MASK_NEG).astype(F32)
    lower_bias = jnp.where(key > qry, 0.0, MASK_NEG).astype(F32)
    per_bg = lambda b, g, i: (g * batch + b, 0, 0)
    seq_spec = pl.BlockSpec((1, seq, LANES), lambda b, g, i: (g, b, 0))
    const = lambda b, g, i: (0, 0)
    return pl.pallas_call(
        functools.partial(_nsa_kernel, tq=tq, n_slc=n_slc, n_top=min(SLC_TOPK, n_slc)),
        grid=(batch, G, nq),
        in_specs=[
            pl.BlockSpec((R * dh, tq), lambda b, g, i: (g, b * nq + i)),
            pl.BlockSpec((GATE_ROWS, tq), lambda b, g, i: (g, b * nq + i)),
            pl.BlockSpec((1, n_c, dh), per_bg),
            pl.BlockSpec((1, n_c, dh), per_bg),
            seq_spec, seq_spec, seq_spec, seq_spec,
            pl.BlockSpec((n_blk, n_c), const),
            pl.BlockSpec((tq, R * tq), const),
            pl.BlockSpec((tq, R * tq), const),
        ],
        out_specs=pl.BlockSpec((R * dh, tq), lambda b, g, i: (g, b * nq + i)),
        out_shape=jax.ShapeDtypeStruct((G * R * dh, T), BF16),
        compiler_params=_params("parallel", "parallel", "arbitrary"),
    )(q_t, gates_t, kc, vc, ks_aug, vs, kw, vw, overlap_t, causal_bias, lower_bias)


R_E1, R_E2, R_W1, R_W2, R_RANK1, R_RANK2 = range(6)
P_E1 = 6


def _router_kernel(x_ref, whi_ref, wlo_ref, route_ref, pieces_ref, before_ref, total_ref, cnt_ref):
    @pl.when(pl.program_id(0) == 0)
    def _():
        cnt_ref[...] = jnp.zeros_like(cnt_ref)

    x = x_ref[...]
    x_hi = x.astype(BF16)
    x_lo = (x - x_hi.astype(F32)).astype(BF16)
    logits = _dot(x_hi, whi_ref[...]) + _dot(x_hi, wlo_ref[...]) + _dot(x_lo, whi_ref[...])
    tc = logits.shape[0]
    lane = lax.broadcasted_iota(jnp.int32, logits.shape, 1).astype(F32)
    logits = jnp.where(lane < N_EXPERTS, logits, -jnp.inf)
    m1 = jnp.max(logits, -1, keepdims=True)
    i1 = jnp.min(jnp.where(logits == m1, lane, float(LANES)), -1, keepdims=True)
    rest = jnp.where(lane == i1, -jnp.inf, logits)
    m2 = jnp.max(rest, -1, keepdims=True)
    i2 = jnp.min(jnp.where(rest == m2, lane, float(LANES)), -1, keepdims=True)
    e2 = jnp.exp(m2 - m1)
    den = 1.0 + e2
    chosen = jnp.where(lane == i1, 1.0, jnp.where(lane == i2, 1.0, 0.0))
    earlier = (lax.broadcasted_iota(jnp.int32, (tc, tc), 1) < lax.broadcasted_iota(jnp.int32, (tc, tc), 0))
    before = cnt_ref[0:1, :]
    excl = _dot(jnp.where(earlier, 1.0, 0.0).astype(BF16), chosen.astype(BF16)) + before
    rank1 = jnp.sum(jnp.where(lane == i1, excl, 0.0), -1, keepdims=True)
    rank2 = jnp.sum(jnp.where(lane == i2, excl, 0.0), -1, keepdims=True)
    cols = {R_E1: i1, R_E2: i2, R_W1: 1.0 / den, R_W2: e2 / den, R_RANK1: rank1, R_RANK2: rank2}
    route = jnp.zeros_like(logits)
    for k, v in cols.items():
        route = jnp.where(lane == float(k), v, route)
    route_ref[...] = route
    pieces = jnp.where(lane == float(P_E1), i1, 0.0)
    for slot, weight in enumerate((1.0 / den, e2 / den)):
        rem = weight
        for k in range(3):
            piece = rem.astype(BF16).astype(F32)
            pieces = jnp.where(lane == float(3 * slot + k), piece, pieces)
            rem = rem - piece
    pieces_ref[...] = pieces.T.astype(BF16)
    before_ref[0] = jnp.broadcast_to(before, before_ref.shape[1:])
    after = before + jnp.sum(chosen, axis=0, keepdims=True)
    cnt_ref[...] = jnp.broadcast_to(after, cnt_ref.shape)
    total_ref[...] = jnp.broadcast_to(after, total_ref.shape)


def _router(x, w_hi, w_lo, tc):
    T, D = x.shape
    return pl.pallas_call(
        _router_kernel,
        grid=(T // tc,),
        in_specs=[
            pl.BlockSpec((tc, D), lambda i: (i, 0)),
            pl.BlockSpec((D, LANES), lambda i: (0, 0)),
            pl.BlockSpec((D, LANES), lambda i: (0, 0)),
        ],
        out_specs=[pl.BlockSpec((tc, LANES), lambda i: (i, 0)),
                   pl.BlockSpec((LANES, tc), lambda i: (0, i)),
                   pl.BlockSpec((1, 8, LANES), lambda i: (i, 0, 0)),
                   pl.BlockSpec((8, LANES), lambda i: (0, 0))],
        out_shape=[jax.ShapeDtypeStruct((T, LANES), F32),
                   jax.ShapeDtypeStruct((LANES, T), BF16),
                   jax.ShapeDtypeStruct((T // tc, 8, LANES), F32),
                   jax.ShapeDtypeStruct((8, LANES), F32)],
        scratch_shapes=[pltpu.VMEM((8, LANES), F32)],
        compiler_params=_params("arbitrary"),
    )(x, w_hi, w_lo)


def _moe_plan(before, total, n_tok, tc, ts, tm):
    E = N_EXPERTS
    n_c = n_tok // tc
    n_rows = 2 * n_tok + E * tm
    n_sub, n_tiles = n_rows // ts, n_rows // tm
    n_work = E * n_c + n_sub
    i32 = jnp.int32
    tot = total[0, :E].astype(i32)
    cum = jnp.concatenate([before[:, 0, :E].astype(i32).T, tot[:, None]], axis=1)
    tiles_e = (tot + tm - 1) // tm
    tile_end = jnp.cumsum(tiles_e)
    row_off = (tile_end - tiles_e) * tm
    tile_id = jnp.arange(n_tiles, dtype=i32)
    tile_expert = jnp.minimum(jnp.sum(tile_id[:, None] >= tile_end[None, :], axis=1), E - 1).astype(i32)
    tile_valid = (tile_id < tile_end[-1]).astype(i32)
    sub = jnp.arange(n_sub, dtype=i32)
    sub_e = tile_expert[(sub * ts) // tm]
    rel0 = sub * ts - row_off[sub_e]
    rel1 = rel0 + ts
    cum_s = cum[sub_e]
    c_lo = jnp.sum(cum_s[:, 1:] <= rel0[:, None], axis=1).astype(i32)
    c_hi = jnp.sum(cum_s[:, :-1] < rel1[:, None], axis=1).astype(i32) - 1
    c_lo = jnp.clip(c_lo, 0, n_c - 1)
    c_hi = jnp.clip(c_hi, c_lo, n_c - 1)
    n_j = c_hi - c_lo + 1
    ends = jnp.cumsum(n_j)
    starts = ends - n_j
    n_valid = ends[-1]
    w = jnp.arange(n_work, dtype=i32)
    valid = w < n_valid
    jw = jnp.minimum(jnp.sum(w[:, None] >= ends[None, :], axis=1), n_sub - 1).astype(i32)
    cw = jnp.where(valid, c_lo[jw] + (w - starts[jw]), c_hi[n_sub - 1]).astype(i32)
    first = valid & (w == starts[jw])
    last = valid & (w == ends[jw] - 1)
    disp = (jw, cw, valid.astype(i32), first.astype(i32), last.astype(i32), sub_e[jw].astype(i32))
    order = jnp.argsort(jnp.where(valid, cw * n_sub + jw, n_c * n_sub + w))
    cj, cc, cv = jw[order], cw[order], valid[order]
    last_j, last_c = cj[n_valid - 1], cc[n_valid - 1]
    cj = jnp.where(cv, cj, last_j)
    cc = jnp.where(cv, cc, last_c)
    prev_c = jnp.concatenate([jnp.full((1,), -1, i32), cc[:-1]])
    next_c = jnp.concatenate([cc[1:], jnp.full((1,), -1, i32)])
    next_v = jnp.concatenate([cv[1:], jnp.zeros((1,), bool)])
    cfirst = cv & (cc != prev_c)
    clast = cv & ((cc != next_c) | ~next_v)
    comb = (cj.astype(i32), cc.astype(i32), cv.astype(i32), cfirst.astype(i32), clast.astype(i32))
    return row_off, tile_expert, tile_valid, disp, comb, n_rows, n_work


def _dispatch_kernel(jw, cw, vw, fw, lw, ew, xt_ref, pct_ref, pos_ref, xs_ref, gs_ref, acc_ref, gacc_ref):
    w = pl.program_id(0)

    @pl.when(vw[w] == 1)
    def _():
        ts = xs_ref.shape[0]
        col_id = jw[w] * ts + lax.broadcasted_iota(jnp.int32, (1, ts), 1)
        pos = pos_ref[...]
        hit_t = jnp.where(pos[:, 0:1] == col_id, 1.0,
                          jnp.where(pos[:, 1:2] == col_id, 1.0, 0.0)).astype(BF16)
        rows_t = _dot(xt_ref[...], hit_t)
        side_t = _dot(pct_ref[...], hit_t)

        @pl.when(fw[w] == 1)
        def _():
            acc_ref[...] = rows_t
            gacc_ref[...] = side_t

        @pl.when(fw[w] == 0)
        def _():
            acc_ref[...] += rows_t
            gacc_ref[...] += side_t

        @pl.when(lw[w] == 1)
        def _():
            xs_ref[...] = acc_ref[...].T.astype(xs_ref.dtype)
            side_all = gacc_ref[...].T
            lane = lax.broadcasted_iota(jnp.int32, side_all.shape, 1)
            lane_sum = lambda keep: jnp.sum(jnp.where(keep, side_all, 0.0), -1, keepdims=True)
            weight1 = lane_sum(lane < 3)
            weight2 = lane_sum((lane >= 3) & (lane < 6))
            expert1 = lane_sum(lane == P_E1)
            gate = jnp.where(expert1 == ew[w].astype(F32), weight1, weight2)
            gs_ref[...] = jnp.broadcast_to(gate, gs_ref.shape)


def _dispatch(xb_t, pieces_t, pos_cols, disp, n_rows, n_work, tc, ts):
    D, T = xb_t.shape
    chunk_cols = lambda w, jw, cw, vw, fw, lw, ew: (0, cw[w])
    by_sub = lambda w, jw, cw, vw, fw, lw, ew: (jw[w], 0)
    grid_spec = pltpu.PrefetchScalarGridSpec(
        num_scalar_prefetch=6,
        grid=(n_work,),
        in_specs=[
            pl.BlockSpec((D, tc), chunk_cols),
            pl.BlockSpec((LANES, tc), chunk_cols),
            pl.BlockSpec((tc, LANES), lambda w, jw, cw, vw, fw, lw, ew: (cw[w], 0)),
        ],
        out_specs=[pl.BlockSpec((ts, D), by_sub), pl.BlockSpec((ts, LANES), by_sub)],
        scratch_shapes=[pltpu.VMEM((D, ts), F32), pltpu.VMEM((LANES, ts), F32)],
    )
    return pl.pallas_call(
        _dispatch_kernel,
        grid_spec=grid_spec,
        out_shape=[jax.ShapeDtypeStruct((n_rows, D), BF16), jax.ShapeDtypeStruct((n_rows, LANES), F32)],
        compiler_params=_params("arbitrary"),
    )(*disp, xb_t, pieces_t, pos_cols)


def _moe_kernel(te, tv, xs_ref, gs_ref, wa_ref, wb_ref, wo_ref, y_ref, acc_ref):
    i = pl.program_id(0)
    f = pl.program_id(1)
    last_f = pl.num_programs(1) - 1

    @pl.when(tv[i] == 1)
    def _():
        xs = xs_ref[...]
        h = (_silu(_dot(xs, wa_ref[0])) * _dot(xs, wb_ref[0]) * gs_ref[:, 0:1]).astype(BF16)
        contrib = _dot(h, wo_ref[0])

        @pl.when(f == 0)
        def _():
            acc_ref[...] = contrib

        @pl.when(f > 0)
        def _():
            acc_ref[...] += contrib

        @pl.when(f == last_f)
        def _():
            y_ref[...] = acc_ref[...].astype(y_ref.dtype)

    @pl.when((tv[i] == 0) & (f == last_f))
    def _():
        y_ref[...] = jnp.zeros_like(y_ref)


def _moe(xs, gs, tile_expert, tile_valid, w_in, w_out, tm, tf):
    n_rows, D = xs.shape
    E, F, _ = w_out.shape
    nf = F // tf
    f_of = lambda i, f, te, tv: jnp.where(tv[i] == 1, f, nf - 1)
    grid_spec = pltpu.PrefetchScalarGridSpec(
        num_scalar_prefetch=2,
        grid=(n_rows // tm, nf),
        in_specs=[
            pl.BlockSpec((tm, D), lambda i, f, te, tv: (i, 0)),
            pl.BlockSpec((tm, LANES), lambda i, f, te, tv: (i, 0)),
            pl.BlockSpec((1, D, tf), lambda i, f, te, tv: (te[i], 0, f_of(i, f, te, tv))),
            pl.BlockSpec((1, D, tf), lambda i, f, te, tv: (te[i], 0, nf + f_of(i, f, te, tv))),
            pl.BlockSpec((1, tf, D), lambda i, f, te, tv: (te[i], f_of(i, f, te, tv), 0)),
        ],
        out_specs=pl.BlockSpec((tm, D), lambda i, f, te, tv: (i, 0)),
        scratch_shapes=[pltpu.VMEM((tm, D), F32)],
    )
    return pl.pallas_call(
        _moe_kernel,
        grid_spec=grid_spec,
        out_shape=jax.ShapeDtypeStruct((n_rows, D), BF16),
        compiler_params=_params("parallel", "arbitrary"),
    )(tile_expert, tile_valid, xs, gs, w_in, w_in, w_out)


def _combine_kernel(cj, cc, cv, cf, cl, y_ref, pos_ref, x_ref, g_ref, b_ref, o_ref, acc_ref):
    w = pl.program_id(0)

    @pl.when(cv[w] == 1)
    def _():
        ts = y_ref.shape[0]
        col_id = cj[w] * ts + lax.broadcasted_iota(jnp.int32, (1, ts), 1)
        pos = pos_ref[...]
        hit = (jnp.where(pos[:, 0:1] == col_id, 1.0, 0.0).astype(BF16)
               + jnp.where(pos[:, 1:2] == col_id, 1.0, 0.0).astype(BF16))
        contrib = _dot(hit, y_ref[...])

        @pl.when(cf[w] == 1)
        def _():
            acc_ref[...] = contrib

        @pl.when(cf[w] == 0)
        def _():
            acc_ref[...] += contrib

        @pl.when(cl[w] == 1)
        def _():
            o_ref[...] = _layer_norm(DN_ALPHA * x_ref[...] + acc_ref[...], g_ref[...], b_ref[...])


def _combine(y, pos_cols, x, g, b, comb, n_work, tc, ts):
    T, D = x.shape
    grid_spec = pltpu.PrefetchScalarGridSpec(
        num_scalar_prefetch=5,
        grid=(n_work,),
        in_specs=[
            pl.BlockSpec((ts, D), lambda w, cj, cc, cv, cf, cl: (cj[w], 0)),
            pl.BlockSpec((tc, LANES), lambda w, cj, cc, cv, cf, cl: (cc[w], 0)),
            pl.BlockSpec((tc, D), lambda w, cj, cc, cv, cf, cl: (cc[w], 0)),
            pl.BlockSpec((1, D), lambda w, cj, cc, cv, cf, cl: (0, 0)),
            pl.BlockSpec((1, D), lambda w, cj, cc, cv, cf, cl: (0, 0)),
        ],
        out_specs=pl.BlockSpec((tc, D), lambda w, cj, cc, cv, cf, cl: (cc[w], 0)),
        scratch_shapes=[pltpu.VMEM((tc, D), F32)],
    )
    return pl.pallas_call(
        _combine_kernel,
        grid_spec=grid_spec,
        out_shape=jax.ShapeDtypeStruct((T, D), F32),
        compiler_params=_params("arbitrary"),
    )(*comb, y, pos_cols, x, g.reshape(1, D), b.reshape(1, D))


def _moe_layer(x, xb_t, w_router, w_in, w_out, g, b, tc, ts, tm, tf):
    T, D = x.shape
    wr = jnp.pad(w_router, ((0, 0), (0, LANES - N_EXPERTS)))
    wr_hi = wr.astype(BF16)
    wr_lo = (wr - wr_hi.astype(F32)).astype(BF16)
    route, pieces, before, total = _router(x, wr_hi, wr_lo, tc)
    row_off, tile_expert, tile_valid, disp, comb, n_rows, n_work = _moe_plan(before, total, T, tc, ts, tm)
    e1 = route[:, R_E1].astype(jnp.int32)
    e2 = route[:, R_E2].astype(jnp.int32)
    pos1 = row_off[e1] + route[:, R_RANK1].astype(jnp.int32)
    pos2 = row_off[e2] + route[:, R_RANK2].astype(jnp.int32)
    pos_cols = jnp.pad(jnp.stack([pos1, pos2], axis=1), ((0, 0), (0, LANES - 2)), constant_values=-1)
    xs, gs = _dispatch(xb_t, pieces, pos_cols, disp, n_rows, n_work, tc, ts)
    y = _moe(xs, gs, tile_expert, tile_valid, w_in, w_out, tm, tf)
    return _combine(y, pos_cols, x, g, b, comb, n_work, tc, ts)


def kernel(x, ret_w_in, ret_gn_g, ret_w_out, nsa_w_kv, cmp_k_pe, cmp_k_w1, cmp_k_w2, cmp_v_pe, cmp_v_w1, cmp_v_w2,
           nsa_w_q, nsa_w_out, ffn_w_in, ffn_w_out, moe_router, moe_w_in, moe_w_out, ln_g, ln_b):
    B, S, D = x.shape
    T = B * S
    G, R, dh = NSA_GROUPS, NSA_REP, NSA_HD
    assert ret_w_in.shape[0] == 1 and nsa_w_q.shape[0] == 1 and ln_g.shape[0] == DEPTH
    dk = D // RET_HEADS
    dv = 2 * dk
    n_slc = S // SLC_LEN
    tq = TILES["nsa_queries"]
    assert n_slc <= dh and S % tq == 0 and WIN % tq == 0 and R * 3 <= GATE_ROWS
    rows = lambda name: min(TILES[name], S)

    xf = x.reshape(T, D)

    cos_r, sin_r = _rope_tables(S, dk, dk)
    qkvg = _ret_proj(xf, ret_w_in[0].astype(BF16), cos_r[:, :dk // 2], -sin_r[:, :dk // 2], S, dk,
                     tm=rows("ret_proj"))
    y = _retention(qkvg, ret_gn_g[0], B, S, dk, dv, rows=rows("retention"))
    x1, x1b = _proj_ln(y, ret_w_out[0].astype(BF16), xf, ln_g[0, 0], ln_b[0, 0], tm=rows("proj_ln"))
    x2, x2b = _ffn(x1b, ffn_w_in[0].astype(BF16), ffn_w_out[0].astype(BF16), x1, ln_g[0, 1], ln_b[0, 1],
                   tm=rows("ffn"), tf=TILES["ffn_hidden"])

    cos_n, sin_n = _rope_tables(S, dh, LANES)
    ks_aug, vs, kw, vw, cmp_kv = _kv_proj(x2b, nsa_w_kv.astype(BF16), cos_n, sin_n, S, tm=rows("kv_proj"))
    n_c = S // CMP_STRIDE
    cmp_in = cmp_kv.reshape(2, G * B * n_c, CMP_STRIDE * dh)
    w1 = jnp.stack([cmp_k_w1, cmp_v_w1]).astype(BF16)
    w2 = jnp.stack([cmp_k_w2, cmp_v_w2]).astype(BF16)
    pe = jnp.stack([cmp_k_pe, cmp_v_pe]).reshape(2, 1, CMP_LEN * dh)
    pe = jnp.broadcast_to(pe, (2, 8, CMP_LEN * dh)).astype(BF16)
    kvc = _compress(cmp_in, w1, w2, pe, tm=min(TILES["compress"], G * B * n_c)).reshape(2, G * B, n_c, dh)
    n_blk = -(-n_slc // 32) * 32
    cs = jnp.arange(n_c) * CMP_STRIDE
    ss = jnp.arange(n_blk) * SLC_LEN
    overlap_t = jnp.clip(jnp.minimum(cs[None, :] + CMP_LEN, ss[:, None] + SLC_LEN)
                         - jnp.maximum(cs[None, :], ss[:, None]), 0).astype(F32) / CMP_STRIDE
    overlap_t = jnp.where(jnp.arange(n_blk)[:, None] < n_slc, overlap_t, 0.0).astype(BF16)

    nq_cols = NSA_HEADS * dh
    wq = nsa_w_q[0]
    wq_t = wq[:, :nq_cols].T.astype(BF16)
    wg_t = jnp.pad(wq[:, nq_cols:].T.reshape(G, R * 3, D), ((0, 0), (0, GATE_ROWS - R * 3), (0, 0)))
    wg_t = wg_t.reshape(G * GATE_ROWS, D).astype(BF16)
    ang = _rope_angles(S, dh).T
    q_t, gates_t = _q_proj(x2b, wq_t, wg_t, jnp.cos(ang), jnp.sin(ang), S, tm=rows("q_proj"))
    attn_t = _nsa_attention(q_t, gates_t, kvc[0], kvc[1], ks_aug, vs, kw, vw, overlap_t, B, S, tq)
    x3, x3b_t = _proj_ln(attn_t, nsa_w_out[0].astype(BF16), x2, ln_g[1, 0], ln_b[1, 0], tm=rows("proj_ln"),
                         y_transposed=True, ob_transposed=True)

    out = _moe_layer(x3, x3b_t, moe_router[0], moe_w_in[0].astype(BF16), moe_w_out[0].astype(BF16),
                     ln_g[1, 1], ln_b[1, 1], tc=rows("moe_chunk"), ts=TILES["moe_sorted_rows"],
                     tm=TILES["moe_expert_rows"], tf=TILES["moe_hidden"])
    return out.reshape(B, S, D)
```

```python
import functools

import jax
import jax.numpy as jnp
from jax import lax
from jax.experimental import pallas as pl
from jax.experimental.pallas import tpu as pltpu

F32 = jnp.float32
BF16 = jnp.bfloat16

DEPTH = 2
ROPE_THETA = 10000.0
LN_EPS = 1e-5
DN_ALPHA = (2.0 * DEPTH) ** 0.25

RET_HEADS = 4
RET_CHUNK = 128

NSA_HEADS = 16
NSA_GROUPS = 4
NSA_REP = NSA_HEADS // NSA_GROUPS
NSA_HD = 64
CMP_LEN = 32
CMP_STRIDE = 16
SLC_LEN = 64
SLC_TOPK = 8
WIN = 512
FORCE_SCORE = 1e9

N_EXPERTS = 8

LANES = 128
GATE_ROWS = 16
MASK_NEG = -1e30
LOG2_E = 1.4426950408889634
VMEM_LIMIT = 56 * 1024 * 1024

TILES = {
    "ret_proj": 512, "retention": 512, "proj_ln": 1024, "ffn": 512, "ffn_hidden": 1408,
    "kv_proj": 512, "q_proj": 512, "compress": 1024, "nsa_queries": 256,
    "moe_chunk": 1024, "moe_sorted_rows": 256, "moe_expert_rows": 512, "moe_hidden": 896,
}


def _params(*sem):
    return pltpu.CompilerParams(dimension_semantics=sem, vmem_limit_bytes=VMEM_LIMIT)


def _dot(a, b):
    return jnp.dot(a, b, preferred_element_type=F32)


def _dot_nt(a, b):
    return lax.dot_general(a, b, (((1,), (1,)), ((), ())), preferred_element_type=F32)


def _dot_tn(a, b):
    return lax.dot_general(a, b, (((0,), (0,)), ((), ())), preferred_element_type=F32)


def _layer_norm(z, g, b):
    mu = jnp.mean(z, -1, keepdims=True)
    zc = z - mu
    var = jnp.mean(zc * zc, -1, keepdims=True)
    return zc * lax.rsqrt(var + LN_EPS) * g + b


def _silu(a):
    return a * jax.nn.sigmoid(a)


def _rope_angles(seq, dim):
    inv = 1.0 / (ROPE_THETA ** (jnp.arange(0, dim, 2, dtype=F32) / dim))
    return jnp.arange(seq, dtype=F32)[:, None] * inv[None, :]


def _rope_tables(seq, dim, width):
    ang = _rope_angles(seq, dim)
    cos, sin = jnp.cos(ang), jnp.sin(ang)
    cos_h = jnp.concatenate([cos, cos], -1)
    sin_h = jnp.concatenate([-sin, sin], -1)
    rep = width // dim
    return jnp.tile(cos_h, (1, rep)), jnp.tile(sin_h, (1, rep))


def _ret_proj_kernel(x_ref, w_ref, cos_ref, sin_ref, o_ref, *, dk, tn):
    x = x_ref[...].astype(BF16)
    half = dk // 2
    for j in range(w_ref.shape[1] // tn):
        acc = _dot(x, w_ref[:, j * tn:(j + 1) * tn])
        if j < 2:
            scale = dk ** -0.5 if j == 1 else 1.0
            cos = cos_ref[...] * scale
            sin = sin_ref[...] * scale
            for h in range(tn // dk):
                a = acc[:, h * dk:h * dk + half]
                b = acc[:, h * dk + half:(h + 1) * dk]
                lo = j * tn + h * dk
                o_ref[:, lo:lo + half] = (a * cos - b * sin).astype(o_ref.dtype)
                o_ref[:, lo + half:lo + dk] = (a * sin + b * cos).astype(o_ref.dtype)
        else:
            o_ref[:, j * tn:(j + 1) * tn] = acc.astype(o_ref.dtype)


def _ret_proj(xb, w, cos, sin, seq, dk, tm):
    T, D = xb.shape
    N = w.shape[1]
    tn = RET_HEADS * dk
    half = dk // 2
    n_pos = seq // tm
    return pl.pallas_call(
        functools.partial(_ret_proj_kernel, dk=dk, tn=tn),
        grid=(T // tm,),
        in_specs=[
            pl.BlockSpec((tm, D), lambda i: (i, 0)),
            pl.BlockSpec((D, N), lambda i: (0, 0)),
            pl.BlockSpec((tm, half), lambda i: (i % n_pos, 0)),
            pl.BlockSpec((tm, half), lambda i: (i % n_pos, 0)),
        ],
        out_specs=pl.BlockSpec((tm, N), lambda i: (i, 0)),
        out_shape=jax.ShapeDtypeStruct((T, N), BF16),
        compiler_params=_params("parallel"),
    )(xb, w, cos, sin)


def _retention_kernel(q_ref, k_ref, v_ref, g_ref, dec_ref, qd_ref, kd_ref, gn_ref, o_ref, state_ref,
                      *, chunk, n_chunks):
    @pl.when(pl.program_id(2) == 0)
    def _():
        state_ref[...] = jnp.zeros_like(state_ref)

    decay = dec_ref[0]
    qd = qd_ref[0]
    kd = kd_ref[0]
    cd = qd[chunk - 1:chunk, :]
    gn = gn_ref[0]
    for c in range(n_chunks):
        rows = slice(c * chunk, (c + 1) * chunk)
        for u in range(q_ref.shape[0]):
            q = q_ref[u, rows, :]
            k = k_ref[u, rows, :]
            v = v_ref[u, rows, :]
            scores = _dot_nt(q, k) * decay
            inner = _dot(scores.astype(BF16), v)
            state = state_ref[u]
            cross = _dot(q, state.astype(BF16)) * qd
            out = inner + cross
            k_dec = (k.astype(F32) * kd).astype(BF16)
            state_ref[u] = state * cd + _dot_tn(k_dec, v)
            mu = jnp.mean(out, -1, keepdims=True)
            oc = out - mu
            var = jnp.mean(oc * oc, -1, keepdims=True)
            normed = oc * lax.rsqrt(var + LN_EPS) * gn
            o_ref[u, rows, :] = (_silu(g_ref[u, rows, :].astype(F32)) * normed).astype(o_ref.dtype)


def _retention(qkvg, gn_g, batch, seq, dk, dv, rows):
    H, C = RET_HEADS, RET_CHUNK
    T = batch * seq
    n_steps = seq // rows
    nb = 2 if batch % 2 == 0 else 1
    qkvg = qkvg.reshape(batch, seq, qkvg.shape[1])
    log_gamma = jnp.log1p(-(2.0 ** (-5.0 - jnp.arange(H, dtype=F32))))
    i = jnp.arange(C, dtype=F32)
    rel = i[:, None] - i[None, :]
    intra = jnp.where(rel >= 0, jnp.exp(log_gamma[:, None, None] * jnp.maximum(rel, 0.0)), 0.0)
    q_decay = jnp.exp(log_gamma[:, None] * (i + 1.0))[:, :, None]
    k_decay = jnp.exp(log_gamma[:, None] * (C - 1.0 - i))[:, :, None]
    v_off = (2 * H * dk) // dv
    g_off = (2 * H * dk + H * dv) // dv
    y = pl.pallas_call(
        functools.partial(_retention_kernel, chunk=C, n_chunks=rows // C),
        grid=(batch // nb, H, n_steps),
        in_specs=[
            pl.BlockSpec((nb, rows, dk), lambda b, h, n: (b, n, h)),
            pl.BlockSpec((nb, rows, dk), lambda b, h, n: (b, n, H + h)),
            pl.BlockSpec((nb, rows, dv), lambda b, h, n: (b, n, v_off + h)),
            pl.BlockSpec((nb, rows, dv), lambda b, h, n: (b, n, g_off + h)),
            pl.BlockSpec((1, C, C), lambda b, h, n: (h, 0, 0)),
            pl.BlockSpec((1, C, 1), lambda b, h, n: (h, 0, 0)),
            pl.BlockSpec((1, C, 1), lambda b, h, n: (h, 0, 0)),
            pl.BlockSpec((1, 1, dv), lambda b, h, n: (h, 0, 0)),
        ],
        out_specs=pl.BlockSpec((nb, rows, dv), lambda b, h, n: (b, n, h)),
        out_shape=jax.ShapeDtypeStruct((batch, seq, H * dv), BF16),
        scratch_shapes=[pltpu.VMEM((nb, dk, dv), F32)],
        compiler_params=_params("parallel", "parallel", "arbitrary"),
    )(qkvg, qkvg, qkvg, qkvg, intra, q_decay, k_decay, gn_g.reshape(H, 1, dv))
    return y.reshape(T, H * dv)


def _proj_ln_kernel(y_ref, w_ref, x_ref, g_ref, b_ref, o_ref, ob_ref, *, y_transposed):
    mix = _dot_tn(y_ref[...], w_ref[...]) if y_transposed else _dot(y_ref[...], w_ref[...])
    o = _layer_norm(DN_ALPHA * x_ref[...] + mix, g_ref[...], b_ref[...])
    o_ref[...] = o
    ob_ref[...] = o.astype(BF16)


def _proj_ln(y, w, x, g, b, tm, y_transposed=False):
    K, D = w.shape
    T = x.shape[0]
    y_spec = pl.BlockSpec((K, tm), lambda i: (0, i)) if y_transposed else pl.BlockSpec((tm, K), lambda i: (i, 0))
    return pl.pallas_call(
        functools.partial(_proj_ln_kernel, y_transposed=y_transposed),
        grid=(T // tm,),
        in_specs=[
            y_spec,
            pl.BlockSpec((K, D), lambda i: (0, 0)),
            pl.BlockSpec((tm, D), lambda i: (i, 0)),
            pl.BlockSpec((1, D), lambda i: (0, 0)),
            pl.BlockSpec((1, D), lambda i: (0, 0)),
        ],
        out_specs=[pl.BlockSpec((tm, D), lambda i: (i, 0)), pl.BlockSpec((tm, D), lambda i: (i, 0))],
        out_shape=[jax.ShapeDtypeStruct((T, D), F32), jax.ShapeDtypeStruct((T, D), BF16)],
        compiler_params=_params("parallel"),
    )(y, w, x, g.reshape(1, D), b.reshape(1, D))


def _ffn_kernel(xb_ref, wi_ref, wo_ref, x_ref, g_ref, b_ref, o_ref, ob_ref, *, tf):
    xb = xb_ref[...]
    F = wo_ref.shape[0]
    acc = None
    for f in range(F // tf):
        cols = slice(f * tf, (f + 1) * tf)
        gate_cols = slice(F + f * tf, F + (f + 1) * tf)
        h = (_silu(_dot(xb, wi_ref[:, cols])) * _dot(xb, wi_ref[:, gate_cols])).astype(BF16)
        contrib = _dot(h, wo_ref[cols, :])
        acc = contrib if acc is None else acc + contrib
    o = _layer_norm(DN_ALPHA * x_ref[...] + acc, g_ref[...], b_ref[...])
    o_ref[...] = o
    ob_ref[...] = o.astype(BF16)


def _ffn(xb, w_in, w_out, x, g, b, tm, tf):
    T, D = xb.shape
    F = w_out.shape[0]
    assert F % tf == 0
    return pl.pallas_call(
        functools.partial(_ffn_kernel, tf=tf),
        grid=(T // tm,),
        in_specs=[
            pl.BlockSpec((tm, D), lambda i: (i, 0)),
            pl.BlockSpec((D, 2 * F), lambda i: (0, 0)),
            pl.BlockSpec((F, D), lambda i: (0, 0)),
            pl.BlockSpec((tm, D), lambda i: (i, 0)),
            pl.BlockSpec((1, D), lambda i: (0, 0)),
            pl.BlockSpec((1, D), lambda i: (0, 0)),
        ],
        out_specs=[pl.BlockSpec((tm, D), lambda i: (i, 0)), pl.BlockSpec((tm, D), lambda i: (i, 0))],
        out_shape=[jax.ShapeDtypeStruct((T, D), F32), jax.ShapeDtypeStruct((T, D), BF16)],
        compiler_params=_params("parallel"),
    )(xb, w_in, w_out, x, g.reshape(1, D), b.reshape(1, D))


def _rope64(acc, cos, sin_signed):
    half = NSA_HD // 2
    lane = lax.broadcasted_iota(jnp.int32, acc.shape, 1)
    first = (lane % NSA_HD) < half
    rot = jnp.where(first, pltpu.roll(acc, LANES - half, 1), pltpu.roll(acc, half, 1))
    return acc * cos + rot * sin_signed


def _kv_proj_kernel(x_ref, w_ref, cos_ref, sin_ref, ks_ref, vs_ref, kw_ref, vw_ref, cmp_ref, *, n_pos):
    G, dh = NSA_GROUPS, NSA_HD
    acc = _dot(x_ref[...], w_ref[...])
    tm = acc.shape[0]
    cos, sin = cos_ref[...], sin_ref[...]
    lane = lax.broadcasted_iota(jnp.int32, (tm, LANES), 1)
    low = lane < dh
    pos = (pl.program_id(0) % n_pos) * tm + lax.broadcasted_iota(jnp.int32, (tm, LANES), 0)
    block_onehot = jnp.where(lane - dh == pos // SLC_LEN, 1.0, 0.0)
    ones_lane = jnp.where(lane == dh, 1.0, 0.0)
    swap = lambda a: pltpu.roll(a, dh, 1)

    def slab(branch, s):
        off = branch * G * dh + s * LANES
        return acc[:, off:off + LANES]

    for s in range(G * dh // LANES):
        k_cmp, v_cmp = _rope64(slab(0, s), cos, sin), slab(1, s)
        k_slc, v_slc = _rope64(slab(2, s), cos, sin), slab(3, s)
        k_win, v_win = _rope64(slab(4, s), cos, sin), slab(5, s)
        for h in range(2):
            g = 2 * s + h
            to_low = (lambda a: a) if h == 0 else swap
            ks_ref[g] = jnp.where(low, to_low(k_slc), block_onehot).astype(BF16)
            vs_ref[g] = jnp.where(low, to_low(v_slc), ones_lane).astype(BF16)
            kw_ref[g] = jnp.where(low, to_low(k_win), 0.0).astype(BF16)
            vw_ref[g] = jnp.where(low, to_low(v_win), ones_lane).astype(BF16)
            cmp_ref[0, g] = to_low(k_cmp)[:, :dh].astype(BF16)
            cmp_ref[1, g] = to_low(v_cmp)[:, :dh].astype(BF16)


def _kv_proj(xb, w, cos, sin, seq, tm):
    T, D = xb.shape
    N = w.shape[1]
    G, dh = NSA_GROUPS, NSA_HD
    n_pos = seq // tm
    wide = jax.ShapeDtypeStruct((G, T, LANES), BF16)
    wide_spec = pl.BlockSpec((G, tm, LANES), lambda i: (0, i, 0))
    return pl.pallas_call(
        functools.partial(_kv_proj_kernel, n_pos=n_pos),
        grid=(T // tm,),
        in_specs=[
            pl.BlockSpec((tm, D), lambda i: (i, 0)),
            pl.BlockSpec((D, N), lambda i: (0, 0)),
            pl.BlockSpec((tm, LANES), lambda i: (i % n_pos, 0)),
            pl.BlockSpec((tm, LANES), lambda i: (i % n_pos, 0)),
        ],
        out_specs=[wide_spec, wide_spec, wide_spec, wide_spec, pl.BlockSpec((2, G, tm, dh), lambda i: (0, 0, i, 0))],
        out_shape=[wide, wide, wide, wide, jax.ShapeDtypeStruct((2, G, T, dh), BF16)],
        compiler_params=_params("parallel"),
    )(xb, w, cos, sin)


def _q_proj_kernel(x_ref, wq_ref, wg_ref, cos_ref, sin_ref, q_ref, gate_ref):
    dh = NSA_HD
    half = dh // 2
    x = x_ref[...]
    acc = _dot_nt(wq_ref[...], x)
    cos = cos_ref[...] * (dh ** -0.5 * LOG2_E)
    sin = sin_ref[...] * (dh ** -0.5 * LOG2_E)
    for h in range(acc.shape[0] // dh):
        a = acc[h * dh:h * dh + half]
        b = acc[h * dh + half:(h + 1) * dh]
        q_ref[h * dh:h * dh + half, :] = (a * cos - b * sin).astype(q_ref.dtype)
        q_ref[h * dh + half:(h + 1) * dh, :] = (a * sin + b * cos).astype(q_ref.dtype)
    gate_ref[...] = jax.nn.sigmoid(_dot_nt(wg_ref[...], x))


def _q_proj(xb, wq_t, wg_t, cos_t, sin_t, seq, tm):
    T, D = xb.shape
    nq, ng = wq_t.shape[0], wg_t.shape[0]
    n_pos = seq // tm
    half = NSA_HD // 2
    return pl.pallas_call(
        _q_proj_kernel,
        grid=(T // tm,),
        in_specs=[
            pl.BlockSpec((tm, D), lambda i: (i, 0)),
            pl.BlockSpec((nq, D), lambda i: (0, 0)),
            pl.BlockSpec((ng, D), lambda i: (0, 0)),
            pl.BlockSpec((half, tm), lambda i: (0, i % n_pos)),
            pl.BlockSpec((half, tm), lambda i: (0, i % n_pos)),
        ],
        out_specs=[pl.BlockSpec((nq, tm), lambda i: (0, i)), pl.BlockSpec((ng, tm), lambda i: (0, i))],
        out_shape=[jax.ShapeDtypeStruct((nq, T), BF16), jax.ShapeDtypeStruct((ng, T), F32)],
        compiler_params=_params("parallel"),
    )(xb, wq_t, wg_t, cos_t, sin_t)


def _compress_kernel(a_ref, w1_ref, w2_ref, pe_ref, o_ref):
    a = a_ref[0]
    w1 = w1_ref[0]
    half = w1.shape[0] // 2
    first = _dot(a, w1[:half])
    second = _dot(a, w1[half:])
    bias = _dot(pe_ref[0], w1)[0:1]
    m = a.shape[0]
    hid = first + pltpu.roll(second, m - 1, 0) + bias
    o_ref[0] = _dot(jax.nn.gelu(hid).astype(BF16), w2_ref[0]).astype(o_ref.dtype)


def _compress(a, w1, w2, pe, tm):
    _, rows, width = a.shape
    hid = w1.shape[2]
    dh = w2.shape[2]
    return pl.pallas_call(
        _compress_kernel,
        grid=(2, rows // tm),
        in_specs=[
            pl.BlockSpec((1, tm, width), lambda s, i: (s, i, 0)),
            pl.BlockSpec((1, 2 * width, hid), lambda s, i: (s, 0, 0)),
            pl.BlockSpec((1, hid, dh), lambda s, i: (s, 0, 0)),
            pl.BlockSpec((1, 8, 2 * width), lambda s, i: (s, 0, 0)),
        ],
        out_specs=pl.BlockSpec((1, tm, dh), lambda s, i: (s, i, 0)),
        out_shape=jax.ShapeDtypeStruct((2, rows, dh), BF16),
        compiler_params=_params("parallel", "parallel"),
    )(a, w1, w2, pe)


def _nsa_kernel(q_ref, gt_ref, kc_ref, vc_ref, ks_ref, vs_ref, kw_ref, vw_ref, ov_ref, bc_ref, bl_ref, o_ref,
                *, tq, n_slc, n_top):
    R, dh = NSA_REP, NSA_HD
    i = pl.program_id(2)
    s0 = pl.multiple_of(i * tq, tq)
    q_all = q_ref[...]
    q_heads = [q_all[r * dh:(r + 1) * dh, :] for r in range(R)]
    t_row = s0 + lax.broadcasted_iota(jnp.int32, (1, tq), 1)

    q4 = jnp.concatenate(q_heads, axis=1)
    n_c = kc_ref.shape[1]
    sc = _dot(kc_ref[0], q4)
    t4 = s0 + (lax.broadcasted_iota(jnp.int32, sc.shape, 1) & (tq - 1))
    valid_c = lax.broadcasted_iota(jnp.int32, sc.shape, 0) * CMP_STRIDE + (CMP_LEN - 1) <= t4
    sc = jnp.where(valid_c, sc, -jnp.inf)
    mc = jnp.max(sc, 0, keepdims=True)
    mc = jnp.where(mc == -jnp.inf, 0.0, mc)
    ec = jnp.where(valid_c, jnp.exp2(sc - mc), 0.0)
    p_cmp = ec / jnp.maximum(jnp.sum(ec, 0, keepdims=True), 1e-30)
    o_cmp = _dot_tn(vc_ref[0], p_cmp.astype(BF16))

    p_sum = p_cmp[:, 0:tq]
    for r in range(1, R):
        p_sum = p_sum + p_cmp[:, r * tq:(r + 1) * tq]
    p_hi = p_sum.astype(BF16)
    p_lo = (p_sum - p_hi.astype(F32)).astype(BF16)
    imp = _dot(ov_ref[...], p_hi) + _dot(ov_ref[...], p_lo)
    blk = lax.broadcasted_iota(jnp.int32, imp.shape, 0)
    cur = t_row // SLC_LEN
    forced = (blk == 0) | (blk == cur) | (blk == cur - 1)
    imp = jnp.where(forced, FORCE_SCORE, imp)
    imp = jnp.where(blk <= cur, imp, -jnp.inf)
    rank = jnp.zeros(imp.shape, jnp.int32)
    for b in range(n_slc):
        row = imp[b:b + 1, :]
        wins_tie = jnp.where(blk > b, 1, 0)
        rank = rank + jnp.where(row > imp, 1, jnp.where(row == imp, wins_tie, 0))
    selected = (rank < n_top) & (blk <= cur)
    sel_bias = jnp.where(selected, 0.0, MASK_NEG).astype(BF16)
    if sel_bias.shape[0] < dh:
        sel_bias = jnp.concatenate([sel_bias, jnp.zeros((dh - sel_bias.shape[0], tq), BF16)], axis=0)
    zero_half = jnp.zeros((dh, tq), BF16)
    q_sel = jnp.concatenate([jnp.concatenate([qh, sel_bias], axis=0) for qh in q_heads], axis=1)
    q_win = jnp.concatenate([jnp.concatenate([qh, zero_half], axis=0) for qh in q_heads], axis=1)

    def key_block(ref, blk_idx):
        return ref[0, pl.ds(pl.multiple_of(blk_idx * tq, tq), tq), :]

    def attend(state, k_blk, v_blk, queries, add_bias):
        m, acc = state
        s = _dot(k_blk, queries)
        if add_bias is not None:
            s = add_bias(s)
        m_new = jnp.maximum(m, jnp.max(s, 0, keepdims=True))
        p = jnp.exp2(s - m_new).astype(BF16)
        acc = jnp.exp2(m - m_new) * acc + _dot_tn(v_blk, p)
        return m_new, acc

    init = (jnp.full((1, R * tq), MASK_NEG, F32), jnp.zeros((2 * dh, R * tq), F32))
    causal_bias = bc_ref[...]

    def sel_pair(kb, state):
        k0 = pl.multiple_of(kb * 2 * tq, 2 * tq)
        return attend(state, ks_ref[0, pl.ds(k0, 2 * tq), :], vs_ref[0, pl.ds(k0, 2 * tq), :], q_sel, None)

    st_sel = lax.fori_loop(0, i // 2, sel_pair, init)
    prev_open = jnp.where(i % 2 == 1, 0.0, MASK_NEG)
    prev = jnp.maximum(i - 1, 0)
    k_fin = jnp.concatenate([key_block(ks_ref, prev), key_block(ks_ref, i)], axis=0)
    v_fin = jnp.concatenate([key_block(vs_ref, prev), key_block(vs_ref, i)], axis=0)
    _, acc_sel = attend(
        st_sel, k_fin, v_fin, q_sel,
        lambda s: jnp.concatenate([s[:tq] + prev_open, s[tq:] + causal_bias], axis=0))

    n_back = WIN // tq
    win_blocks = [jnp.maximum(i - back, 0) for back in range(n_back, -1, -1)]
    k_win = jnp.concatenate([key_block(kw_ref, blk_idx) for blk_idx in win_blocks], axis=0)
    v_win = jnp.concatenate([key_block(vw_ref, blk_idx) for blk_idx in win_blocks], axis=0)

    def window_bias(s):
        parts = []
        for n, back in enumerate(range(n_back, -1, -1)):
            blk = s[n * tq:(n + 1) * tq]
            if back == 0:
                parts.append(blk + causal_bias)
            else:
                exists = jnp.where(i >= back, 0.0, MASK_NEG)
                parts.append(blk + (bl_ref[...] + exists) if back == n_back else blk + exists)
        return jnp.concatenate(parts, axis=0)

    _, acc_win = attend(init, k_win, v_win, q_win, window_bias)

    gates = gt_ref[...]
    gate = [jnp.concatenate([gates[3 * r + c:3 * r + c + 1] for r in range(R)], axis=1) for c in range(3)]
    o = (gate[0] * o_cmp + (gate[1] * (1.0 / acc_sel[dh:dh + 1])) * acc_sel[:dh]
         + (gate[2] * (1.0 / acc_win[dh:dh + 1])) * acc_win[:dh])
    for r in range(R):
        o_ref[r * dh:(r + 1) * dh, :] = o[:, r * tq:(r + 1) * tq].astype(o_ref.dtype)


def _nsa_attention(q_t, gates_t, kc, vc, ks_aug, vs, kw, vw, overlap_t, batch, seq, tq):
    G, R, dh = NSA_GROUPS, NSA_REP, NSA_HD
    T = batch * seq
    nq = seq // tq
    n_slc = seq // SLC_LEN
    n_c = kc.shape[1]
    n_blk = overlap_t.shape[0]
    key = lax.broadcasted_iota(jnp.int32, (tq, R * tq), 0)
    qry = lax.broadcasted_iota(jnp.int32, (tq, R * tq), 1) % tq
    causal_bias = jnp.where(key <= qry, 0.0, MASK_NEG).astype(F32)
    lower_bias = jnp.where(key > qry, 0.0, MASK_NEG).astype(F32)
    per_bg = lambda b, g, i: (g * batch + b, 0, 0)
    seq_spec = pl.BlockSpec((1, seq, LANES), lambda b, g, i: (g, b, 0))
    const = lambda b, g, i: (0, 0)
    return pl.pallas_call(
        functools.partial(_nsa_kernel, tq=tq, n_slc=n_slc, n_top=min(SLC_TOPK, n_slc)),
        grid=(batch, G, nq),
        in_specs=[
            pl.BlockSpec((R * dh, tq), lambda b, g, i: (g, b * nq + i)),
            pl.BlockSpec((GATE_ROWS, tq), lambda b, g, i: (g, b * nq + i)),
            pl.BlockSpec((1, n_c, dh), per_bg),
            pl.BlockSpec((1, n_c, dh), per_bg),
            seq_spec, seq_spec, seq_spec, seq_spec,
            pl.BlockSpec((n_blk, n_c), const),
            pl.BlockSpec((tq, R * tq), const),
            pl.BlockSpec((tq, R * tq), const),
        ],
        out_specs=pl.BlockSpec((R * dh, tq), lambda b, g, i: (g, b * nq + i)),
        out_shape=jax.ShapeDtypeStruct((G * R * dh, T), BF16),
        compiler_params=_params("parallel", "parallel", "arbitrary"),
    )(q_t, gates_t, kc, vc, ks_aug, vs, kw, vw, overlap_t, causal_bias, lower_bias)


R_E1, R_E2, R_W1, R_W2, R_RANK1, R_RANK2 = range(6)
P_E1 = 6


def _router_kernel(x_ref, whi_ref, wlo_ref, route_ref, pieces_ref, before_ref, total_ref, cnt_ref):
    @pl.when(pl.program_id(0) == 0)
    def _():
        cnt_ref[...] = jnp.zeros_like(cnt_ref)

    x = x_ref[...]
    x_hi = x.astype(BF16)
    x_lo = (x - x_hi.astype(F32)).astype(BF16)
    logits = _dot(x_hi, whi_ref[...]) + _dot(x_hi, wlo_ref[...]) + _dot(x_lo, whi_ref[...])
    tc = logits.shape[0]
    lane = lax.broadcasted_iota(jnp.int32, logits.shape, 1).astype(F32)
    logits = jnp.where(lane < N_EXPERTS, logits, -jnp.inf)
    m1 = jnp.max(logits, -1, keepdims=True)
    i1 = jnp.min(jnp.where(logits == m1, lane, float(LANES)), -1, keepdims=True)
    rest = jnp.where(lane == i1, -jnp.inf, logits)
    m2 = jnp.max(rest, -1, keepdims=True)
    i2 = jnp.min(jnp.where(rest == m2, lane, float(LANES)), -1, keepdims=True)
    e2 = jnp.exp(m2 - m1)
    den = 1.0 + e2
    chosen = jnp.where(lane == i1, 1.0, jnp.where(lane == i2, 1.0, 0.0))
    earlier = (lax.broadcasted_iota(jnp.int32, (tc, tc), 1) < lax.broadcasted_iota(jnp.int32, (tc, tc), 0))
    before = cnt_ref[0:1, :]
    excl = _dot(jnp.where(earlier, 1.0, 0.0).astype(BF16), chosen.astype(BF16)) + before
    rank1 = jnp.sum(jnp.where(lane == i1, excl, 0.0), -1, keepdims=True)
    rank2 = jnp.sum(jnp.where(lane == i2, excl, 0.0), -1, keepdims=True)
    cols = {R_E1: i1, R_E2: i2, R_W1: 1.0 / den, R_W2: e2 / den, R_RANK1: rank1, R_RANK2: rank2}
    route = jnp.zeros_like(logits)
    for k, v in cols.items():
        route = jnp.where(lane == float(k), v, route)
    route_ref[...] = route
    pieces = jnp.where(lane == float(P_E1), i1, 0.0)
    for slot, weight in enumerate((1.0 / den, e2 / den)):
        rem = weight
        for k in range(3):
            piece = rem.astype(BF16).astype(F32)
            pieces = jnp.where(lane == float(3 * slot + k), piece, pieces)
            rem = rem - piece
    pieces_ref[...] = pieces.astype(BF16)
    before_ref[0] = jnp.broadcast_to(before, before_ref.shape[1:])
    after = before + jnp.sum(chosen, axis=0, keepdims=True)
    cnt_ref[...] = jnp.broadcast_to(after, cnt_ref.shape)
    total_ref[...] = jnp.broadcast_to(after, total_ref.shape)


def _router(x, w_hi, w_lo, tc):
    T, D = x.shape
    return pl.pallas_call(
        _router_kernel,
        grid=(T // tc,),
        in_specs=[
            pl.BlockSpec((tc, D), lambda i: (i, 0)),
            pl.BlockSpec((D, LANES), lambda i: (0, 0)),
            pl.BlockSpec((D, LANES), lambda i: (0, 0)),
        ],
        out_specs=[pl.BlockSpec((tc, LANES), lambda i: (i, 0)),
                   pl.BlockSpec((tc, LANES), lambda i: (i, 0)),
                   pl.BlockSpec((1, 8, LANES), lambda i: (i, 0, 0)),
                   pl.BlockSpec((8, LANES), lambda i: (0, 0))],
        out_shape=[jax.ShapeDtypeStruct((T, LANES), F32),
                   jax.ShapeDtypeStruct((T, LANES), BF16),
                   jax.ShapeDtypeStruct((T // tc, 8, LANES), F32),
                   jax.ShapeDtypeStruct((8, LANES), F32)],
        scratch_shapes=[pltpu.VMEM((8, LANES), F32)],
        compiler_params=_params("arbitrary"),
    )(x, w_hi, w_lo)


def _moe_plan(before, total, n_tok, tc, ts, tm):
    E = N_EXPERTS
    n_c = n_tok // tc
    n_rows = 2 * n_tok + E * tm
    n_sub, n_tiles = n_rows // ts, n_rows // tm
    n_work = E * n_c + n_sub
    i32 = jnp.int32
    tot = total[0, :E].astype(i32)
    cum = jnp.concatenate([before[:, 0, :E].astype(i32).T, tot[:, None]], axis=1)
    tiles_e = (tot + tm - 1) // tm
    tile_end = jnp.cumsum(tiles_e)
    row_off = (tile_end - tiles_e) * tm
    tile_id = jnp.arange(n_tiles, dtype=i32)
    tile_expert = jnp.minimum(jnp.sum(tile_id[:, None] >= tile_end[None, :], axis=1), E - 1).astype(i32)
    tile_valid = (tile_id < tile_end[-1]).astype(i32)
    sub = jnp.arange(n_sub, dtype=i32)
    sub_e = tile_expert[(sub * ts) // tm]
    rel0 = sub * ts - row_off[sub_e]
    rel1 = rel0 + ts
    cum_s = cum[sub_e]
    c_lo = jnp.sum(cum_s[:, 1:] <= rel0[:, None], axis=1).astype(i32)
    c_hi = jnp.sum(cum_s[:, :-1] < rel1[:, None], axis=1).astype(i32) - 1
    c_lo = jnp.clip(c_lo, 0, n_c - 1)
    c_hi = jnp.clip(c_hi, c_lo, n_c - 1)
    n_j = c_hi - c_lo + 1
    ends = jnp.cumsum(n_j)
    starts = ends - n_j
    n_valid = ends[-1]
    w = jnp.arange(n_work, dtype=i32)
    valid = w < n_valid
    jw = jnp.minimum(jnp.sum(w[:, None] >= ends[None, :], axis=1), n_sub - 1).astype(i32)
    cw = jnp.where(valid, c_lo[jw] + (w - starts[jw]), c_hi[n_sub - 1]).astype(i32)
    first = valid & (w == starts[jw])
    last = valid & (w == ends[jw] - 1)
    disp = (jw, cw, valid.astype(i32), first.astype(i32), last.astype(i32), sub_e[jw].astype(i32))
    order = jnp.argsort(jnp.where(valid, cw * n_sub + jw, n_c * n_sub + w))
    cj, cc, cv = jw[order], cw[order], valid[order]
    last_j, last_c = cj[n_valid - 1], cc[n_valid - 1]
    cj = jnp.where(cv, cj, last_j)
    cc = jnp.where(cv, cc, last_c)
    prev_c = jnp.concatenate([jnp.full((1,), -1, i32), cc[:-1]])
    next_c = jnp.concatenate([cc[1:], jnp.full((1,), -1, i32)])
    next_v = jnp.concatenate([cv[1:], jnp.zeros((1,), bool)])
    cfirst = cv & (cc != prev_c)
    clast = cv & ((cc != next_c) | ~next_v)
    comb = (cj.astype(i32), cc.astype(i32), cv.astype(i32), cfirst.astype(i32), clast.astype(i32))
    return row_off, tile_expert, tile_valid, disp, comb, n_rows, n_work


def _dispatch_kernel(jw, cw, vw, fw, lw, ew, x_ref, pc_ref, pos_ref, xs_ref, gs_ref, acc_ref, gacc_ref):
    w = pl.program_id(0)

    @pl.when(vw[w] == 1)
    def _():
        ts = acc_ref.shape[0]
        row_id = jw[w] * ts + lax.broadcasted_iota(jnp.int32, (ts, 1), 0)
        pos = pos_ref[0]
        hit = jnp.where(pos[0:1, :] == row_id, 1.0,
                        jnp.where(pos[1:2, :] == row_id, 1.0, 0.0)).astype(BF16)
        rows = _dot(hit, x_ref[...])
        side = _dot(hit, pc_ref[...])

        @pl.when(fw[w] == 1)
        def _():
            acc_ref[...] = rows
            gacc_ref[...] = side

        @pl.when(fw[w] == 0)
        def _():
            acc_ref[...] += rows
            gacc_ref[...] += side

        @pl.when(lw[w] == 1)
        def _():
            xs_ref[...] = acc_ref[...].astype(xs_ref.dtype)
            side_all = gacc_ref[...]
            lane = lax.broadcasted_iota(jnp.int32, side_all.shape, 1)
            lane_sum = lambda keep: jnp.sum(jnp.where(keep, side_all, 0.0), -1, keepdims=True)
            weight1 = lane_sum(lane < 3)
            weight2 = lane_sum((lane >= 3) & (lane < 6))
            expert1 = lane_sum(lane == P_E1)
            gate = jnp.where(expert1 == ew[w].astype(F32), weight1, weight2)
            gs_ref[...] = jnp.broadcast_to(gate, gs_ref.shape)


def _dispatch(xb, pieces, pos_rows, disp, n_rows, n_work, tc, ts):
    T, D = xb.shape
    by_chunk = lambda w, jw, cw, vw, fw, lw, ew: (cw[w], 0)
    by_sub = lambda w, jw, cw, vw, fw, lw, ew: (jw[w], 0)
    grid_spec = pltpu.PrefetchScalarGridSpec(
        num_scalar_prefetch=6,
        grid=(n_work,),
        in_specs=[
            pl.BlockSpec((tc, D), by_chunk),
            pl.BlockSpec((tc, LANES), by_chunk),
            pl.BlockSpec((1, 8, tc), lambda w, jw, cw, vw, fw, lw, ew: (cw[w], 0, 0)),
        ],
        out_specs=[pl.BlockSpec((ts, D), by_sub), pl.BlockSpec((ts, LANES), by_sub)],
        scratch_shapes=[pltpu.VMEM((ts, D), F32), pltpu.VMEM((ts, LANES), F32)],
    )
    return pl.pallas_call(
        _dispatch_kernel,
        grid_spec=grid_spec,
        out_shape=[jax.ShapeDtypeStruct((n_rows, D), BF16), jax.ShapeDtypeStruct((n_rows, LANES), F32)],
        compiler_params=_params("arbitrary"),
    )(*disp, xb, pieces, pos_rows)


def _moe_kernel(te, tv, xs_ref, gs_ref, wa_ref, wb_ref, wo_ref, y_ref, acc_ref):
    i = pl.program_id(0)
    f = pl.program_id(1)
    last_f = pl.num_programs(1) - 1

    @pl.when(tv[i] == 1)
    def _():
        xs = xs_ref[...]
        h = (_silu(_dot(xs, wa_ref[0])) * _dot(xs, wb_ref[0]) * gs_ref[:, 0:1]).astype(BF16)
        contrib = _dot(h, wo_ref[0])

        @pl.when(f == 0)
        def _():
            acc_ref[...] = contrib

        @pl.when(f > 0)
        def _():
            acc_ref[...] += contrib

        @pl.when(f == last_f)
        def _():
            y_ref[...] = acc_ref[...].astype(y_ref.dtype)

    @pl.when((tv[i] == 0) & (f == last_f))
    def _():
        y_ref[...] = jnp.zeros_like(y_ref)


def _moe(xs, gs, tile_expert, tile_valid, w_in, w_out, tm, tf):
    n_rows, D = xs.shape
    E, F, _ = w_out.shape
    nf = F // tf
    f_of = lambda i, f, te, tv: jnp.where(tv[i] == 1, f, nf - 1)
    grid_spec = pltpu.PrefetchScalarGridSpec(
        num_scalar_prefetch=2,
        grid=(n_rows // tm, nf),
        in_specs=[
            pl.BlockSpec((tm, D), lambda i, f, te, tv: (i, 0)),
            pl.BlockSpec((tm, LANES), lambda i, f, te, tv: (i, 0)),
            pl.BlockSpec((1, D, tf), lambda i, f, te, tv: (te[i], 0, f_of(i, f, te, tv))),
            pl.BlockSpec((1, D, tf), lambda i, f, te, tv: (te[i], 0, nf + f_of(i, f, te, tv))),
            pl.BlockSpec((1, tf, D), lambda i, f, te, tv: (te[i], f_of(i, f, te, tv), 0)),
        ],
        out_specs=pl.BlockSpec((tm, D), lambda i, f, te, tv: (i, 0)),
        scratch_shapes=[pltpu.VMEM((tm, D), F32)],
    )
    return pl.pallas_call(
        _moe_kernel,
        grid_spec=grid_spec,
        out_shape=jax.ShapeDtypeStruct((n_rows, D), BF16),
        compiler_params=_params("parallel", "arbitrary"),
    )(tile_expert, tile_valid, xs, gs, w_in, w_in, w_out)


def _combine_kernel(cj, cc, cv, cf, cl, y_ref, pos_ref, x_ref, g_ref, b_ref, o_ref, acc_ref):
    w = pl.program_id(0)

    @pl.when(cv[w] == 1)
    def _():
        ts = y_ref.shape[0]
        col_id = cj[w] * ts + lax.broadcasted_iota(jnp.int32, (1, ts), 1)
        pos = pos_ref[...]
        hit = (jnp.where(pos[:, 0:1] == col_id, 1.0, 0.0).astype(BF16)
               + jnp.where(pos[:, 1:2] == col_id, 1.0, 0.0).astype(BF16))
        contrib = _dot(hit, y_ref[...])

        @pl.when(cf[w] == 1)
        def _():
            acc_ref[...] = contrib

        @pl.when(cf[w] == 0)
        def _():
            acc_ref[...] += contrib

        @pl.when(cl[w] == 1)
        def _():
            o_ref[...] = _layer_norm(DN_ALPHA * x_ref[...] + acc_ref[...], g_ref[...], b_ref[...])


def _combine(y, pos_cols, x, g, b, comb, n_work, tc, ts):
    T, D = x.shape
    grid_spec = pltpu.PrefetchScalarGridSpec(
        num_scalar_prefetch=5,
        grid=(n_work,),
        in_specs=[
            pl.BlockSpec((ts, D), lambda w, cj, cc, cv, cf, cl: (cj[w], 0)),
            pl.BlockSpec((tc, LANES), lambda w, cj, cc, cv, cf, cl: (cc[w], 0)),
            pl.BlockSpec((tc, D), lambda w, cj, cc, cv, cf, cl: (cc[w], 0)),
            pl.BlockSpec((1, D), lambda w, cj, cc, cv, cf, cl: (0, 0)),
            pl.BlockSpec((1, D), lambda w, cj, cc, cv, cf, cl: (0, 0)),
        ],
        out_specs=pl.BlockSpec((tc, D), lambda w, cj, cc, cv, cf, cl: (cc[w], 0)),
        scratch_shapes=[pltpu.VMEM((tc, D), F32)],
    )
    return pl.pallas_call(
        _combine_kernel,
        grid_spec=grid_spec,
        out_shape=jax.ShapeDtypeStruct((T, D), F32),
        compiler_params=_params("arbitrary"),
    )(*comb, y, pos_cols, x, g.reshape(1, D), b.reshape(1, D))


def _moe_layer(x, xb, w_router, w_in, w_out, g, b, tc, ts, tm, tf):
    T, D = x.shape
    wr = jnp.pad(w_router, ((0, 0), (0, LANES - N_EXPERTS)))
    wr_hi = wr.astype(BF16)
    wr_lo = (wr - wr_hi.astype(F32)).astype(BF16)
    route, pieces, before, total = _router(x, wr_hi, wr_lo, tc)
    row_off, tile_expert, tile_valid, disp, comb, n_rows, n_work = _moe_plan(before, total, T, tc, ts, tm)
    e1 = route[:, R_E1].astype(jnp.int32)
    e2 = route[:, R_E2].astype(jnp.int32)
    pos1 = row_off[e1] + route[:, R_RANK1].astype(jnp.int32)
    pos2 = row_off[e2] + route[:, R_RANK2].astype(jnp.int32)
    pos = jnp.stack([pos1, pos2])
    pos_rows = jnp.pad(pos.reshape(2, T // tc, tc).transpose(1, 0, 2), ((0, 0), (0, 6), (0, 0)),
                       constant_values=-1)
    pos_cols = jnp.pad(pos.T, ((0, 0), (0, LANES - 2)), constant_values=-1)
    xs, gs = _dispatch(xb, pieces, pos_rows, disp, n_rows, n_work, tc, ts)
    y = _moe(xs, gs, tile_expert, tile_valid, w_in, w_out, tm, tf)
    return _combine(y, pos_cols, x, g, b, comb, n_work, tc, ts)


def kernel(x, ret_w_in, ret_gn_g, ret_w_out, nsa_w_kv, cmp_k_pe, cmp_k_w1, cmp_k_w2, cmp_v_pe, cmp_v_w1, cmp_v_w2,
           nsa_w_q, nsa_w_out, ffn_w_in, ffn_w_out, moe_router, moe_w_in, moe_w_out, ln_g, ln_b):
    B, S, D = x.shape
    T = B * S
    G, R, dh = NSA_GROUPS, NSA_REP, NSA_HD
    assert ret_w_in.shape[0] == 1 and nsa_w_q.shape[0] == 1 and ln_g.shape[0] == DEPTH
    dk = D // RET_HEADS
    dv = 2 * dk
    n_slc = S // SLC_LEN
    tq = TILES["nsa_queries"]
    assert n_slc <= dh and S % tq == 0 and WIN % tq == 0 and R * 3 <= GATE_ROWS
    rows = lambda name: min(TILES[name], S)

    xf = x.reshape(T, D)

    cos_r, sin_r = _rope_tables(S, dk, dk)
    qkvg = _ret_proj(xf, ret_w_in[0].astype(BF16), cos_r[:, :dk // 2], -sin_r[:, :dk // 2], S, dk,
                     tm=rows("ret_proj"))
    y = _retention(qkvg, ret_gn_g[0], B, S, dk, dv, rows=rows("retention"))
    x1, x1b = _proj_ln(y, ret_w_out[0].astype(BF16), xf, ln_g[0, 0], ln_b[0, 0], tm=rows("proj_ln"))
    x2, x2b = _ffn(x1b, ffn_w_in[0].astype(BF16), ffn_w_out[0].astype(BF16), x1, ln_g[0, 1], ln_b[0, 1],
                   tm=rows("ffn"), tf=TILES["ffn_hidden"])

    cos_n, sin_n = _rope_tables(S, dh, LANES)
    ks_aug, vs, kw, vw, cmp_kv = _kv_proj(x2b, nsa_w_kv.astype(BF16), cos_n, sin_n, S, tm=rows("kv_proj"))
    n_c = S // CMP_STRIDE
    cmp_in = cmp_kv.reshape(2, G * B * n_c, CMP_STRIDE * dh)
    w1 = jnp.stack([cmp_k_w1, cmp_v_w1]).astype(BF16)
    w2 = jnp.stack([cmp_k_w2, cmp_v_w2]).astype(BF16)
    pe = jnp.stack([cmp_k_pe, cmp_v_pe]).reshape(2, 1, CMP_LEN * dh)
    pe = jnp.broadcast_to(pe, (2, 8, CMP_LEN * dh)).astype(BF16)
    kvc = _compress(cmp_in, w1, w2, pe, tm=min(TILES["compress"], G * B * n_c)).reshape(2, G * B, n_c, dh)
    n_blk = -(-n_slc // 32) * 32
    cs = jnp.arange(n_c) * CMP_STRIDE
    ss = jnp.arange(n_blk) * SLC_LEN
    overlap_t = jnp.clip(jnp.minimum(cs[None, :] + CMP_LEN, ss[:, None] + SLC_LEN)
                         - jnp.maximum(cs[None, :], ss[:, None]), 0).astype(F32) / CMP_STRIDE
    overlap_t = jnp.where(jnp.arange(n_blk)[:, None] < n_slc, overlap_t, 0.0).astype(BF16)

    nq_cols = NSA_HEADS * dh
    wq = nsa_w_q[0]
    wq_t = wq[:, :nq_cols].T.astype(BF16)
    wg_t = jnp.pad(wq[:, nq_cols:].T.reshape(G, R * 3, D), ((0, 0), (0, GATE_ROWS - R * 3), (0, 0)))
    wg_t = wg_t.reshape(G * GATE_ROWS, D).astype(BF16)
    ang = _rope_angles(S, dh).T
    q_t, gates_t = _q_proj(x2b, wq_t, wg_t, jnp.cos(ang), jnp.sin(ang), S, tm=rows("q_proj"))
    attn_t = _nsa_attention(q_t, gates_t, kvc[0], kvc[1], ks_aug, vs, kw, vw, overlap_t, B, S, tq)
    x3, x3b = _proj_ln(attn_t, nsa_w_out[0].astype(BF16), x2, ln_g[1, 0], ln_b[1, 0], tm=rows("proj_ln"),
                       y_transposed=True)

    out = _moe_layer(x3, x3b, moe_router[0], moe_w_in[0].astype(BF16), moe_w_out[0].astype(BF16),
                     ln_g[1, 1], ln_b[1, 1], tc=rows("moe_chunk"), ts=TILES["moe_sorted_rows"],
                     tm=TILES["moe_expert_rows"], tf=TILES["moe_hidden"])
    return out.reshape(B, S, D)
```
